```python
import jax, jax.numpy as jnp
from jax import lax
import numpy as np

D_MODEL = 1024
BATCH = 16
SEQ = 4096
DEPTH = 1
DEC_BATCH = 128
DEC_SEQ = 8
PAST_LEN = 8192
PAGE_SIZE = 128

HEAD_DIM = 64
FOX_HEADS = 8
NSA_HEADS = 8
NSA_GROUPS = 2
NSA_HPG = NSA_HEADS // NSA_GROUPS
FOX_WIDTH = FOX_HEADS * HEAD_DIM
NSA_WIDTH = NSA_HEADS * HEAD_DIM
NSA_KV_WIDTH = NSA_GROUPS * HEAD_DIM
N_NSA_BRANCHES = 3
CMP_LEN = 32
CMP_STRIDE = 16
CMP_HIDDEN = 128
SLC_LEN = 64
N_SELECT = 16
WINDOW = 512
D_FF = 2816
ROPE_THETA = 10000.0
Q_BLOCK = 128
NSA_Q_BLOCK = 32
LN_EPS = 1e-5
FORCE_BONUS = 1e4
NEG_INF = -1e30
FORGET_BIAS_INIT = 2.0
ALPHA = (2.0 * DEPTH) ** 0.25
BETA = (8.0 * DEPTH) ** -0.25
SPLIT_SIZES = (('fox_q', FOX_WIDTH), ('fox_k', FOX_WIDTH), ('fox_v', FOX_WIDTH), ('fox_f', FOX_HEADS),
               ('nsa_q', NSA_WIDTH), ('k_cmp', NSA_KV_WIDTH), ('v_cmp', NSA_KV_WIDTH),
               ('k_slc', NSA_KV_WIDTH), ('v_slc', NSA_KV_WIDTH), ('k_win', NSA_KV_WIDTH), ('v_win', NSA_KV_WIDTH),
               ('nsa_gate', N_NSA_BRANCHES * NSA_HEADS), ('gate_fox', D_MODEL), ('gate_nsa', D_MODEL))
D_IN_PROJ = 3 * FOX_WIDTH + FOX_HEADS + NSA_WIDTH + 6 * NSA_KV_WIDTH + N_NSA_BRANCHES * NSA_HEADS + 2 * D_MODEL

kernel_name = 'fox_nsa_gated_macaron_deepnorm_step'


def layer_norm(x, g, b):
    xf = x.astype(jnp.float32)
    mu = jnp.mean(xf, axis=-1, keepdims=True)
    var = jnp.mean(jnp.square(xf - mu), axis=-1, keepdims=True)
    return ((xf - mu) * lax.rsqrt(var + LN_EPS) * g.astype(jnp.float32) + b.astype(jnp.float32)).astype(x.dtype)


def swiglu(h, w_up, w_down):
    a, b = jnp.split(h @ w_up, 2, axis=-1)
    return (jax.nn.silu(a) * b) @ w_down


def macaron_half_ffn(x, w_up, w_down, g, b):
    return layer_norm(ALPHA * x + 0.5 * swiglu(x, w_up, w_down), g, b)


def rope(x, pos):
    half = HEAD_DIM // 2
    inv_freq = ROPE_THETA ** (-jnp.arange(half, dtype=jnp.float32) / half)
    ang = pos.astype(jnp.float32)[:, None] * inv_freq[None, :]
    cos = jnp.cos(ang)[:, None, :]
    sin = jnp.sin(ang)[:, None, :]
    xf = x.astype(jnp.float32)
    x1, x2 = xf[..., :half], xf[..., half:]
    return jnp.concatenate([x1 * cos - x2 * sin, x2 * cos + x1 * sin], axis=-1).astype(x.dtype)


def masked_softmax(s, mask):
    s = jnp.where(mask, s, NEG_INF)
    m = jnp.max(s, axis=-1, keepdims=True)
    e = jnp.where(mask, jnp.exp(s - m), 0.0)
    return e / jnp.maximum(jnp.sum(e, axis=-1, keepdims=True), 1e-30)


def project_mixer_inputs(h, w_in, b_fgate, pos):
    B, T, _ = h.shape
    cols = h @ w_in
    p, start = {}, 0
    for name, size in SPLIT_SIZES:
        p[name] = cols[..., start:start + size]
        start += size

    def heads(a, n):
        return a.reshape(B, T, n, HEAD_DIM)

    return {
        'fox_q': heads(p['fox_q'], FOX_HEADS),
        'fox_k': heads(p['fox_k'], FOX_HEADS),
        'fox_v': heads(p['fox_v'], FOX_HEADS),
        'logf': jax.nn.log_sigmoid(p['fox_f'].astype(jnp.float32) + b_fgate.astype(jnp.float32)),
        'nsa_q': rope(heads(p['nsa_q'], NSA_HEADS), pos).reshape(B, T, NSA_GROUPS, NSA_HPG, HEAD_DIM),
        'k_cmp': rope(heads(p['k_cmp'], NSA_GROUPS), pos),
        'v_cmp': heads(p['v_cmp'], NSA_GROUPS),
        'k_slc': rope(heads(p['k_slc'], NSA_GROUPS), pos),
        'v_slc': heads(p['v_slc'], NSA_GROUPS),
        'k_win': rope(heads(p['k_win'], NSA_GROUPS), pos),
        'v_win': heads(p['v_win'], NSA_GROUPS),
        'nsa_gate': p['nsa_gate'].reshape(B, T, N_NSA_BRANCHES, NSA_GROUPS, NSA_HPG),
        'gate_fox': p['gate_fox'],
        'gate_nsa': p['gate_nsa'],
    }


def fox_attend(q, k, v, c_q, c_k, q_pos, k_pos):
    s = jnp.einsum('bqhd,bkhd->bhqk', q, k, preferred_element_type=jnp.float32) * (HEAD_DIM ** -0.5)
    s = s + jnp.swapaxes(c_q, 1, 2)[..., :, None] - jnp.swapaxes(c_k, 1, 2)[..., None, :]
    p = masked_softmax(s, k_pos[None, :] <= q_pos[:, None])
    o = jnp.einsum('bhqk,bkhd->bqhd', p.astype(v.dtype), v)
    return o.reshape(o.shape[0], o.shape[1], -1)


def fox_prompt(q, k, v, logf):
    B, T = q.shape[:2]
    c = jnp.cumsum(logf, axis=1)
    k_pos = jnp.arange(T)

    def block(i):
        t0 = i * Q_BLOCK
        qb = lax.dynamic_slice_in_dim(q, t0, Q_BLOCK, axis=1)
        cb = lax.dynamic_slice_in_dim(c, t0, Q_BLOCK, axis=1)
        return fox_attend(qb, k, v, cb, c, t0 + jnp.arange(Q_BLOCK), k_pos)

    o = lax.map(block, jnp.arange(T // Q_BLOCK))
    return o.transpose(1, 0, 2, 3).reshape(B, T, FOX_WIDTH)


def compress_blocks(rows, pos_emb, w1, w2):
    B, T, G, D = rows.shape
    ratio = CMP_LEN // CMP_STRIDE
    n_sub = T // CMP_STRIDE
    n_cmp = n_sub - ratio + 1
    sub = rows[:, :n_sub * CMP_STRIDE].reshape(B, n_sub, CMP_STRIDE, G, D)
    blocks = jnp.concatenate([sub[:, r:r + n_cmp] for r in range(ratio)], axis=2)
    blocks = blocks + pos_emb[:, None, :].astype(rows.dtype)
    flat = blocks.transpose(0, 1, 3, 2, 4).reshape(B, n_cmp, G, CMP_LEN * D)
    return jax.nn.silu(flat @ w1) @ w2


def cmp_slc_coverage(n_cmp, n_slc):
    c0 = jnp.arange(n_cmp) * CMP_STRIDE
    s0 = jnp.arange(n_slc) * SLC_LEN
    lo = jnp.maximum(c0[:, None], s0[None, :])
    hi = jnp.minimum(c0[:, None] + CMP_LEN, s0[None, :] + SLC_LEN)
    return jnp.maximum(hi - lo, 0).astype(jnp.float32) / CMP_LEN


def to_slc_blocks(rows, n_slc):
    B, T, G, D = rows.shape
    rows = jnp.pad(rows, ((0, 0), (0, n_slc * SLC_LEN - T), (0, 0), (0, 0)))
    return rows.reshape(B, n_slc, SLC_LEN, G, D).transpose(0, 3, 1, 2, 4)


def nsa_context(kc_rows, vc_rows, ks_rows, vs_rows, cmp_w):
    pos_k, wk1, wk2, pos_v, wv1, wv2 = cmp_w
    T = kc_rows.shape[1]
    kc = compress_blocks(kc_rows, pos_k, wk1, wk2)
    vc = compress_blocks(vc_rows, pos_v, wv1, wv2)
    n_cmp = kc.shape[1]
    cmp_end = jnp.arange(n_cmp) * CMP_STRIDE + CMP_LEN - 1
    n_slc = -(-T // SLC_LEN)
    return (kc, vc, cmp_end, cmp_slc_coverage(n_cmp, n_slc), to_slc_blocks(ks_rows, n_slc), to_slc_blocks(vs_rows, n_slc))


def nsa_attend(q, gates, kc, vc, cmp_end, coverage, ksb, vsb, kw, vw, kw_pos, q_pos):
    B, Tq = q.shape[:2]
    scale = HEAD_DIM ** -0.5
    s_c = jnp.einsum('bqgrd,bngd->bgrqn', q, kc, preferred_element_type=jnp.float32) * scale
    p_c = masked_softmax(s_c, cmp_end[None, :] <= q_pos[:, None])
    o_c = jnp.einsum('bgrqn,bngd->bqgrd', p_c.astype(vc.dtype), vc)
    imp = jnp.einsum('bgrqn,ns->bgqs', p_c, coverage)
    n_slc = ksb.shape[2]
    n_sel = min(N_SELECT, n_slc)
    blk = jnp.arange(n_slc)
    cur = q_pos // SLC_LEN
    forced = (blk[None, :] == 0) | (blk[None, :] == cur[:, None]) | (blk[None, :] == cur[:, None] - 1)
    valid = blk[None, :] * SLC_LEN <= q_pos[:, None]
    imp = jnp.where(valid, jnp.where(forced, imp + FORCE_BONUS, imp), NEG_INF)
    _, idx = lax.top_k(imp, n_sel)
    take = jax.vmap(jax.vmap(lambda blocks, ids: blocks[ids]))
    k_sel = take(ksb, idx).reshape(B, NSA_GROUPS, Tq, n_sel * SLC_LEN, HEAD_DIM)
    v_sel = take(vsb, idx).reshape(B, NSA_GROUPS, Tq, n_sel * SLC_LEN, HEAD_DIM)
    sel_pos = (idx[..., None] * SLC_LEN + jnp.arange(SLC_LEN)).reshape(B, NSA_GROUPS, Tq, n_sel * SLC_LEN)
    s_s = jnp.einsum('bqgrd,bgqkd->bgrqk', q, k_sel, preferred_element_type=jnp.float32) * scale
    p_s = masked_softmax(s_s, (sel_pos <= q_pos[None, None, :, None])[:, :, None])
    o_s = jnp.einsum('bgrqk,bgqkd->bqgrd', p_s.astype(v_sel.dtype), v_sel)
    s_w = jnp.einsum('bqgrd,bkgd->bgrqk', q, kw, preferred_element_type=jnp.float32) * scale
    rel = q_pos[:, None] - kw_pos[None, :]
    p_w = masked_softmax(s_w, (rel >= 0) & (rel < WINDOW) & (kw_pos[None, :] >= 0))
    o_w = jnp.einsum('bgrqk,bkgd->bqgrd', p_w.astype(vw.dtype), vw)
    g = jax.nn.sigmoid(gates.astype(jnp.float32)).astype(q.dtype)[..., None]
    o = g[:, :, 0] * o_c + g[:, :, 1] * o_s + g[:, :, 2] * o_w
    return o.reshape(B, Tq, NSA_WIDTH)


def nsa_prompt(q, gates, ctx, kw_rows, vw_rows):
    B, T = q.shape[:2]
    pad = ((0, 0), (WINDOW, 0), (0, 0), (0, 0))
    kwp = jnp.pad(kw_rows, pad)
    vwp = jnp.pad(vw_rows, pad)

    def block(i):
        t0 = i * NSA_Q_BLOCK
        qb = lax.dynamic_slice_in_dim(q, t0, NSA_Q_BLOCK, axis=1)
        gb = lax.dynamic_slice_in_dim(gates, t0, NSA_Q_BLOCK, axis=1)
        kwb = lax.dynamic_slice_in_dim(kwp, t0, WINDOW + NSA_Q_BLOCK, axis=1)
        vwb = lax.dynamic_slice_in_dim(vwp, t0, WINDOW + NSA_Q_BLOCK, axis=1)
        kw_pos = t0 - WINDOW + jnp.arange(WINDOW + NSA_Q_BLOCK)
        return nsa_attend(qb, gb, *ctx, kwb, vwb, kw_pos, t0 + jnp.arange(NSA_Q_BLOCK))

    o = lax.map(block, jnp.arange(T // NSA_Q_BLOCK))
    return o.transpose(1, 0, 2, 3).reshape(B, T, NSA_WIDTH)


def sample_mixers(m, cache_fox_kv, cache_fox_logf, cache_nsa_kv, win_state, page_table, layer, cmp_w):
    past_len = page_table.shape[1] * PAGE_SIZE
    n_new = m['fox_q'].shape[1]
    win_keep = win_state.shape[1]
    q_pos = past_len + jnp.arange(n_new)
    k_pos = jnp.arange(past_len + n_new)
    w_pos = past_len - win_keep + jnp.arange(win_keep + n_new)

    def gather_rows(pool, pages):
        return pool[layer, pages].reshape((past_len,) + pool.shape[3:])

    def one(args):
        fq, fk, fv, logf, nq, gates, kc, vc, ks, vs, kw, vw, win, pages = args
        fox_past = gather_rows(cache_fox_kv, pages).astype(fk.dtype)
        k_all = jnp.concatenate([fox_past[:, 0], fk], axis=0)[None]
        v_all = jnp.concatenate([fox_past[:, 1], fv], axis=0)[None]
        c = jnp.cumsum(jnp.concatenate([gather_rows(cache_fox_logf, pages).astype(jnp.float32), logf], axis=0), axis=0)[None]
        fox_o = fox_attend(fq[None], k_all, v_all, c[:, past_len:], c, q_pos, k_pos)
        nsa_past = gather_rows(cache_nsa_kv, pages).astype(kc.dtype)
        kc_all = jnp.concatenate([nsa_past[:, 0], kc], axis=0)[None]
        vc_all = jnp.concatenate([nsa_past[:, 1], vc], axis=0)[None]
        ks_all = jnp.concatenate([nsa_past[:, 2], ks], axis=0)[None]
        vs_all = jnp.concatenate([nsa_past[:, 3], vs], axis=0)[None]
        ctx = nsa_context(kc_all, vc_all, ks_all, vs_all, cmp_w)
        win = win.astype(kw.dtype)
        kw_all = jnp.concatenate([win[:, 0], kw], axis=0)[None]
        vw_all = jnp.concatenate([win[:, 1], vw], axis=0)[None]
        nsa_o = nsa_attend(nq[None], gates[None], *ctx, kw_all, vw_all, w_pos, q_pos)
        return fox_o[0], nsa_o[0]

    xs = (m['fox_q'], m['fox_k'], m['fox_v'], m['logf'], m['nsa_q'], m['nsa_gate'],
          m['k_cmp'], m['v_cmp'], m['k_slc'], m['v_slc'], m['k_win'], m['v_win'], win_state, page_table)
    return lax.map(one, xs)


def merge_branches(fox_o, nsa_o, gate_fox, gate_nsa, w_up_fox, w_up_nsa, w_out):
    gf = jax.nn.sigmoid(gate_fox.astype(jnp.float32)).astype(fox_o.dtype)
    gn = jax.nn.sigmoid(gate_nsa.astype(jnp.float32)).astype(nsa_o.dtype)
    return (gf * (fox_o @ w_up_fox) + gn * (nsa_o @ w_up_nsa)) @ w_out


def last_rows(rows, n):
    T = rows.shape[1]
    if T >= n:
        return rows[:, T - n:]
    return jnp.pad(rows, [(0, 0), (n - T, 0)] + [(0, 0)] * (rows.ndim - 2))


def setup_inputs(seed: int = 0) -> dict:
    key = jax.random.key(seed)
    ks = jax.random.split(key, 32)
    f32 = jnp.float32
    n_pages = PAST_LEN // PAGE_SIZE
    n_used = DEC_BATCH * n_pages
    n_pool = (n_used * 5) // 4
    win_keep = min(WINDOW, PAST_LEN)

    def normal(k, shape, scale):
        return jax.random.normal(k, shape, f32) * scale

    def gain(k):
        return 1.0 + normal(k, (DEPTH, D_MODEL), 0.05)

    def bias(k):
        return normal(k, (DEPTH, D_MODEL), 0.02)

    page_table = jax.random.permutation(ks[0], n_pool)[:n_used].reshape(DEC_BATCH, n_pages).astype(jnp.int32)
    return {
        'x_prompt': normal(ks[1], (BATCH, SEQ, D_MODEL), 1.0),
        'x_sample': normal(ks[2], (DEC_BATCH, DEC_SEQ, D_MODEL), 1.0),
        'cache_fox_kv': normal(ks[3], (DEPTH, n_pool, PAGE_SIZE, 2, FOX_HEADS, HEAD_DIM), 1.0),
        'cache_fox_logf': jax.nn.log_sigmoid(FORGET_BIAS_INIT + normal(ks[4], (DEPTH, n_pool, PAGE_SIZE, FOX_HEADS), 1.0)),
        'cache_nsa_kv': normal(ks[5], (DEPTH, n_pool, PAGE_SIZE, 4, NSA_GROUPS, HEAD_DIM), 1.0),
        'state_win_kv': normal(ks[6], (DEPTH, DEC_BATCH, win_keep, 2, NSA_GROUPS, HEAD_DIM), 1.0),
        'page_table': page_table,
        'ln1_g': gain(ks[7]),
        'ln1_b': bias(ks[8]),
        'ffn1_w_up': normal(ks[9], (DEPTH, D_MODEL, 2 * D_FF), D_MODEL ** -0.5),
        'ffn1_w_down': normal(ks[10], (DEPTH, D_FF, D_MODEL), BETA * D_FF ** -0.5),
        'w_in': normal(ks[11], (DEPTH, D_MODEL, D_IN_PROJ), D_MODEL ** -0.5),
        'b_fgate': FORGET_BIAS_INIT + normal(ks[12], (DEPTH, FOX_HEADS), 0.1),
        'cmp_pos_k': normal(ks[13], (DEPTH, CMP_LEN, HEAD_DIM), 0.1),
        'cmp_wk1': normal(ks[14], (DEPTH, CMP_LEN * HEAD_DIM, CMP_HIDDEN), (CMP_LEN * HEAD_DIM) ** -0.5),
        'cmp_wk2': normal(ks[15], (DEPTH, CMP_HIDDEN, HEAD_DIM), CMP_HIDDEN ** -0.5),
        'cmp_pos_v': normal(ks[16], (DEPTH, CMP_LEN, HEAD_DIM), 0.1),
        'cmp_wv1': normal(ks[17], (DEPTH, CMP_LEN * HEAD_DIM, CMP_HIDDEN), (CMP_LEN * HEAD_DIM) ** -0.5),
        'cmp_wv2': normal(ks[18], (DEPTH, CMP_HIDDEN, HEAD_DIM), CMP_HIDDEN ** -0.5),
        'w_up_fox': normal(ks[19], (DEPTH, FOX_WIDTH, D_MODEL), BETA * FOX_WIDTH ** -0.5),
        'w_up_nsa': normal(ks[20], (DEPTH, NSA_WIDTH, D_MODEL), BETA * NSA_WIDTH ** -0.5),
        'w_out': normal(ks[21], (DEPTH, D_MODEL, D_MODEL), BETA * D_MODEL ** -0.5),
        'ln2_g': gain(ks[22]),
        'ln2_b': bias(ks[23]),
        'ffn2_w_up': normal(ks[24], (DEPTH, D_MODEL, 2 * D_FF), D_MODEL ** -0.5),
        'ffn2_w_down': normal(ks[25], (DEPTH, D_FF, D_MODEL), BETA * D_FF ** -0.5),
        'ln3_g': gain(ks[26]),
        'ln3_b': bias(ks[27]),
    }


def reference(x_prompt, x_sample, cache_fox_kv, cache_fox_logf, cache_nsa_kv, state_win_kv, page_table,
              ln1_g, ln1_b, ffn1_w_up, ffn1_w_down, w_in, b_fgate,
              cmp_pos_k, cmp_wk1, cmp_wk2, cmp_pos_v, cmp_wv1, cmp_wv2,
              w_up_fox, w_up_nsa, w_out, ln2_g, ln2_b, ffn2_w_up, ffn2_w_down, ln3_g, ln3_b):
    win_keep = state_win_kv.shape[2]
    past_len = page_table.shape[1] * PAGE_SIZE
    pos_p = jnp.arange(x_prompt.shape[1], dtype=jnp.int32)
    pos_s = past_len + jnp.arange(x_sample.shape[1], dtype=jnp.int32)
    xp, xs = x_prompt, x_sample
    fox_kv_p, fox_kv_s, logf_p, logf_s, nsa_kv_p, nsa_kv_s, win_p, win_s = [], [], [], [], [], [], [], []
    for layer in range(DEPTH):
        cmp_w = (cmp_pos_k[layer], cmp_wk1[layer], cmp_wk2[layer], cmp_pos_v[layer], cmp_wv1[layer], cmp_wv2[layer])
        xp = macaron_half_ffn(xp, ffn1_w_up[layer], ffn1_w_down[layer], ln1_g[layer], ln1_b[layer])
        xs = macaron_half_ffn(xs, ffn1_w_up[layer], ffn1_w_down[layer], ln1_g[layer], ln1_b[layer])
        mp = project_mixer_inputs(xp, w_in[layer], b_fgate[layer], pos_p)
        fox_o = fox_prompt(mp['fox_q'], mp['fox_k'], mp['fox_v'], mp['logf'])
        ctx = nsa_context(mp['k_cmp'], mp['v_cmp'], mp['k_slc'], mp['v_slc'], cmp_w)
        nsa_o = nsa_prompt(mp['nsa_q'], mp['nsa_gate'], ctx, mp['k_win'], mp['v_win'])
        mix_p = merge_branches(fox_o, nsa_o, mp['gate_fox'], mp['gate_nsa'], w_up_fox[layer], w_up_nsa[layer], w_out[layer])
        xp = layer_norm(ALPHA * xp + mix_p, ln2_g[layer], ln2_b[layer])
        fox_kv_p.append(jnp.stack([mp['fox_k'], mp['fox_v']], axis=2))
        logf_p.append(mp['logf'])
        nsa_kv_p.append(jnp.stack([mp['k_cmp'], mp['v_cmp'], mp['k_slc'], mp['v_slc']], axis=2))
        win_p.append(last_rows(jnp.stack([mp['k_win'], mp['v_win']], axis=2), win_keep))
        ms = project_mixer_inputs(xs, w_in[layer], b_fgate[layer], pos_s)
        fox_o_s, nsa_o_s = sample_mixers(ms, cache_fox_kv, cache_fox_logf, cache_nsa_kv, state_win_kv[layer], page_table, layer, cmp_w)
        mix_s = merge_branches(fox_o_s, nsa_o_s, ms['gate_fox'], ms['gate_nsa'], w_up_fox[layer], w_up_nsa[layer], w_out[layer])
        xs = layer_norm(ALPHA * xs + mix_s, ln2_g[layer], ln2_b[layer])
        fox_kv_s.append(jnp.stack([ms['fox_k'], ms['fox_v']], axis=2))
        logf_s.append(ms['logf'])
        nsa_kv_s.append(jnp.stack([ms['k_cmp'], ms['v_cmp'], ms['k_slc'], ms['v_slc']], axis=2))
        new_win_rows = jnp.stack([ms['k_win'], ms['v_win']], axis=2).astype(state_win_kv.dtype)
        win_s.append(jnp.concatenate([state_win_kv[layer], new_win_rows], axis=1)[:, -win_keep:])
        xp = macaron_half_ffn(xp, ffn2_w_up[layer], ffn2_w_down[layer], ln3_g[layer], ln3_b[layer])
        xs = macaron_half_ffn(xs, ffn2_w_up[layer], ffn2_w_down[layer], ln3_g[layer], ln3_b[layer])
    return (xp, xs, jnp.stack(fox_kv_p), jnp.stack(fox_kv_s), jnp.stack(logf_p), jnp.stack(logf_s),
            jnp.stack(nsa_kv_p), jnp.stack(nsa_kv_s), jnp.stack(win_p), jnp.stack(win_s))
```

```python
import functools

import numpy as np
import jax
import jax.numpy as jnp
from jax import lax
from jax.experimental import pallas as pl
from jax.experimental.pallas import tpu as pltpu

F32 = jnp.float32
BF16 = jnp.bfloat16

HEAD_DIM = 64
FOX_HEADS = 8
NSA_HEADS = 8
NSA_GROUPS = 2
NSA_HPG = NSA_HEADS // NSA_GROUPS
FOX_WIDTH = FOX_HEADS * HEAD_DIM
NSA_WIDTH = NSA_HEADS * HEAD_DIM
NSA_KV_WIDTH = NSA_GROUPS * HEAD_DIM
N_NSA_BRANCHES = 3
CMP_LEN = 32
CMP_STRIDE = 16
CMP_HIDDEN = 128
SLC_LEN = 64
N_SELECT = 16
WINDOW = 512
PAGE_SIZE = 128
ROPE_THETA = 10000.0
LN_EPS = 1e-5
FORCE_BONUS = 1e4
NEG_INF = -1e30
SCALE = HEAD_DIM ** -0.5

LANES = 128
N_PAIRS = 4
MIB = 1024 * 1024
HIGHEST = lax.Precision.HIGHEST
NT_DIMS = (((1,), (1,)), ((), ()))


def _params(semantics, vmem_mib):
    return pltpu.CompilerParams(dimension_semantics=semantics, vmem_limit_bytes=vmem_mib * MIB)


def _pick(n, candidates):
    for c in candidates:
        if n % c == 0:
            return c
    return n


def _layer_norm(y, g, b):
    mu = jnp.mean(y, axis=-1, keepdims=True)
    d = y - mu
    var = jnp.mean(d * d, axis=-1, keepdims=True)
    return d * lax.rsqrt(var + LN_EPS) * g + b


def _const_spec(shape):
    nd = len(shape)
    return pl.BlockSpec(shape, lambda *_: (0,) * nd, pipeline_mode=pl.Buffered(1))


def _ffn_ln_kernel(x_ref, wa_ref, wb_ref, wd_ref, g_ref, b_ref, o_ref, acc_ref, *, alpha, n_chunks):
    x = x_ref[...]
    xb = x.astype(BF16)
    acc_ref[...] = jnp.zeros_like(acc_ref)

    def body(c, carry):
        a = jnp.dot(xb, wa_ref[c], preferred_element_type=F32)
        b = jnp.dot(xb, wb_ref[c], preferred_element_type=F32)
        h = (a * jax.nn.sigmoid(a) * b).astype(BF16)
        acc_ref[...] += jnp.dot(h, wd_ref[c], preferred_element_type=F32)
        return carry

    lax.fori_loop(0, n_chunks, body, 0)
    o_ref[...] = _layer_norm(alpha * x + 0.5 * acc_ref[...], g_ref[...], b_ref[...])


def _ffn_ln(x, w_up, w_down, g, b, alpha, name):
    n, d = x.shape
    f = w_down.shape[0]
    fc = _pick(f, (256, 128))
    nc = f // fc
    wa = w_up[:, :f].astype(BF16).reshape(d, nc, fc).transpose(1, 0, 2)
    wb = w_up[:, f:].astype(BF16).reshape(d, nc, fc).transpose(1, 0, 2)
    wd = w_down.astype(BF16).reshape(nc, fc, d)
    tm = _pick(n, (512, 256, 128, 8))
    return pl.pallas_call(
        functools.partial(_ffn_ln_kernel, alpha=alpha, n_chunks=nc),
        grid=(n // tm,),
        in_specs=[pl.BlockSpec((tm, d), lambda i: (i, 0)),
                  _const_spec((nc, d, fc)), _const_spec((nc, d, fc)), _const_spec((nc, fc, d)),
                  _const_spec((1, d)), _const_spec((1, d))],
        out_specs=pl.BlockSpec((tm, d), lambda i: (i, 0)),
        out_shape=jax.ShapeDtypeStruct((n, d), F32),
        scratch_shapes=[pltpu.VMEM((tm, d), F32)],
        compiler_params=_params(("parallel",), 56),
        name=name,
    )(x, wa, wb, wd, g.reshape(1, d), b.reshape(1, d))


def _rope_tables(pos):
    half = HEAD_DIM // 2
    inv_freq = ROPE_THETA ** (-jnp.arange(half, dtype=F32) / half)
    ang = pos.astype(F32)[:, None] * inv_freq[None, :]
    cos, sin = jnp.cos(ang), jnp.sin(ang)
    cos64 = jnp.concatenate([cos, cos], axis=-1)
    sin64 = jnp.concatenate([-sin, sin], axis=-1)
    return jnp.tile(cos64, (1, LANES // HEAD_DIM)), jnp.tile(sin64, (1, LANES // HEAD_DIM))


def _in_proj_kernel(x_ref, wfq, wfkv, wsm, wnq, wnkv, wwin, wgf, wgn, bf_ref, cos_ref, sin_ref,
                    fq_o, fkv_o, fk_o, fv_o, lf_o, nq_o, nkv_o, ks_o, vs_o, win_o, kw_o, vw_o,
                    ng_o, gf_o, gn_o):
    xb = x_ref[...].astype(BF16)
    cos = cos_ref[...]
    sin = sin_ref[...]
    lane = lax.broadcasted_iota(jnp.int32, (1, LANES), 1)
    low_half = (lane % HEAD_DIM) < (HEAD_DIM // 2)

    def rope(v):
        partner = jnp.where(low_half, pltpu.roll(v, LANES - HEAD_DIM // 2, 1), pltpu.roll(v, HEAD_DIM // 2, 1))
        return v * cos + partner * sin

    def proj(w_ref):
        return jnp.dot(xb, w_ref[...], preferred_element_type=F32)

    fq_o[...] = (proj(wfq) * SCALE).astype(BF16)
    fkv = proj(wfkv)
    fkv_o[...] = fkv
    fk_o[...] = fkv[:, :FOX_WIDTH].astype(BF16)
    fv_o[...] = fkv[:, FOX_WIDTH:].astype(BF16)

    sm = proj(wsm)
    z = sm[:, :LANES] + bf_ref[...]
    logf = jnp.minimum(z, 0.0) - jnp.log1p(jnp.exp(-jnp.abs(z)))
    lf_o[...] = logf[:, :FOX_HEADS]
    ng_o[...] = sm[:, LANES:]

    nq = proj(wnq)
    for r in range(N_PAIRS):
        nq_o[:, r * LANES:(r + 1) * LANES] = (rope(nq[:, r * LANES:(r + 1) * LANES]) * SCALE).astype(BF16)

    nkv = proj(wnkv)
    k_cmp = rope(nkv[:, 0:LANES])
    k_slc = rope(nkv[:, 2 * LANES:3 * LANES])
    v_slc = nkv[:, 3 * LANES:4 * LANES]
    nkv_o[:, 0:LANES] = k_cmp
    nkv_o[:, LANES:2 * LANES] = nkv[:, LANES:2 * LANES]
    nkv_o[:, 2 * LANES:3 * LANES] = k_slc
    nkv_o[:, 3 * LANES:4 * LANES] = v_slc
    ks_o[...] = k_slc.astype(BF16)
    vs_o[...] = v_slc.astype(BF16)

    win = proj(wwin)
    k_win = rope(win[:, :LANES])
    v_win = win[:, LANES:]
    win_o[:, :LANES] = k_win
    win_o[:, LANES:] = v_win
    kw_o[...] = k_win.astype(BF16)
    vw_o[...] = v_win.astype(BF16)

    gf_o[...] = proj(wgf)
    gn_o[...] = proj(wgn)


def _nsa_perm():
    perm = np.zeros(NSA_WIDTH, np.int32)
    for r in range(NSA_HPG):
        for g in range(NSA_GROUPS):
            for d in range(HEAD_DIM):
                perm[r * LANES + g * HEAD_DIM + d] = (g * NSA_HPG + r) * HEAD_DIM + d
    return perm


def _split_w_in(w_in, b_fgate):
    d = w_in.shape[0]
    sizes = (FOX_WIDTH, FOX_WIDTH, FOX_WIDTH, FOX_HEADS, NSA_WIDTH) + (NSA_KV_WIDTH,) * 6 + (
        N_NSA_BRANCHES * NSA_HEADS, d, d)
    offs = np.concatenate([[0], np.cumsum(sizes)])
    col = lambda i, j=None: w_in[:, offs[i]:offs[(i if j is None else j) + 1]]
    pad = lambda w: jnp.pad(w, ((0, 0), (0, LANES - w.shape[1])))
    ws = dict(
        wfq=col(0), wfkv=col(1, 2),
        wsm=jnp.concatenate([pad(col(3)), pad(col(11))], axis=1),
        wnq=col(4)[:, _nsa_perm()], wnkv=col(5, 8), wwin=col(9, 10), wgf=col(12), wgn=col(13))
    ws = {k: v.astype(BF16) for k, v in ws.items()}
    bf = jnp.pad(b_fgate.astype(F32), (0, LANES - FOX_HEADS)).reshape(1, LANES)
    return ws, bf


def _in_proj(x, ws, bf, cos_tab, sin_tab, name):
    n, d = x.shape
    n_tab = cos_tab.shape[0]
    tm = _pick(n_tab, (256, 128, 8))
    tab_tiles = n_tab // tm
    row = lambda w: pl.BlockSpec((tm, w), lambda i: (i, 0))
    tab = pl.BlockSpec((tm, LANES), lambda i: (i % tab_tiles, 0))
    names = ("wfq", "wfkv", "wsm", "wnq", "wnkv", "wwin", "wgf", "wgn")
    outs = [("fq", FOX_WIDTH, BF16), ("fkv", 2 * FOX_WIDTH, F32), ("fk", FOX_WIDTH, BF16), ("fv", FOX_WIDTH, BF16),
            ("lf", FOX_HEADS, F32), ("nq", NSA_WIDTH, BF16), ("nkv", 4 * NSA_KV_WIDTH, F32),
            ("ks", NSA_KV_WIDTH, BF16), ("vs", NSA_KV_WIDTH, BF16), ("win", 2 * NSA_KV_WIDTH, F32),
            ("kw", NSA_KV_WIDTH, BF16), ("vw", NSA_KV_WIDTH, BF16), ("ng", LANES, F32), ("gf", d, F32), ("gn", d, F32)]
    res = pl.pallas_call(
        _in_proj_kernel,
        grid=(n // tm,),
        in_specs=[row(d)] + [_const_spec(ws[k].shape) for k in names] + [_const_spec((1, LANES)), tab, tab],
        out_specs=[row(w) for _, w, _ in outs],
        out_shape=[jax.ShapeDtypeStruct((n, w), dt) for _, w, dt in outs],
        compiler_params=_params(("parallel",), 56),
        name=name,
    )(x, *[ws[k] for k in names], bf, cos_tab, sin_tab)
    return {k: v for (k, _, _), v in zip(outs, res)}


def _cumsum_kernel(x_ref, tri_ref, low_ref, o_ref):
    x = x_ref[0]
    within = lax.dot_general(x, tri_ref[...], (((1,), (0,)), ((), ())), precision=HIGHEST, preferred_element_type=F32)
    tot = jnp.broadcast_to(within[:, LANES - 1:LANES], within.shape)
    before = lax.dot_general(low_ref[...], tot, (((1,), (0,)), ((), ())), precision=HIGHEST, preferred_element_type=F32)
    o_ref[0] = within + before


def _cumsum_time(logf_t, name):
    b, h, t = logf_t.shape
    r = t // LANES
    rows = h * r
    idx = np.arange(LANES)
    tri = jnp.asarray((idx[:, None] <= idx[None, :]).astype(np.float32))
    ridx = np.arange(rows)
    low = jnp.asarray(((ridx[None, :] < ridx[:, None]) & (ridx[None, :] // r == ridx[:, None] // r)).astype(np.float32))
    out = pl.pallas_call(
        _cumsum_kernel,
        grid=(b,),
        in_specs=[pl.BlockSpec((1, rows, LANES), lambda i: (i, 0, 0)), _const_spec((LANES, LANES)),
                  _const_spec((rows, rows))],
        out_specs=pl.BlockSpec((1, rows, LANES), lambda i: (i, 0, 0)),
        out_shape=jax.ShapeDtypeStruct((b, rows, LANES), F32),
        compiler_params=_params(("parallel",), 32),
        name=name,
    )(logf_t.reshape(b, rows, LANES), tri, low)
    return out.reshape(b, h, t)


def _k_tile(mode, qi, j, *, tq, tk, nkt, q_off, k_off):
    q_max = q_off + (qi + 1) * tq - 1
    if mode == "win":
        kt = jnp.floor_divide(q_off + qi * tq - (WINDOW - 1) - k_off, tk) + j
        valid = (kt >= 0) & (kt < nkt) & (k_off + kt * tk <= q_max)
        return jnp.clip(kt, 0, nkt - 1), kt, valid
    last = jnp.clip(jnp.floor_divide(q_max - k_off, tk), 0, nkt - 1)
    return jnp.minimum(j, last), j, (k_off + j * tk) <= q_max


def _attn_kernel(*refs, mode, tq, tk, nk, nkt, q_off, k_off, nbp):
    if mode == "fox":
        q_ref, k_ref, v_ref, cc_ref, cr_ref, o_ref, m_ref, l_ref, acc_ref = refs
    elif mode == "slc":
        q_ref, k_ref, v_ref, sel_ref, o_ref, m_ref, l_ref, acc_ref = refs
    else:
        q_ref, k_ref, v_ref, o_ref, m_ref, l_ref, acc_ref = refs
    qi = pl.program_id(2)
    j = pl.program_id(3)
    lane = lax.broadcasted_iota(jnp.int32, (1, LANES), 1)
    left = lane < HEAD_DIM

    @pl.when(j == 0)
    def _():
        m_ref[...] = jnp.full_like(m_ref, NEG_INF)
        l_ref[...] = jnp.zeros_like(l_ref)
        acc_ref[...] = jnp.zeros_like(acc_ref)

    _, kt, valid = _k_tile(mode, qi, j, tq=tq, tk=tk, nkt=nkt, q_off=q_off, k_off=k_off)

    @pl.when(valid)
    def _():
        q = q_ref[0]
        k = k_ref[0]
        v = v_ref[0]
        q_pos = q_off + qi * tq + lax.broadcasted_iota(jnp.int32, (tq, 1), 0)
        k_idx = kt * tk + lax.broadcasted_iota(jnp.int32, (1, tk), 1)
        k_pos = k_off + k_idx
        if mode == "win":
            rel = q_pos - k_pos
            mask = (rel >= 0) & (rel < WINDOW)
        else:
            mask = k_pos <= q_pos
        if mode == "slc":
            blk_row = lax.broadcasted_iota(jnp.int32, (nbp, tk), 0)
            blk_key = (kt * tk + lax.broadcasted_iota(jnp.int32, (nbp, tk), 1)) // SLC_LEN
            expand = (blk_row == blk_key).astype(BF16)
        for hh in range(2):
            qh = jnp.where(left if hh == 0 else jnp.logical_not(left), q, jnp.zeros_like(q))
            s = lax.dot_general(qh, k, NT_DIMS, preferred_element_type=F32)
            keep = mask
            if mode == "fox":
                s = s + (cc_ref[0, 0][:, hh:hh + 1] - cr_ref[0, 0][hh:hh + 1, :])
            if mode == "slc":
                picked = jnp.dot(sel_ref[0][:, hh * nbp:(hh + 1) * nbp], expand, preferred_element_type=F32)
                keep = mask & (picked > 0.5)
            s = jnp.where(keep, s, NEG_INF)
            m_prev = m_ref[hh]
            m_new = jnp.maximum(m_prev, jnp.max(s, axis=1, keepdims=True))
            p = jnp.where(keep, jnp.exp(s - m_new), 0.0)
            alpha = jnp.exp(m_prev - m_new)
            l_ref[hh] = alpha * l_ref[hh] + jnp.sum(p, axis=1, keepdims=True)
            acc_ref[hh] = alpha * acc_ref[hh] + jnp.dot(p.astype(BF16), v, preferred_element_type=F32)
            m_ref[hh] = m_new

    @pl.when(j == nk - 1)
    def _():
        o0 = acc_ref[0] / jnp.maximum(l_ref[0], 1e-30)
        o1 = acc_ref[1] / jnp.maximum(l_ref[1], 1e-30)
        o_ref[0] = jnp.where(left, o0, o1)


def _attention(mode, q, k, v, *, tq, tk, q_off, k_off, extra=(), nbp=0, name):
    b, t_q, _ = q.shape
    t_k, lk = k.shape[1], k.shape[2]
    nq, nkt = t_q // tq, t_k // tk
    if mode == "win":
        if (q_off - k_off) % tk == 0 and tq % tk == 0:
            nk = min(nkt, -(-(WINDOW - 1) // tk) + tq // tk)
        else:
            nk = min(nkt, (WINDOW + tq - 2) // tk + 2)
    else:
        nk = nkt
    kt_of = functools.partial(_k_tile, mode, tq=tq, tk=tk, nkt=nkt, q_off=q_off, k_off=k_off)
    kv_lane = (lambda p: p) if lk == N_PAIRS * LANES else (lambda p: 0)
    q_spec = pl.BlockSpec((1, tq, LANES), lambda bi, p, qi, j: (bi, qi, p))
    kv_spec = pl.BlockSpec((1, tk, LANES), lambda bi, p, qi, j: (bi, kt_of(qi, j)[0], kv_lane(p)))
    in_specs = [q_spec, kv_spec, kv_spec]
    if mode == "fox":
        in_specs += [pl.BlockSpec((1, 1, tq, 2), lambda bi, p, qi, j: (bi, p, qi, 0)),
                     pl.BlockSpec((1, 1, 2, tk), lambda bi, p, qi, j: (bi, p, 0, kt_of(qi, j)[0]))]
    elif mode == "slc":
        in_specs += [pl.BlockSpec((1, tq, 2 * nbp), lambda bi, p, qi, j: (bi, qi, 0))]
    return pl.pallas_call(
        functools.partial(_attn_kernel, mode=mode, tq=tq, tk=tk, nk=nk, nkt=nkt, q_off=q_off, k_off=k_off, nbp=nbp),
        grid=(b, N_PAIRS, nq, nk),
        in_specs=in_specs,
        out_specs=pl.BlockSpec((1, tq, LANES), lambda bi, p, qi, j: (bi, qi, p)),
        out_shape=jax.ShapeDtypeStruct((b, t_q, N_PAIRS * LANES), F32),
        scratch_shapes=[pltpu.VMEM((2, tq, 1), F32), pltpu.VMEM((2, tq, 1), F32), pltpu.VMEM((2, tq, LANES), F32)],
        compiler_params=_params(("parallel", "parallel", "parallel", "arbitrary"), 48),
        name=name,
    )(q, k, v, *extra)


def _compress_kernel(u_ref, pa_ref, pb_ref, wa_ref, wb_ref, w2_ref, o_ref, *, ns):
    u = u_ref[0]
    first = jnp.dot((u + pa_ref[...]).astype(BF16), wa_ref[...], preferred_element_type=F32)
    second = jnp.dot((u + pb_ref[...]).astype(BF16), wb_ref[...], preferred_element_type=F32)
    pre = first + pltpu.roll(second, ns - 1, 0)
    h = (pre * jax.nn.sigmoid(pre)).astype(BF16)
    o_ref[0] = jnp.dot(h, w2_ref[...], preferred_element_type=F32).astype(BF16)


def _compress_weights(pos, w1, w2):
    ratio = CMP_LEN // CMP_STRIDE
    assert ratio == 2
    w1r = w1.reshape(CMP_LEN, HEAD_DIM, CMP_HIDDEN)
    zeros = jnp.zeros((CMP_STRIDE, HEAD_DIM, CMP_HIDDEN), w1.dtype)

    def half(rows):
        g0 = jnp.concatenate([jnp.stack([rows, zeros], axis=1).reshape(-1, CMP_HIDDEN),
                              jnp.stack([zeros, rows], axis=1).reshape(-1, CMP_HIDDEN)], axis=1)
        return g0.astype(BF16)

    def pos_tab(p):
        return jnp.stack([p, p], axis=1).reshape(1, -1).astype(F32)

    z2 = jnp.zeros_like(w2)
    w2d = jnp.concatenate([jnp.concatenate([w2, z2], axis=1), jnp.concatenate([z2, w2], axis=1)], axis=0)
    return (pos_tab(pos[:CMP_STRIDE]), pos_tab(pos[CMP_STRIDE:]), half(w1r[:CMP_STRIDE]), half(w1r[CMP_STRIDE:]),
            w2d.astype(BF16))


def _compress(rows, weights, name):
    b, t, _ = rows.shape
    ns = t // CMP_STRIDE
    width = CMP_STRIDE * LANES
    pa, pb, wa, wb, w2d = weights
    return pl.pallas_call(
        functools.partial(_compress_kernel, ns=ns),
        grid=(b,),
        in_specs=[pl.BlockSpec((1, ns, width), lambda i: (i, 0, 0)), _const_spec((1, width)), _const_spec((1, width)),
                  _const_spec(wa.shape), _const_spec(wb.shape), _const_spec(w2d.shape)],
        out_specs=pl.BlockSpec((1, ns, LANES), lambda i: (i, 0, 0)),
        out_shape=jax.ShapeDtypeStruct((b, ns, LANES), BF16),
        compiler_params=_params(("parallel",), 40),
        name=name,
    )(rows.reshape(b, ns, width), pa, pb, wa, wb, w2d)


def _cmp_select_kernel(q_ref, kc_ref, vc_ref, cov_ref, oc_ref, sel_ref, vt_ref, *,
                       tq, tqp, nc, n_cmp, n_slc, nbp, q_off, n_sel):
    qi = pl.program_id(1)
    lane = lax.broadcasted_iota(jnp.int32, (1, LANES), 1)
    left = lane < HEAD_DIM
    q_pos = q_off + qi * tq + lax.broadcasted_iota(jnp.int32, (tq, 1), 0)
    n_idx = lax.broadcasted_iota(jnp.int32, (1, nc), 1)
    cmask = ((n_idx * CMP_STRIDE + CMP_LEN - 1) <= q_pos) & (n_idx < n_cmp)
    kc = kc_ref[0]
    vc = vc_ref[0]
    cov = cov_ref[...]
    imp = [jnp.zeros((tq, nbp), F32) for _ in range(NSA_GROUPS)]
    for r in range(N_PAIRS):
        qp = q_ref[0, :, r * LANES:(r + 1) * LANES]
        halves = []
        for g in range(NSA_GROUPS):
            qh = jnp.where(left if g == 0 else jnp.logical_not(left), qp, jnp.zeros_like(qp))
            s = lax.dot_general(qh, kc, NT_DIMS, preferred_element_type=F32)
            s = jnp.where(cmask, s, NEG_INF)
            m = jnp.max(s, axis=1, keepdims=True)
            e = jnp.where(cmask, jnp.exp(s - m), 0.0)
            pb = (e / jnp.maximum(jnp.sum(e, axis=1, keepdims=True), 1e-30)).astype(BF16)
            halves.append(jnp.dot(pb, vc, preferred_element_type=F32))
            imp[g] = imp[g] + jnp.dot(pb, cov, preferred_element_type=F32)
        oc_ref[0, :, r * LANES:(r + 1) * LANES] = jnp.where(left, halves[0], halves[1])

    blk = lax.broadcasted_iota(jnp.int32, (1, nbp), 1)
    cur = q_pos // SLC_LEN
    forced = (blk == 0) | (blk == cur) | (blk == cur - 1)
    valid = (blk * SLC_LEN <= q_pos) & (blk < n_slc)
    row = lax.broadcasted_iota(jnp.int32, (nbp, tqp), 0)
    for g in range(NSA_GROUPS):
        val = jnp.where(valid, jnp.where(forced, imp[g] + FORCE_BONUS, imp[g]), NEG_INF)
        if tqp > tq:
            val = jnp.concatenate([val, jnp.zeros((tqp - tq, nbp), F32)], axis=0)
        vt_ref[...] = val.T

        def body(i, cnt):
            vi = vt_ref[pl.ds(i, 1), :]
            vt = vt_ref[...]
            ahead = (vi > vt) | ((vi == vt) & (i < row))
            return cnt + ahead.astype(F32)

        cnt = lax.fori_loop(0, n_slc, body, jnp.zeros((nbp, tqp), F32))
        picked = (cnt < n_sel).astype(F32).T
        sel_ref[0, :, g * nbp:(g + 1) * nbp] = picked[:tq].astype(BF16)


def _coverage(n_cmp, nc, n_slc, nbp):
    c0 = np.arange(nc) * CMP_STRIDE
    s0 = np.arange(nbp) * SLC_LEN
    lo = np.maximum(c0[:, None], s0[None, :])
    hi = np.minimum(c0[:, None] + CMP_LEN, s0[None, :] + SLC_LEN)
    cov = np.maximum(hi - lo, 0).astype(np.float32) / CMP_LEN
    cov[n_cmp:, :] = 0.0
    cov[:, n_slc:] = 0.0
    return jnp.asarray(cov, dtype=BF16)


def _cmp_select(q, kc, vc, *, tq, n_cmp, n_slc, nbp, q_off, name):
    b, t_q, _ = q.shape
    nc = kc.shape[1]
    tqp = max(tq, LANES)
    cov = _coverage(n_cmp, nc, n_slc, nbp)
    kern = functools.partial(_cmp_select_kernel, tq=tq, tqp=tqp, nc=nc, n_cmp=n_cmp, n_slc=n_slc, nbp=nbp,
                             q_off=q_off, n_sel=min(N_SELECT, n_slc))
    return pl.pallas_call(
        kern,
        grid=(b, t_q // tq),
        in_specs=[pl.BlockSpec((1, tq, N_PAIRS * LANES), lambda bi, qi: (bi, qi, 0)),
                  pl.BlockSpec((1, nc, LANES), lambda bi, qi: (bi, 0, 0)),
                  pl.BlockSpec((1, nc, LANES), lambda bi, qi: (bi, 0, 0)),
                  _const_spec((nc, nbp))],
        out_specs=[pl.BlockSpec((1, tq, N_PAIRS * LANES), lambda bi, qi: (bi, qi, 0)),
                   pl.BlockSpec((1, tq, NSA_GROUPS * nbp), lambda bi, qi: (bi, qi, 0))],
        out_shape=[jax.ShapeDtypeStruct((b, t_q, N_PAIRS * LANES), F32),
                   jax.ShapeDtypeStruct((b, t_q, NSA_GROUPS * nbp), BF16)],
        scratch_shapes=[pltpu.VMEM((nbp, tqp), F32)],
        compiler_params=_params(("parallel", "parallel"), 40),
        name=name,
    )(q, kc, vc, cov)


def _merge_kernel(x_ref, fo_ref, oc_ref, os_ref, ow_ref, ng_ref, gf_ref, gn_ref, wuf_ref, wun_ref, wout_ref,
                  eg_ref, g_ref, b_ref, o_ref, *, alpha):
    gates = jax.nn.sigmoid(ng_ref[...])
    gx = lax.dot_general(gates, eg_ref[...], (((1,), (0,)), ((), ())), precision=HIGHEST, preferred_element_type=F32)
    w = NSA_WIDTH
    nsa_o = gx[:, :w] * oc_ref[...] + gx[:, w:2 * w] * os_ref[...] + gx[:, 2 * w:] * ow_ref[...]
    up_f = jnp.dot(fo_ref[...].astype(BF16), wuf_ref[...], preferred_element_type=F32)
    up_n = jnp.dot(nsa_o.astype(BF16), wun_ref[...], preferred_element_type=F32)
    mixed = jax.nn.sigmoid(gf_ref[...]) * up_f + jax.nn.sigmoid(gn_ref[...]) * up_n
    mix = jnp.dot(mixed.astype(BF16), wout_ref[...], preferred_element_type=F32)
    o_ref[...] = _layer_norm(alpha * x_ref[...] + mix, g_ref[...], b_ref[...])


def _gate_expand():
    perm = _nsa_perm()
    e = np.zeros((LANES, N_NSA_BRANCHES * NSA_WIDTH), np.float32)
    for br in range(N_NSA_BRANCHES):
        for pos in range(NSA_WIDTH):
            head = perm[pos] // HEAD_DIM
            e[br * NSA_HEADS + head, br * NSA_WIDTH + pos] = 1.0
    return jnp.asarray(e)


def _merge_ln(x, fox_o, o_c, o_s, o_w, ng, gf, gn, w_up_fox, w_up_nsa, w_out, g, b, alpha, name):
    n, d = x.shape
    tm = _pick(n, (256, 128, 8))
    row = lambda w: pl.BlockSpec((tm, w), lambda i: (i, 0))
    wuf = w_up_fox.astype(BF16)
    wun = w_up_nsa[_nsa_perm(), :].astype(BF16)
    wout = w_out.astype(BF16)
    eg = _gate_expand()
    return pl.pallas_call(
        functools.partial(_merge_kernel, alpha=alpha),
        grid=(n // tm,),
        in_specs=[row(d), row(FOX_WIDTH), row(NSA_WIDTH), row(NSA_WIDTH), row(NSA_WIDTH), row(LANES), row(d), row(d),
                  _const_spec(wuf.shape), _const_spec(wun.shape), _const_spec(wout.shape), _const_spec(eg.shape),
                  _const_spec((1, d)), _const_spec((1, d))],
        out_specs=row(d),
        out_shape=jax.ShapeDtypeStruct((n, d), F32),
        compiler_params=_params(("parallel",), 48),
        name=name,
    )(x, fox_o, o_c, o_s, o_w, ng, gf, gn, wuf, wun, wout, eg, g.reshape(1, d), b.reshape(1, d))


def _gather_rows(ref, first, count, stride):
    return ref[0, pl.ds(first, count, stride=stride), :]


def _fox_pages_kernel(pt_ref, page_ref, lf_ref, knew_ref, vnew_ref, lfnew_ref, k_o, v_o, lf_o, *, n_pages, n_new):
    j = pl.program_id(1)
    per_key = 2 * FOX_HEADS

    @pl.when(j < n_pages)
    def _():
        for h in range(FOX_HEADS):
            k_o[0, :, h * HEAD_DIM:(h + 1) * HEAD_DIM] = _gather_rows(page_ref, h, PAGE_SIZE, per_key).astype(BF16)
            v_o[0, :, h * HEAD_DIM:(h + 1) * HEAD_DIM] = _gather_rows(
                page_ref, FOX_HEADS + h, PAGE_SIZE, per_key).astype(BF16)
        lf_o[0] = lf_ref[0]

    @pl.when(j == n_pages)
    def _():
        k_o[0] = jnp.zeros(k_o.shape[1:], BF16)
        v_o[0] = jnp.zeros(v_o.shape[1:], BF16)
        lf_o[0] = jnp.zeros(lf_o.shape[1:], F32)
        k_o[0, 0:n_new, :] = knew_ref[0]
        v_o[0, 0:n_new, :] = vnew_ref[0]
        lf_o[0, 0:n_new, :] = lfnew_ref[0]


def _fox_pages(page_table, cache_kv, cache_lf, k_new, v_new, lf_new):
    s, n_pages = page_table.shape
    n_new = k_new.shape[1]
    n_pool = cache_kv.shape[0]
    rows = PAGE_SIZE * 2 * FOX_HEADS
    page = lambda si, j, pt: (pt[si * n_pages + jnp.minimum(j, n_pages - 1)], 0, 0)
    new = lambda si, j, pt: (si, 0, 0)
    out = lambda si, j, pt: (si, j, 0)
    t_all = (n_pages + 1) * PAGE_SIZE
    grid_spec = pltpu.PrefetchScalarGridSpec(
        num_scalar_prefetch=1, grid=(s, n_pages + 1),
        in_specs=[pl.BlockSpec((1, rows, HEAD_DIM), page), pl.BlockSpec((1, PAGE_SIZE, FOX_HEADS), page),
                  pl.BlockSpec((1, n_new, FOX_WIDTH), new), pl.BlockSpec((1, n_new, FOX_WIDTH), new),
                  pl.BlockSpec((1, n_new, FOX_HEADS), new)],
        out_specs=[pl.BlockSpec((1, PAGE_SIZE, FOX_WIDTH), out), pl.BlockSpec((1, PAGE_SIZE, FOX_WIDTH), out),
                   pl.BlockSpec((1, PAGE_SIZE, FOX_HEADS), out)])
    return pl.pallas_call(
        functools.partial(_fox_pages_kernel, n_pages=n_pages, n_new=n_new),
        grid_spec=grid_spec,
        out_shape=[jax.ShapeDtypeStruct((s, t_all, FOX_WIDTH), BF16), jax.ShapeDtypeStruct((s, t_all, FOX_WIDTH), BF16),
                   jax.ShapeDtypeStruct((s, t_all, FOX_HEADS), F32)],
        compiler_params=_params(("parallel", "arbitrary"), 32),
        name="fox_pages",
    )(page_table.reshape(-1), cache_kv.reshape(n_pool, rows, HEAD_DIM), cache_lf, k_new, v_new, lf_new)


def _nsa_pages_kernel(pt_ref, page_ref, ksnew_ref, vsnew_ref, kc_o, vc_o, ks_o, vs_o, *, n_pages, n_new):
    j = pl.program_id(1)
    per_key = 4 * NSA_GROUPS

    def pair(c):
        return jnp.concatenate([_gather_rows(page_ref, c * NSA_GROUPS + g, PAGE_SIZE, per_key)
                                for g in range(NSA_GROUPS)], axis=1)

    @pl.when(j < n_pages)
    def _():
        kc_o[0] = pair(0)
        vc_o[0] = pair(1)
        ks_o[0] = pair(2).astype(BF16)
        vs_o[0] = pair(3).astype(BF16)

    @pl.when(j == n_pages)
    def _():
        kc_o[0] = jnp.zeros(kc_o.shape[1:], F32)
        vc_o[0] = jnp.zeros(vc_o.shape[1:], F32)
        ks_o[0] = jnp.zeros(ks_o.shape[1:], BF16)
        vs_o[0] = jnp.zeros(vs_o.shape[1:], BF16)
        ks_o[0, 0:n_new, :] = ksnew_ref[0]
        vs_o[0, 0:n_new, :] = vsnew_ref[0]


def _nsa_pages(page_table, cache_kv, ks_new, vs_new):
    s, n_pages = page_table.shape
    n_new = ks_new.shape[1]
    n_pool = cache_kv.shape[0]
    rows = PAGE_SIZE * 4 * NSA_GROUPS
    page = lambda si, j, pt: (pt[si * n_pages + jnp.minimum(j, n_pages - 1)], 0, 0)
    new = lambda si, j, pt: (si, 0, 0)
    out = lambda si, j, pt: (si, j, 0)
    t_all = (n_pages + 1) * PAGE_SIZE
    blk = pl.BlockSpec((1, PAGE_SIZE, LANES), out)
    grid_spec = pltpu.PrefetchScalarGridSpec(
        num_scalar_prefetch=1, grid=(s, n_pages + 1),
        in_specs=[pl.BlockSpec((1, rows, HEAD_DIM), page), pl.BlockSpec((1, n_new, LANES), new),
                  pl.BlockSpec((1, n_new, LANES), new)],
        out_specs=[blk, blk, blk, blk])
    return pl.pallas_call(
        functools.partial(_nsa_pages_kernel, n_pages=n_pages, n_new=n_new),
        grid_spec=grid_spec,
        out_shape=[jax.ShapeDtypeStruct((s, t_all, LANES), F32), jax.ShapeDtypeStruct((s, t_all, LANES), F32),
                   jax.ShapeDtypeStruct((s, t_all, LANES), BF16), jax.ShapeDtypeStruct((s, t_all, LANES), BF16)],
        compiler_params=_params(("parallel", "arbitrary"), 32),
        name="nsa_pages",
    )(page_table.reshape(-1), cache_kv.reshape(n_pool, rows, HEAD_DIM), ks_new, vs_new)


def _win_rows_kernel(st_ref, kwnew_ref, vwnew_ref, kw_o, vw_o, *, keep, n_new):
    per_key = 2 * NSA_GROUPS
    kw_o[0] = jnp.zeros(kw_o.shape[1:], BF16)
    vw_o[0] = jnp.zeros(vw_o.shape[1:], BF16)
    for c, o_ref in ((0, kw_o), (1, vw_o)):
        o_ref[0, 0:keep, :] = jnp.concatenate(
            [_gather_rows(st_ref, c * NSA_GROUPS + g, keep, per_key) for g in range(NSA_GROUPS)], axis=1).astype(BF16)
    kw_o[0, keep:keep + n_new, :] = kwnew_ref[0]
    vw_o[0, keep:keep + n_new, :] = vwnew_ref[0]


def _win_rows(state, kw_new, vw_new, t_all):
    s, keep = state.shape[0], state.shape[1]
    n_new = kw_new.shape[1]
    rows = keep * 2 * NSA_GROUPS
    seq = lambda si: (si, 0, 0)
    return pl.pallas_call(
        functools.partial(_win_rows_kernel, keep=keep, n_new=n_new),
        grid=(s,),
        in_specs=[pl.BlockSpec((1, rows, HEAD_DIM), seq), pl.BlockSpec((1, n_new, LANES), seq),
                  pl.BlockSpec((1, n_new, LANES), seq)],
        out_specs=[pl.BlockSpec((1, t_all, LANES), seq), pl.BlockSpec((1, t_all, LANES), seq)],
        out_shape=[jax.ShapeDtypeStruct((s, t_all, LANES), BF16), jax.ShapeDtypeStruct((s, t_all, LANES), BF16)],
        compiler_params=_params(("parallel",), 32),
        name="win_rows",
    )(state.reshape(s, rows, HEAD_DIM), kw_new, vw_new)


def _pair_views(c):
    b, h, t = c.shape
    c4 = c.reshape(b, h // 2, 2, t)
    return c4.transpose(0, 1, 3, 2), c4


def _round_up(n, m):
    return (n + m - 1) // m * m


def kernel(x_prompt, x_sample, cache_fox_kv, cache_fox_logf, cache_nsa_kv, state_win_kv, page_table, ln1_g, ln1_b, ffn1_w_up, ffn1_w_down, w_in, b_fgate, cmp_pos_k, cmp_wk1, cmp_wk2, cmp_pos_v, cmp_wv1, cmp_wv2, w_up_fox, w_up_nsa, w_out, ln2_g, ln2_b, ffn2_w_up, ffn2_w_down, ln3_g, ln3_b):
    depth = ln1_g.shape[0]
    assert depth == 1, "single-layer step"
    alpha = (2.0 * depth) ** 0.25
    bsz, seq, d = x_prompt.shape
    sb, n_new, _ = x_sample.shape
    n_pages = page_table.shape[1]
    past = n_pages * PAGE_SIZE
    keep = state_win_kv.shape[2]
    assert seq % 256 == 0 and past % SLC_LEN == 0 and n_new < CMP_STRIDE and keep == WINDOW
    layer = 0

    ws, bf = _split_w_in(w_in[layer], b_fgate[layer])
    cmp_k = _compress_weights(cmp_pos_k[layer], cmp_wk1[layer], cmp_wk2[layer])
    cmp_v = _compress_weights(cmp_pos_v[layer], cmp_wv1[layer], cmp_wv2[layer])

    n = bsz * seq
    xp = _ffn_ln(x_prompt.reshape(n, d), ffn1_w_up[layer], ffn1_w_down[layer], ln1_g[layer], ln1_b[layer], alpha,
                 "ffn1_prompt")
    cos_p, sin_p = _rope_tables(jnp.arange(seq, dtype=jnp.int32))
    mp = _in_proj(xp, ws, bf, cos_p, sin_p, "in_proj_prompt")
    r3 = lambda a: a.reshape(bsz, seq, a.shape[-1])

    c_p = _cumsum_time(r3(mp["lf"]).transpose(0, 2, 1), "cumsum_prompt")
    cc_p, cr_p = _pair_views(c_p)
    tq = 256
    tk = _pick(seq, (512, 256))
    fox_o = _attention("fox", r3(mp["fq"]), r3(mp["fk"]), r3(mp["fv"]), tq=tq, tk=tk, q_off=0, k_off=0,
                       extra=(cc_p, cr_p), name="fox_prompt")

    nkv3 = r3(mp["nkv"])
    kc_p = _compress(nkv3[:, :, 0:LANES], cmp_k, "compress_k_prompt")
    vc_p = _compress(nkv3[:, :, LANES:2 * LANES], cmp_v, "compress_v_prompt")
    n_cmp_p = seq // CMP_STRIDE - 1
    n_slc_p = -(-seq // SLC_LEN)
    nbp_p = _round_up(n_slc_p, LANES)
    nq3 = r3(mp["nq"])
    oc_p, sel_p = _cmp_select(nq3, kc_p, vc_p, tq=tq, n_cmp=n_cmp_p, n_slc=n_slc_p, nbp=nbp_p, q_off=0,
                              name="cmp_select_prompt")
    os_p = _attention("slc", nq3, r3(mp["ks"]), r3(mp["vs"]), tq=tq, tk=tk, q_off=0, k_off=0, extra=(sel_p,),
                      nbp=nbp_p, name="slc_prompt")
    ow_p = _attention("win", nq3, r3(mp["kw"]), r3(mp["vw"]), tq=tq, tk=256, q_off=0, k_off=0, name="win_prompt")
    flat = lambda a: a.reshape(n, a.shape[-1])
    xp2 = _merge_ln(xp, flat(fox_o), flat(oc_p), flat(os_p), flat(ow_p), mp["ng"], mp["gf"], mp["gn"],
                    w_up_fox[layer], w_up_nsa[layer], w_out[layer], ln2_g[layer], ln2_b[layer], alpha, "merge_prompt")
    yp = _ffn_ln(xp2, ffn2_w_up[layer], ffn2_w_down[layer], ln3_g[layer], ln3_b[layer], alpha, "ffn2_prompt")

    ns_rows = sb * n_new
    xs = _ffn_ln(x_sample.reshape(ns_rows, d), ffn1_w_up[layer], ffn1_w_down[layer], ln1_g[layer], ln1_b[layer],
                 alpha, "ffn1_sample")
    cos_s, sin_s = _rope_tables(past + jnp.arange(n_new, dtype=jnp.int32))
    tile_rows = _pick(ns_rows, (256, 128, 8))
    reps = tile_rows // n_new
    ms = _in_proj(xs, ws, bf, jnp.tile(cos_s, (reps, 1)), jnp.tile(sin_s, (reps, 1)), "in_proj_sample")
    s3 = lambda a: a.reshape(sb, n_new, a.shape[-1])

    t_all = past + PAGE_SIZE
    k_all, v_all, lf_all = _fox_pages(page_table, cache_fox_kv[layer], cache_fox_logf[layer], s3(ms["fk"]),
                                      s3(ms["fv"]), s3(ms["lf"]))
    t_cs = _round_up(t_all, 8 * LANES)
    lf_t = jnp.pad(lf_all.transpose(0, 2, 1), ((0, 0), (0, 0), (0, t_cs - t_all)))
    c_s = _cumsum_time(lf_t, "cumsum_sample")
    cc_s, _ = _pair_views(c_s[:, :, past:past + n_new])
    _, cr_s = _pair_views(c_s[:, :, :t_all])
    tk_s = _pick(t_all, (13 * LANES, 8 * LANES, 5 * LANES, LANES))
    fox_o_s = _attention("fox", s3(ms["fq"]), k_all, v_all, tq=n_new, tk=tk_s, q_off=past, k_off=0,
                         extra=(cc_s, cr_s), name="fox_sample")

    kcr_s, vcr_s, ks_all, vs_all = _nsa_pages(page_table, cache_nsa_kv[layer], s3(ms["ks"]), s3(ms["vs"]))
    kc_s = _compress(kcr_s[:, :past], cmp_k, "compress_k_sample")
    vc_s = _compress(vcr_s[:, :past], cmp_v, "compress_v_sample")
    n_cmp_s = (past + n_new) // CMP_STRIDE - 1
    n_slc_s = -(-(past + n_new) // SLC_LEN)
    nbp_s = _round_up(n_slc_s, LANES)
    nq_s = s3(ms["nq"])
    oc_s, sel_s = _cmp_select(nq_s, kc_s, vc_s, tq=n_new, n_cmp=n_cmp_s, n_slc=n_slc_s, nbp=nbp_s, q_off=past,
                              name="cmp_select_sample")
    os_s = _attention("slc", nq_s, ks_all, vs_all, tq=n_new, tk=tk_s, q_off=past, k_off=0, extra=(sel_s,),
                      nbp=nbp_s, name="slc_sample")
    t_win = _round_up(keep + n_new, LANES)
    kw_all, vw_all = _win_rows(state_win_kv[layer], s3(ms["kw"]), s3(ms["vw"]), t_win)
    ow_s = _attention("win", nq_s, kw_all, vw_all, tq=n_new, tk=t_win, q_off=past, k_off=past - keep,
                      name="win_sample")
    flat_s = lambda a: a.reshape(ns_rows, a.shape[-1])
    xs2 = _merge_ln(xs, flat_s(fox_o_s), flat_s(oc_s), flat_s(os_s), flat_s(ow_s), ms["ng"], ms["gf"], ms["gn"],
                    w_up_fox[layer], w_up_nsa[layer], w_out[layer], ln2_g[layer], ln2_b[layer], alpha, "merge_sample")
    ys = _ffn_ln(xs2, ffn2_w_up[layer], ffn2_w_down[layer], ln3_g[layer], ln3_b[layer], alpha, "ffn2_sample")

    fox_kv_p = mp["fkv"].reshape(1, bsz, seq, 2, FOX_HEADS, HEAD_DIM)
    fox_kv_s = ms["fkv"].reshape(1, sb, n_new, 2, FOX_HEADS, HEAD_DIM)
    logf_p = mp["lf"].reshape(1, bsz, seq, FOX_HEADS)
    logf_s = ms["lf"].reshape(1, sb, n_new, FOX_HEADS)
    nsa_kv_p = mp["nkv"].reshape(1, bsz, seq, 4, NSA_GROUPS, HEAD_DIM)
    nsa_kv_s = ms["nkv"].reshape(1, sb, n_new, 4, NSA_GROUPS, HEAD_DIM)
    win_rows_p = r3(mp["win"]).reshape(bsz, seq, 2, NSA_GROUPS, HEAD_DIM)
    if seq >= keep:
        win_p = win_rows_p[:, seq - keep:]
    else:
        win_p = jnp.pad(win_rows_p, ((0, 0), (keep - seq, 0), (0, 0), (0, 0), (0, 0)))
    new_win = ms["win"].reshape(sb, n_new, 2, NSA_GROUPS, HEAD_DIM).astype(state_win_kv.dtype)
    win_s = jnp.concatenate([state_win_kv[layer], new_win], axis=1)[:, -keep:]
    return (yp.reshape(bsz, seq, d), ys.reshape(sb, n_new, d), fox_kv_p, fox_kv_s, logf_p, logf_s,
            nsa_kv_p, nsa_kv_s, win_p[None], win_s[None])
```

```python
import functools

import numpy as np
import jax
import jax.numpy as jnp
from jax import lax
from jax.experimental import pallas as pl
from jax.experimental.pallas import tpu as pltpu

F32 = jnp.float32
BF16 = jnp.bfloat16

HEAD_DIM = 64
FOX_HEADS = 8
NSA_HEADS = 8
NSA_GROUPS = 2
NSA_HPG = NSA_HEADS // NSA_GROUPS
FOX_WIDTH = FOX_HEADS * HEAD_DIM
NSA_WIDTH = NSA_HEADS * HEAD_DIM
NSA_KV_WIDTH = NSA_GROUPS * HEAD_DIM
N_NSA_BRANCHES = 3
CMP_LEN = 32
CMP_STRIDE = 16
CMP_HIDDEN = 128
SLC_LEN = 64
N_SELECT = 16
WINDOW = 512
PAGE_SIZE = 128
ROPE_THETA = 10000.0
LN_EPS = 1e-5
FORCE_BONUS = 1e4
NEG_INF = -1e30
M_INIT = 0.1 * NEG_INF
SCALE = HEAD_DIM ** -0.5

LANES = 128
N_PAIRS = 4
MIB = 1024 * 1024
HIGHEST = lax.Precision.HIGHEST
NT_DIMS = (((1,), (1,)), ((), ()))


def _params(semantics, vmem_mib):
    return pltpu.CompilerParams(dimension_semantics=semantics, vmem_limit_bytes=vmem_mib * MIB)


def _pick(n, candidates):
    for c in candidates:
        if n % c == 0:
            return c
    return n


def _round_up(n, m):
    return (n + m - 1) // m * m


def _layer_norm(y, g, b):
    mu = jnp.mean(y, axis=-1, keepdims=True)
    d = y - mu
    var = jnp.mean(d * d, axis=-1, keepdims=True)
    return d * lax.rsqrt(var + LN_EPS) * g + b


def _const_spec(shape):
    nd = len(shape)
    return pl.BlockSpec(shape, lambda *_: (0,) * nd, pipeline_mode=pl.Buffered(1))


def _ffn_ln_kernel(x_ref, wa_ref, wb_ref, wd_ref, g_ref, b_ref, o_ref, acc_ref, *, alpha, n_chunks):
    x = x_ref[...]
    xb = x.astype(BF16)
    acc_ref[...] = jnp.zeros_like(acc_ref)

    def body(c, carry):
        a = jnp.dot(xb, wa_ref[c], preferred_element_type=F32)
        b = jnp.dot(xb, wb_ref[c], preferred_element_type=F32)
        h = (a * jax.nn.sigmoid(a) * b).astype(BF16)
        acc_ref[...] += jnp.dot(h, wd_ref[c], preferred_element_type=F32)
        return carry

    lax.fori_loop(0, n_chunks, body, 0)
    o_ref[...] = _layer_norm(alpha * x + 0.5 * acc_ref[...], g_ref[...], b_ref[...])


def _ffn_ln(x, w_up, w_down, g, b, alpha, name):
    n, d = x.shape
    f = w_down.shape[0]
    fc = _pick(f, (256, 128))
    nc = f // fc
    wa = w_up[:, :f].astype(BF16).reshape(d, nc, fc).transpose(1, 0, 2)
    wb = w_up[:, f:].astype(BF16).reshape(d, nc, fc).transpose(1, 0, 2)
    wd = w_down.astype(BF16).reshape(nc, fc, d)
    tm = _pick(n, (512, 256, 128, 8))
    return pl.pallas_call(
        functools.partial(_ffn_ln_kernel, alpha=alpha, n_chunks=nc),
        grid=(n // tm,),
        in_specs=[pl.BlockSpec((tm, d), lambda i: (i, 0)),
                  _const_spec((nc, d, fc)), _const_spec((nc, d, fc)), _const_spec((nc, fc, d)),
                  _const_spec((1, d)), _const_spec((1, d))],
        out_specs=pl.BlockSpec((tm, d), lambda i: (i, 0)),
        out_shape=jax.ShapeDtypeStruct((n, d), F32),
        scratch_shapes=[pltpu.VMEM((tm, d), F32)],
        compiler_params=_params(("parallel",), 56),
        name=name,
    )(x, wa, wb, wd, g.reshape(1, d), b.reshape(1, d))


def _rope_tables(pos):
    half = HEAD_DIM // 2
    inv_freq = ROPE_THETA ** (-jnp.arange(half, dtype=F32) / half)
    ang = pos.astype(F32)[:, None] * inv_freq[None, :]
    cos, sin = jnp.cos(ang), jnp.sin(ang)
    cos64 = jnp.concatenate([cos, cos], axis=-1)
    sin64 = jnp.concatenate([-sin, sin], axis=-1)
    return jnp.tile(cos64, (1, LANES // HEAD_DIM)), jnp.tile(sin64, (1, LANES // HEAD_DIM))


def _in_proj_kernel(x_ref, wfq, wfkv, wsm, wnq, wnkv, wwin, wgf, wgn, bf_ref, cos_ref, sin_ref,
                    fq_o, fkv_o, fk_o, fv_o, lf_o, nq_o, nkv_o, ks_o, vs_o, win_o, kw_o, vw_o,
                    ng_o, gf_o, gn_o):
    xb = x_ref[...].astype(BF16)
    cos = cos_ref[...]
    sin = sin_ref[...]
    lane = lax.broadcasted_iota(jnp.int32, (1, LANES), 1)
    low_half = (lane % HEAD_DIM) < (HEAD_DIM // 2)

    def rope(v):
        partner = jnp.where(low_half, pltpu.roll(v, LANES - HEAD_DIM // 2, 1), pltpu.roll(v, HEAD_DIM // 2, 1))
        return v * cos + partner * sin

    def proj(w_ref):
        return jnp.dot(xb, w_ref[...], preferred_element_type=F32)

    fq_o[...] = (proj(wfq) * SCALE).astype(BF16)
    fkv = proj(wfkv)
    fkv_o[...] = fkv
    fk_o[...] = fkv[:, :FOX_WIDTH].astype(BF16)
    fv_o[...] = fkv[:, FOX_WIDTH:].astype(BF16)

    sm = proj(wsm)
    z = sm[:, :LANES] + bf_ref[...]
    logf = jnp.minimum(z, 0.0) - jnp.log1p(jnp.exp(-jnp.abs(z)))
    lf_o[...] = logf[:, :FOX_HEADS]
    ng_o[...] = sm[:, LANES:]

    nq = proj(wnq)
    for r in range(N_PAIRS):
        nq_o[:, r * LANES:(r + 1) * LANES] = (rope(nq[:, r * LANES:(r + 1) * LANES]) * SCALE).astype(BF16)

    nkv = proj(wnkv)
    k_cmp = rope(nkv[:, 0:LANES])
    k_slc = rope(nkv[:, 2 * LANES:3 * LANES])
    v_slc = nkv[:, 3 * LANES:4 * LANES]
    nkv_o[:, 0:LANES] = k_cmp
    nkv_o[:, LANES:2 * LANES] = nkv[:, LANES:2 * LANES]
    nkv_o[:, 2 * LANES:3 * LANES] = k_slc
    nkv_o[:, 3 * LANES:4 * LANES] = v_slc
    ks_o[...] = k_slc.astype(BF16)
    vs_o[...] = v_slc.astype(BF16)

    win = proj(wwin)
    k_win = rope(win[:, :LANES])
    v_win = win[:, LANES:]
    win_o[:, :LANES] = k_win
    win_o[:, LANES:] = v_win
    kw_o[...] = k_win.astype(BF16)
    vw_o[...] = v_win.astype(BF16)

    gf_o[...] = proj(wgf)
    gn_o[...] = proj(wgn)


def _nsa_perm():
    perm = np.zeros(NSA_WIDTH, np.int32)
    for r in range(NSA_HPG):
        for g in range(NSA_GROUPS):
            for d in range(HEAD_DIM):
                perm[r * LANES + g * HEAD_DIM + d] = (g * NSA_HPG + r) * HEAD_DIM + d
    return perm


def _split_w_in(w_in, b_fgate):
    d = w_in.shape[0]
    sizes = (FOX_WIDTH, FOX_WIDTH, FOX_WIDTH, FOX_HEADS, NSA_WIDTH) + (NSA_KV_WIDTH,) * 6 + (
        N_NSA_BRANCHES * NSA_HEADS, d, d)
    offs = np.concatenate([[0], np.cumsum(sizes)])
    col = lambda i, j=None: w_in[:, offs[i]:offs[(i if j is None else j) + 1]]
    pad = lambda w: jnp.pad(w, ((0, 0), (0, LANES - w.shape[1])))
    ws = dict(
        wfq=col(0), wfkv=col(1, 2),
        wsm=jnp.concatenate([pad(col(3)), pad(col(11))], axis=1),
        wnq=col(4)[:, _nsa_perm()], wnkv=col(5, 8), wwin=col(9, 10), wgf=col(12), wgn=col(13))
    ws = {k: v.astype(BF16) for k, v in ws.items()}
    bf = jnp.pad(b_fgate.astype(F32), (0, LANES - FOX_HEADS)).reshape(1, LANES)
    return ws, bf


def _in_proj(x, ws, bf, cos_tab, sin_tab, name):
    n, d = x.shape
    n_tab = cos_tab.shape[0]
    tm = _pick(n_tab, (256, 128, 8))
    tab_tiles = n_tab // tm
    row = lambda w: pl.BlockSpec((tm, w), lambda i: (i, 0))
    tab = pl.BlockSpec((tm, LANES), lambda i: (i % tab_tiles, 0))
    names = ("wfq", "wfkv", "wsm", "wnq", "wnkv", "wwin", "wgf", "wgn")
    outs = [("fq", FOX_WIDTH, BF16), ("fkv", 2 * FOX_WIDTH, F32), ("fk", FOX_WIDTH, BF16), ("fv", FOX_WIDTH, BF16),
            ("lf", FOX_HEADS, F32), ("nq", NSA_WIDTH, BF16), ("nkv", 4 * NSA_KV_WIDTH, F32),
            ("ks", NSA_KV_WIDTH, BF16), ("vs", NSA_KV_WIDTH, BF16), ("win", 2 * NSA_KV_WIDTH, F32),
            ("kw", NSA_KV_WIDTH, BF16), ("vw", NSA_KV_WIDTH, BF16), ("ng", LANES, F32), ("gf", d, F32), ("gn", d, F32)]
    res = pl.pallas_call(
        _in_proj_kernel,
        grid=(n // tm,),
        in_specs=[row(d)] + [_const_spec(ws[k].shape) for k in names] + [_const_spec((1, LANES)), tab, tab],
        out_specs=[row(w) for _, w, _ in outs],
        out_shape=[jax.ShapeDtypeStruct((n, w), dt) for _, w, dt in outs],
        compiler_params=_params(("parallel",), 56),
        name=name,
    )(x, *[ws[k] for k in names], bf, cos_tab, sin_tab)
    return {k: v for (k, _, _), v in zip(outs, res)}


def _cumsum_kernel(x_ref, tri_ref, low_ref, o_ref):
    x = x_ref[0]
    within = lax.dot_general(x, tri_ref[...], (((1,), (0,)), ((), ())), precision=HIGHEST, preferred_element_type=F32)
    tot = jnp.broadcast_to(within[:, LANES - 1:LANES], within.shape)
    before = lax.dot_general(low_ref[...], tot, (((1,), (0,)), ((), ())), precision=HIGHEST, preferred_element_type=F32)
    o_ref[0] = within + before


def _cumsum_time(logf_t, name):
    b, h, t = logf_t.shape
    r = t // LANES
    rows = h * r
    idx = np.arange(LANES)
    tri = jnp.asarray((idx[:, None] <= idx[None, :]).astype(np.float32))
    ridx = np.arange(rows)
    low = jnp.asarray(((ridx[None, :] < ridx[:, None]) & (ridx[None, :] // r == ridx[:, None] // r)).astype(np.float32))
    out = pl.pallas_call(
        _cumsum_kernel,
        grid=(b,),
        in_specs=[pl.BlockSpec((1, rows, LANES), lambda i: (i, 0, 0)), _const_spec((LANES, LANES)),
                  _const_spec((rows, rows))],
        out_specs=pl.BlockSpec((1, rows, LANES), lambda i: (i, 0, 0)),
        out_shape=jax.ShapeDtypeStruct((b, rows, LANES), F32),
        compiler_params=_params(("parallel",), 32),
        name=name,
    )(logf_t.reshape(b, rows, LANES), tri, low)
    return out.reshape(b, h, t)


FLAG_FIRST, FLAG_LAST, FLAG_EDGE = 1, 2, 4


def _tile_schedule(mode, nq, nkt, tq, tk, q_off, k_off):
    qi_tab, kt_tab, flag_tab = [], [], []
    for qi in range(nq):
        q_lo, q_hi = q_off + qi * tq, q_off + (qi + 1) * tq - 1
        tiles = []
        for kt in range(nkt):
            k_lo, k_hi = k_off + kt * tk, k_off + (kt + 1) * tk - 1
            if mode == "win":
                visible = (q_hi - k_lo >= 0) and (q_lo - k_hi < WINDOW)
                interior = (q_lo - k_hi >= 0) and (q_hi - k_lo < WINDOW)
            else:
                visible = k_lo <= q_hi
                interior = k_hi <= q_lo
            if visible:
                tiles.append((kt, 0 if interior else FLAG_EDGE))
        assert tiles, "every query tile sees at least one key tile"
        for n, (kt, flag) in enumerate(tiles):
            qi_tab.append(qi)
            kt_tab.append(kt)
            flag_tab.append(flag | (FLAG_FIRST if n == 0 else 0) | (FLAG_LAST if n == len(tiles) - 1 else 0))
    as_i32 = lambda v: jnp.asarray(np.asarray(v, np.int32))
    return as_i32(qi_tab), as_i32(kt_tab), as_i32(flag_tab)


def _attn_kernel(qi_tab, kt_tab, flag_tab, *refs, mode, tq, tk, pps, q_off, k_off, nbp):
    if mode == "fox":
        q_ref, k_ref, v_ref, cc_ref, cr_ref, o_ref, qm_ref, m_ref, l_ref, acc_ref, cq_ref = refs
    elif mode == "slc":
        q_ref, k_ref, v_ref, sel_ref, o_ref, qm_ref, m_ref, l_ref, acc_ref = refs
    else:
        q_ref, k_ref, v_ref, o_ref, qm_ref, m_ref, l_ref, acc_ref = refs
    step = pl.program_id(2)
    qi = qi_tab[step]
    kt = kt_tab[step]
    flags = flag_tab[step]
    lane = lax.broadcasted_iota(jnp.int32, (1, LANES), 1)
    left = lane < HEAD_DIM
    n_chunks = tk // LANES
    heads = [(pr, hh) for pr in range(pps) for hh in range(2)]

    @pl.when((flags & FLAG_FIRST) != 0)
    def _():
        m_ref[...] = jnp.full_like(m_ref, M_INIT)
        l_ref[...] = jnp.zeros_like(l_ref)
        acc_ref[...] = jnp.zeros_like(acc_ref)
        for pr, hh in heads:
            qp = q_ref[0, :, pr * LANES:(pr + 1) * LANES]
            qm_ref[2 * pr + hh] = jnp.where(left if hh == 0 else jnp.logical_not(left), qp, jnp.zeros_like(qp))
            if mode == "fox":
                cq_ref[2 * pr + hh] = jnp.broadcast_to(cc_ref[0, 0][:, hh:hh + 1], (tq, LANES))

    def tile(edge):
        v = v_ref[0]
        if edge:
            q_pos = q_off + qi * tq + lax.broadcasted_iota(jnp.int32, (tq, 1), 0)
        if mode == "slc":
            blk_row = lax.broadcasted_iota(jnp.int32, (nbp, tk), 0)
            blk_key = (kt * tk + lax.broadcasted_iota(jnp.int32, (nbp, tk), 1)) // SLC_LEN
            expand = (blk_row == blk_key).astype(BF16)
            sel_bias = [jnp.dot(sel_ref[0, :, g * nbp:(g + 1) * nbp], expand, preferred_element_type=F32)
                        for g in range(NSA_GROUPS)]
        for pr, hh in heads:
            hd = 2 * pr + hh
            k = k_ref[0, :, pr * LANES:(pr + 1) * LANES] if mode == "fox" else k_ref[0]
            s = lax.dot_general(qm_ref[hd], k, NT_DIMS, preferred_element_type=F32)
            chunks = []
            for c in range(n_chunks):
                t = s[:, c * LANES:(c + 1) * LANES]
                if mode == "fox":
                    t = t + (cq_ref[hd] - cr_ref[0, 0][hh:hh + 1, c * LANES:(c + 1) * LANES])
                if mode == "slc":
                    t = t + sel_bias[hh][:, c * LANES:(c + 1) * LANES]
                if edge:
                    k_pos = k_off + kt * tk + c * LANES + lane
                    if mode == "win":
                        rel = q_pos - k_pos
                        t = jnp.where((rel >= 0) & (rel < WINDOW), t, NEG_INF)
                    else:
                        t = jnp.where(k_pos <= q_pos, t, NEG_INF)
                chunks.append(t)
            mx = chunks[0]
            for t in chunks[1:]:
                mx = jnp.maximum(mx, t)
            m_prev = m_ref[hd]
            m_new = jnp.maximum(m_prev, jnp.max(mx, axis=1, keepdims=True))
            alpha = jnp.exp(m_prev - m_new)
            ps = [jnp.exp(t - m_new) for t in chunks]
            lsum = ps[0]
            for p in ps[1:]:
                lsum = lsum + p
            l_ref[hd] = alpha * l_ref[hd] + lsum
            pb = jnp.concatenate([p.astype(BF16) for p in ps], axis=1) if n_chunks > 1 else ps[0].astype(BF16)
            vv = v[:, pr * LANES:(pr + 1) * LANES] if mode == "fox" else v
            acc_ref[hd] = alpha * acc_ref[hd] + jnp.dot(pb, vv, preferred_element_type=F32)
            m_ref[hd] = m_new

    pl.when((flags & FLAG_EDGE) != 0)(lambda: tile(True))
    pl.when((flags & FLAG_EDGE) == 0)(lambda: tile(False))

    @pl.when((flags & FLAG_LAST) != 0)
    def _():
        for pr in range(pps):
            outs = []
            for hh in range(2):
                total = jnp.sum(l_ref[2 * pr + hh], axis=1, keepdims=True)
                outs.append(acc_ref[2 * pr + hh] / jnp.maximum(total, 1e-30))
            o_ref[0, :, pr * LANES:(pr + 1) * LANES] = jnp.where(left, outs[0], outs[1])


def _attention(mode, q, k, v, *, tq, tk, q_off, k_off, extra=(), nbp=0, name):
    b, t_q, _ = q.shape
    t_k = k.shape[1]
    pps = 1 if mode == "fox" else N_PAIRS
    qi_tab, kt_tab, flag_tab = _tile_schedule(mode, t_q // tq, t_k // tk, tq, tk, q_off, k_off)
    n_steps = qi_tab.shape[0]
    width = pps * LANES
    q_spec = pl.BlockSpec((1, tq, width), lambda bi, p, s, qt, kt, fl: (bi, qt[s], p))
    kv_spec = pl.BlockSpec((1, tk, LANES), lambda bi, p, s, qt, kt, fl: (bi, kt[s], p))
    in_specs = [q_spec, kv_spec, kv_spec]
    scratch = [pltpu.VMEM((2 * pps, tq, LANES), BF16), pltpu.VMEM((2 * pps, tq, LANES), F32),
               pltpu.VMEM((2 * pps, tq, LANES), F32), pltpu.VMEM((2 * pps, tq, LANES), F32)]
    if mode == "fox":
        in_specs += [pl.BlockSpec((1, 1, tq, 2), lambda bi, p, s, qt, kt, fl: (bi, p, qt[s], 0)),
                     pl.BlockSpec((1, 1, 2, tk), lambda bi, p, s, qt, kt, fl: (bi, p, 0, kt[s]))]
        scratch += [pltpu.VMEM((2 * pps, tq, LANES), F32)]
    elif mode == "slc":
        in_specs += [pl.BlockSpec((1, tq, NSA_GROUPS * nbp), lambda bi, p, s, qt, kt, fl: (bi, qt[s], 0))]
    grid_spec = pltpu.PrefetchScalarGridSpec(
        num_scalar_prefetch=3, grid=(b, N_PAIRS // pps, n_steps), in_specs=in_specs,
        out_specs=pl.BlockSpec((1, tq, width), lambda bi, p, s, qt, kt, fl: (bi, qt[s], p)),
        scratch_shapes=scratch)
    return pl.pallas_call(
        functools.partial(_attn_kernel, mode=mode, tq=tq, tk=tk, pps=pps, q_off=q_off, k_off=k_off, nbp=nbp),
        grid_spec=grid_spec,
        out_shape=jax.ShapeDtypeStruct((b, t_q, N_PAIRS * LANES), F32),
        compiler_params=_params(("parallel", "parallel", "arbitrary"), 48),
        name=name,
    )(qi_tab, kt_tab, flag_tab, q, k, v, *extra)


def _compress_kernel(rows_ref, pa_ref, pb_ref, wa_ref, wb_ref, w2_ref, o_ref, *, ns):
    u = jnp.concatenate([rows_ref[0, pl.ds(r, ns, stride=CMP_STRIDE), :] for r in range(CMP_STRIDE)], axis=1)
    first = jnp.dot((u + pa_ref[...]).astype(BF16), wa_ref[...], preferred_element_type=F32)
    second = jnp.dot((u + pb_ref[...]).astype(BF16), wb_ref[...], preferred_element_type=F32)
    pre = first + pltpu.roll(second, ns - 1, 0)
    h = (pre * jax.nn.sigmoid(pre)).astype(BF16)
    o_ref[0] = jnp.dot(h, w2_ref[...], preferred_element_type=F32).astype(BF16)


def _compress_weights(pos, w1, w2):
    ratio = CMP_LEN // CMP_STRIDE
    assert ratio == 2
    w1r = w1.reshape(CMP_LEN, HEAD_DIM, CMP_HIDDEN)
    zeros = jnp.zeros((CMP_STRIDE, HEAD_DIM, CMP_HIDDEN), w1.dtype)

    def half(rows):
        g0 = jnp.concatenate([jnp.stack([rows, zeros], axis=1).reshape(-1, CMP_HIDDEN),
                              jnp.stack([zeros, rows], axis=1).reshape(-1, CMP_HIDDEN)], axis=1)
        return g0.astype(BF16)

    def pos_tab(p):
        return jnp.stack([p, p], axis=1).reshape(1, -1).astype(F32)

    z2 = jnp.zeros_like(w2)
    w2d = jnp.concatenate([jnp.concatenate([w2, z2], axis=1), jnp.concatenate([z2, w2], axis=1)], axis=0)
    return (pos_tab(pos[:CMP_STRIDE]), pos_tab(pos[CMP_STRIDE:]), half(w1r[:CMP_STRIDE]), half(w1r[CMP_STRIDE:]),
            w2d.astype(BF16))


def _compress(rows, col_block, weights, name):
    b, t, _ = rows.shape
    ns = t // CMP_STRIDE
    width = CMP_STRIDE * LANES
    pa, pb, wa, wb, w2d = weights
    return pl.pallas_call(
        functools.partial(_compress_kernel, ns=ns),
        grid=(b,),
        in_specs=[pl.BlockSpec((1, t, LANES), lambda i: (i, 0, col_block)), _const_spec((1, width)),
                  _const_spec((1, width)), _const_spec(wa.shape), _const_spec(wb.shape), _const_spec(w2d.shape)],
        out_specs=pl.BlockSpec((1, ns, LANES), lambda i: (i, 0, 0)),
        out_shape=jax.ShapeDtypeStruct((b, ns, LANES), BF16),
        compiler_params=_params(("parallel",), 48),
        name=name,
    )(rows, pa, pb, wa, wb, w2d)


def _cmp_select_kernel(q_ref, kc_ref, vc_ref, cov_ref, oc_ref, sel_ref, vt_ref, *,
                       tq, tqp, nc, n_cmp, n_slc, nbp, q_off, n_sel):
    qi = pl.program_id(1)
    lane = lax.broadcasted_iota(jnp.int32, (1, LANES), 1)
    left = lane < HEAD_DIM
    q_pos = q_off + qi * tq + lax.broadcasted_iota(jnp.int32, (tq, 1), 0)
    n_idx = lax.broadcasted_iota(jnp.int32, (1, nc), 1)
    cmask = ((n_idx * CMP_STRIDE + CMP_LEN - 1) <= q_pos) & (n_idx < n_cmp)
    kc = kc_ref[0]
    vc = vc_ref[0]
    cov = cov_ref[...]
    imp = [jnp.zeros((tq, nbp), F32) for _ in range(NSA_GROUPS)]
    for r in range(N_PAIRS):
        qp = q_ref[0, :, r * LANES:(r + 1) * LANES]
        halves = []
        for g in range(NSA_GROUPS):
            qh = jnp.where(left if g == 0 else jnp.logical_not(left), qp, jnp.zeros_like(qp))
            s = lax.dot_general(qh, kc, NT_DIMS, preferred_element_type=F32)
            s = jnp.where(cmask, s, NEG_INF)
            m = jnp.max(s, axis=1, keepdims=True)
            e = jnp.where(cmask, jnp.exp(s - m), 0.0)
            pb = (e / jnp.maximum(jnp.sum(e, axis=1, keepdims=True), 1e-30)).astype(BF16)
            halves.append(jnp.dot(pb, vc, preferred_element_type=F32))
            imp[g] = imp[g] + jnp.dot(pb, cov, preferred_element_type=F32)
        oc_ref[0, :, r * LANES:(r + 1) * LANES] = jnp.where(left, halves[0], halves[1])

    blk = lax.broadcasted_iota(jnp.int32, (1, nbp), 1)
    cur = q_pos // SLC_LEN
    forced = (blk == 0) | (blk == cur) | (blk == cur - 1)
    valid = (blk * SLC_LEN <= q_pos) & (blk < n_slc)
    nbr = _round_up(n_slc, 8)
    row = lax.broadcasted_iota(jnp.int32, (nbr, tqp), 0)
    for g in range(NSA_GROUPS):
        val = jnp.where(valid, jnp.where(forced, imp[g] + FORCE_BONUS, imp[g]), NEG_INF)
        if tqp > tq:
            val = jnp.concatenate([val, jnp.zeros((tqp - tq, nbp), F32)], axis=0)
        vt_ref[...] = val.T

        def body(i, cnt):
            vi = vt_ref[pl.ds(i, 1), :]
            vt = vt_ref[0:nbr, :]
            ahead = (vi > vt) | ((vi == vt) & (i < row))
            return cnt + ahead.astype(F32)

        cnt = lax.fori_loop(0, n_slc, body, jnp.zeros((nbr, tqp), F32))
        bias = jnp.where(cnt < n_sel, 0.0, NEG_INF)
        if nbp > nbr:
            bias = jnp.concatenate([bias, jnp.full((nbp - nbr, tqp), NEG_INF, F32)], axis=0)
        sel_ref[0, :, g * nbp:(g + 1) * nbp] = bias.T[:tq].astype(BF16)


def _coverage(n_cmp, nc, n_slc, nbp):
    c0 = np.arange(nc) * CMP_STRIDE
    s0 = np.arange(nbp) * SLC_LEN
    lo = np.maximum(c0[:, None], s0[None, :])
    hi = np.minimum(c0[:, None] + CMP_LEN, s0[None, :] + SLC_LEN)
    cov = np.maximum(hi - lo, 0).astype(np.float32) / CMP_LEN
    cov[n_cmp:, :] = 0.0
    cov[:, n_slc:] = 0.0
    return jnp.asarray(cov, dtype=BF16)


def _cmp_select(q, kc, vc, *, tq, n_cmp, n_slc, nbp, q_off, name):
    b, t_q, _ = q.shape
    nc = kc.shape[1]
    tqp = max(tq, LANES)
    cov = _coverage(n_cmp, nc, n_slc, nbp)
    kern = functools.partial(_cmp_select_kernel, tq=tq, tqp=tqp, nc=nc, n_cmp=n_cmp, n_slc=n_slc, nbp=nbp,
                             q_off=q_off, n_sel=min(N_SELECT, n_slc))
    return pl.pallas_call(
        kern,
        grid=(b, t_q // tq),
        in_specs=[pl.BlockSpec((1, tq, N_PAIRS * LANES), lambda bi, qi: (bi, qi, 0)),
                  pl.BlockSpec((1, nc, LANES), lambda bi, qi: (bi, 0, 0)),
                  pl.BlockSpec((1, nc, LANES), lambda bi, qi: (bi, 0, 0)),
                  _const_spec((nc, nbp))],
        out_specs=[pl.BlockSpec((1, tq, N_PAIRS * LANES), lambda bi, qi: (bi, qi, 0)),
                   pl.BlockSpec((1, tq, NSA_GROUPS * nbp), lambda bi, qi: (bi, qi, 0))],
        out_shape=[jax.ShapeDtypeStruct((b, t_q, N_PAIRS * LANES), F32),
                   jax.ShapeDtypeStruct((b, t_q, NSA_GROUPS * nbp), BF16)],
        scratch_shapes=[pltpu.VMEM((nbp, tqp), F32)],
        compiler_params=_params(("parallel", "parallel"), 40),
        name=name,
    )(q, kc, vc, cov)


def _merge_kernel(x_ref, fo_ref, oc_ref, os_ref, ow_ref, ng_ref, gf_ref, gn_ref, wuf_ref, wun_ref, wout_ref,
                  eg_ref, g_ref, b_ref, o_ref, *, alpha):
    gates = jax.nn.sigmoid(ng_ref[...])
    gx = lax.dot_general(gates, eg_ref[...], (((1,), (0,)), ((), ())), precision=HIGHEST, preferred_element_type=F32)
    w = NSA_WIDTH
    nsa_o = gx[:, :w] * oc_ref[...] + gx[:, w:2 * w] * os_ref[...] + gx[:, 2 * w:] * ow_ref[...]
    up_f = jnp.dot(fo_ref[...].astype(BF16), wuf_ref[...], preferred_element_type=F32)
    up_n = jnp.dot(nsa_o.astype(BF16), wun_ref[...], preferred_element_type=F32)
    mixed = jax.nn.sigmoid(gf_ref[...]) * up_f + jax.nn.sigmoid(gn_ref[...]) * up_n
    mix = jnp.dot(mixed.astype(BF16), wout_ref[...], preferred_element_type=F32)
    o_ref[...] = _layer_norm(alpha * x_ref[...] + mix, g_ref[...], b_ref[...])


def _gate_expand():
    perm = _nsa_perm()
    e = np.zeros((LANES, N_NSA_BRANCHES * NSA_WIDTH), np.float32)
    for br in range(N_NSA_BRANCHES):
        for pos in range(NSA_WIDTH):
            head = perm[pos] // HEAD_DIM
            e[br * NSA_HEADS + head, br * NSA_WIDTH + pos] = 1.0
    return jnp.asarray(e)


def _merge_ln(x, fox_o, o_c, o_s, o_w, ng, gf, gn, w_up_fox, w_up_nsa, w_out, g, b, alpha, name):
    n, d = x.shape
    tm = _pick(n, (256, 128, 8))
    row = lambda w: pl.BlockSpec((tm, w), lambda i: (i, 0))
    wuf = w_up_fox.astype(BF16)
    wun = w_up_nsa[_nsa_perm(), :].astype(BF16)
    wout = w_out.astype(BF16)
    eg = _gate_expand()
    return pl.pallas_call(
        functools.partial(_merge_kernel, alpha=alpha),
        grid=(n // tm,),
        in_specs=[row(d), row(FOX_WIDTH), row(NSA_WIDTH), row(NSA_WIDTH), row(NSA_WIDTH), row(LANES), row(d), row(d),
                  _const_spec(wuf.shape), _const_spec(wun.shape), _const_spec(wout.shape), _const_spec(eg.shape),
                  _const_spec((1, d)), _const_spec((1, d))],
        out_specs=row(d),
        out_shape=jax.ShapeDtypeStruct((n, d), F32),
        compiler_params=_params(("parallel",), 48),
        name=name,
    )(x, fox_o, o_c, o_s, o_w, ng, gf, gn, wuf, wun, wout, eg, g.reshape(1, d), b.reshape(1, d))


def _page_specs(block, group, n_pages):
    nd = len(block)

    def spec(slot):
        return pl.BlockSpec(block, lambda si, j, pt: (pt[si * n_pages + jnp.minimum(j * group + slot, n_pages - 1)],)
                            + (0,) * (nd - 1))

    return [spec(slot) for slot in range(group)]


def _lf_pages_kernel(pt_ref, *refs, group):
    o_ref = refs[group]
    for i in range(group):
        o_ref[0, :, i * PAGE_SIZE:(i + 1) * PAGE_SIZE] = refs[i][0]


def _lf_pages(page_table, cache_lf_t):
    s, n_pages = page_table.shape
    group = _pick(n_pages, (16, 8, 4, 2, 1))
    grid_spec = pltpu.PrefetchScalarGridSpec(
        num_scalar_prefetch=1, grid=(s, n_pages // group),
        in_specs=_page_specs((1, FOX_HEADS, PAGE_SIZE), group, n_pages),
        out_specs=pl.BlockSpec((1, FOX_HEADS, group * PAGE_SIZE), lambda si, j, pt: (si, 0, j)))
    return pl.pallas_call(
        functools.partial(_lf_pages_kernel, group=group),
        grid_spec=grid_spec,
        out_shape=jax.ShapeDtypeStruct((s, FOX_HEADS, n_pages * PAGE_SIZE), F32),
        compiler_params=_params(("parallel", "arbitrary"), 32),
        name="lf_pages",
    )(page_table.reshape(-1), *([cache_lf_t] * group))


def _fox_sample_kernel(pt_ref, q_ref, cc_ref, cr_ref, crn_ref, kn_ref, vn_ref, *refs, group, n_steps, n_new):
    pages = refs[:group]
    o_ref, m_ref, l_ref, acc_ref = refs[group:]
    j = pl.program_id(1)

    @pl.when(j == 0)
    def _():
        m_ref[...] = jnp.full_like(m_ref, M_INIT)
        l_ref[...] = jnp.zeros_like(l_ref)
        acc_ref[...] = jnp.zeros_like(acc_ref)

    def update(s, value_dots):
        m_prev = m_ref[...]
        m_new = jnp.maximum(m_prev, jnp.max(s, axis=1, keepdims=True))
        p = jnp.exp(s - m_new)
        alpha = jnp.exp(m_prev - m_new)
        l_ref[...] = alpha * l_ref[...] + jnp.sum(p, axis=1, keepdims=True)
        acc_ref[...] = alpha * acc_ref[...] + jnp.concatenate(
            [value_dots(h, p[h * n_new:(h + 1) * n_new].astype(BF16)) for h in range(FOX_HEADS)], axis=0)
        m_ref[...] = m_new

    def tiles(kv, h):
        return jnp.concatenate([pages[i][0, kv, h].astype(BF16) for i in range(group)], axis=1)

    scores = [jnp.dot(q_ref[0, h], tiles(0, h), preferred_element_type=F32) + (cc_ref[0, h] - cr_ref[0, h:h + 1, :])
              for h in range(FOX_HEADS)]
    update(jnp.concatenate(scores, axis=0),
           lambda h, pb: lax.dot_general(pb, tiles(1, h), NT_DIMS, preferred_element_type=F32))

    @pl.when(j == n_steps - 1)
    def _():
        t_q = lax.broadcasted_iota(jnp.int32, (n_new, n_new), 0)
        t_k = lax.broadcasted_iota(jnp.int32, (n_new, n_new), 1)
        new_scores = []
        for h in range(FOX_HEADS):
            s = lax.dot_general(q_ref[0, h], kn_ref[0, h], NT_DIMS, preferred_element_type=F32)
            new_scores.append(jnp.where(t_k <= t_q, s + (cc_ref[0, h] - crn_ref[0, h:h + 1, :]), NEG_INF))
        update(jnp.concatenate(new_scores, axis=0),
               lambda h, pb: jnp.dot(pb, vn_ref[0, h], preferred_element_type=F32))
        o_ref[0] = acc_ref[...] / jnp.maximum(l_ref[...], 1e-30)


def _fox_sample(page_table, cache_kv_t, q_h, k_new_h, v_new_h, c_q, c_row, c_new):
    s, n_pages = page_table.shape
    n_new = q_h.shape[2]
    group = _pick(n_pages, (8, 4, 2, 1))
    n_steps = n_pages // group
    rows = FOX_HEADS * n_new
    seq4 = lambda shape: pl.BlockSpec(shape, lambda si, j, pt: (si, 0, 0, 0))
    grid_spec = pltpu.PrefetchScalarGridSpec(
        num_scalar_prefetch=1, grid=(s, n_steps),
        in_specs=[seq4((1, FOX_HEADS, n_new, HEAD_DIM)), seq4((1, FOX_HEADS, n_new, 1)),
                  pl.BlockSpec((1, FOX_HEADS, group * PAGE_SIZE), lambda si, j, pt: (si, 0, j)),
                  pl.BlockSpec((1, FOX_HEADS, n_new), lambda si, j, pt: (si, 0, 0)),
                  seq4((1, FOX_HEADS, n_new, HEAD_DIM)), seq4((1, FOX_HEADS, n_new, HEAD_DIM))]
        + _page_specs((1, 2, FOX_HEADS, HEAD_DIM, PAGE_SIZE), group, n_pages),
        out_specs=pl.BlockSpec((1, rows, HEAD_DIM), lambda si, j, pt: (si, 0, 0)),
        scratch_shapes=[pltpu.VMEM((rows, 1), F32), pltpu.VMEM((rows, 1), F32), pltpu.VMEM((rows, HEAD_DIM), F32)])
    out = pl.pallas_call(
        functools.partial(_fox_sample_kernel, group=group, n_steps=n_steps, n_new=n_new),
        grid_spec=grid_spec,
        out_shape=jax.ShapeDtypeStruct((s, rows, HEAD_DIM), F32),
        compiler_params=_params(("parallel", "arbitrary"), 40),
        name="fox_sample",
    )(page_table.reshape(-1), q_h, c_q, c_row, c_new, k_new_h, v_new_h, *([cache_kv_t] * group))
    return out.reshape(s, FOX_HEADS, n_new, HEAD_DIM)


def _to_rows(tile):
    g, d, n = tile.shape
    return tile.reshape(g * d, n).T


def _nsa_pages_kernel(pt_ref, ksn_ref, vsn_ref, *refs, group, n_steps, n_new):
    pages = refs[:group]
    kc_o, vc_o, ks_o, vs_o = refs[group:]
    j = pl.program_id(1)

    @pl.when(j < n_steps)
    def _():
        for i in range(group):
            rows = slice(i * PAGE_SIZE, (i + 1) * PAGE_SIZE)
            kc_o[0, rows, :] = _to_rows(pages[i][0, 0])
            vc_o[0, rows, :] = _to_rows(pages[i][0, 1])
            ks_o[0, rows, :] = _to_rows(pages[i][0, 2]).astype(BF16)
            vs_o[0, rows, :] = _to_rows(pages[i][0, 3]).astype(BF16)

    @pl.when(j == n_steps)
    def _():
        ks_o[0] = jnp.zeros(ks_o.shape[1:], BF16)
        vs_o[0] = jnp.zeros(vs_o.shape[1:], BF16)
        ks_o[0, 0:n_new, :] = ksn_ref[0]
        vs_o[0, 0:n_new, :] = vsn_ref[0]


def _nsa_pages(page_table, cache_kv_t, ks_new, vs_new):
    s, n_pages = page_table.shape
    n_new = ks_new.shape[1]
    group = _pick(n_pages, (8, 4, 2, 1))
    n_steps = n_pages // group
    rows = group * PAGE_SIZE
    new = pl.BlockSpec((1, n_new, LANES), lambda si, j, pt: (si, 0, 0))
    past_out = pl.BlockSpec((1, rows, LANES), lambda si, j, pt: (si, jnp.minimum(j, n_steps - 1), 0))
    all_out = pl.BlockSpec((1, rows, LANES), lambda si, j, pt: (si, j, 0))
    grid_spec = pltpu.PrefetchScalarGridSpec(
        num_scalar_prefetch=1, grid=(s, n_steps + 1),
        in_specs=[new, new] + _page_specs((1, 4, NSA_GROUPS, HEAD_DIM, PAGE_SIZE), group, n_pages),
        out_specs=[past_out, past_out, all_out, all_out])
    past, t_all = n_pages * PAGE_SIZE, (n_steps + 1) * rows
    return pl.pallas_call(
        functools.partial(_nsa_pages_kernel, group=group, n_steps=n_steps, n_new=n_new),
        grid_spec=grid_spec,
        out_shape=[jax.ShapeDtypeStruct((s, past, LANES), F32), jax.ShapeDtypeStruct((s, past, LANES), F32),
                   jax.ShapeDtypeStruct((s, t_all, LANES), BF16), jax.ShapeDtypeStruct((s, t_all, LANES), BF16)],
        compiler_params=_params(("parallel", "arbitrary"), 40),
        name="nsa_pages",
    )(page_table.reshape(-1), ks_new, vs_new, *([cache_kv_t] * group))


def _win_rows_kernel(st_ref, kwnew_ref, vwnew_ref, kw_o, vw_o, *, keep, n_new):
    for c, new_ref, o_ref in ((0, kwnew_ref, kw_o), (1, vwnew_ref, vw_o)):
        o_ref[0] = jnp.zeros(o_ref.shape[1:], BF16)
        o_ref[0, 0:keep, :] = _to_rows(st_ref[0, c]).astype(BF16)
        o_ref[0, keep:keep + n_new, :] = new_ref[0]


def _win_rows(state_t, kw_new, vw_new, t_all):
    s, keep = state_t.shape[0], state_t.shape[-1]
    n_new = kw_new.shape[1]
    seq = lambda si: (si, 0, 0)
    return pl.pallas_call(
        functools.partial(_win_rows_kernel, keep=keep, n_new=n_new),
        grid=(s,),
        in_specs=[pl.BlockSpec((1, 2, NSA_GROUPS, HEAD_DIM, keep), lambda si: (si, 0, 0, 0, 0)),
                  pl.BlockSpec((1, n_new, LANES), seq), pl.BlockSpec((1, n_new, LANES), seq)],
        out_specs=[pl.BlockSpec((1, t_all, LANES), seq), pl.BlockSpec((1, t_all, LANES), seq)],
        out_shape=[jax.ShapeDtypeStruct((s, t_all, LANES), BF16), jax.ShapeDtypeStruct((s, t_all, LANES), BF16)],
        compiler_params=_params(("parallel",), 32),
        name="win_rows",
    )(state_t, kw_new, vw_new)


def _pair_views(c):
    b, h, t = c.shape
    c4 = c.reshape(b, h // 2, 2, t)
    return c4.transpose(0, 1, 3, 2), c4


def _rows_minor(a):
    return jnp.moveaxis(a, 1, -1)


def kernel(x_prompt, x_sample, cache_fox_kv, cache_fox_logf, cache_nsa_kv, state_win_kv, page_table, ln1_g, ln1_b, ffn1_w_up, ffn1_w_down, w_in, b_fgate, cmp_pos_k, cmp_wk1, cmp_wk2, cmp_pos_v, cmp_wv1, cmp_wv2, w_up_fox, w_up_nsa, w_out, ln2_g, ln2_b, ffn2_w_up, ffn2_w_down, ln3_g, ln3_b):
    depth = ln1_g.shape[0]
    assert depth == 1, "single-layer step"
    alpha = (2.0 * depth) ** 0.25
    bsz, seq, d = x_prompt.shape
    sb, n_new, _ = x_sample.shape
    n_pages = page_table.shape[1]
    past = n_pages * PAGE_SIZE
    keep = state_win_kv.shape[2]
    assert seq % 256 == 0 and past % SLC_LEN == 0 and n_new < CMP_STRIDE and keep == WINDOW
    layer = 0

    ws, bf = _split_w_in(w_in[layer], b_fgate[layer])
    cmp_k = _compress_weights(cmp_pos_k[layer], cmp_wk1[layer], cmp_wk2[layer])
    cmp_v = _compress_weights(cmp_pos_v[layer], cmp_wv1[layer], cmp_wv2[layer])

    n = bsz * seq
    xp = _ffn_ln(x_prompt.reshape(n, d), ffn1_w_up[layer], ffn1_w_down[layer], ln1_g[layer], ln1_b[layer], alpha,
                 "ffn1_prompt")
    cos_p, sin_p = _rope_tables(jnp.arange(seq, dtype=jnp.int32))
    mp = _in_proj(xp, ws, bf, cos_p, sin_p, "in_proj_prompt")
    r3 = lambda a: a.reshape(bsz, seq, a.shape[-1])

    c_p = _cumsum_time(r3(mp["lf"]).transpose(0, 2, 1), "cumsum_prompt")
    cc_p, cr_p = _pair_views(c_p)
    tq = _pick(seq, (512, 256))
    tk = _pick(seq, (512, 256))
    fox_o = _attention("fox", r3(mp["fq"]), r3(mp["fk"]), r3(mp["fv"]), tq=tq, tk=tk, q_off=0, k_off=0,
                       extra=(cc_p, cr_p), name="fox_prompt")

    nkv3 = r3(mp["nkv"])
    kc_p = _compress(nkv3, 0, cmp_k, "compress_k_prompt")
    vc_p = _compress(nkv3, 1, cmp_v, "compress_v_prompt")
    n_cmp_p = seq // CMP_STRIDE - 1
    n_slc_p = -(-seq // SLC_LEN)
    nbp_p = _round_up(n_slc_p, LANES)
    nq3 = r3(mp["nq"])
    oc_p, sel_p = _cmp_select(nq3, kc_p, vc_p, tq=256, n_cmp=n_cmp_p, n_slc=n_slc_p, nbp=nbp_p, q_off=0,
                              name="cmp_select_prompt")
    os_p = _attention("slc", nq3, r3(mp["ks"]), r3(mp["vs"]), tq=tq, tk=tk, q_off=0, k_off=0, extra=(sel_p,),
                      nbp=nbp_p, name="slc_prompt")
    ow_p = _attention("win", nq3, r3(mp["kw"]), r3(mp["vw"]), tq=256, tk=256, q_off=0, k_off=0, name="win_prompt")
    flat = lambda a: a.reshape(n, a.shape[-1])
    xp2 = _merge_ln(xp, flat(fox_o), flat(oc_p), flat(os_p), flat(ow_p), mp["ng"], mp["gf"], mp["gn"],
                    w_up_fox[layer], w_up_nsa[layer], w_out[layer], ln2_g[layer], ln2_b[layer], alpha, "merge_prompt")
    yp = _ffn_ln(xp2, ffn2_w_up[layer], ffn2_w_down[layer], ln3_g[layer], ln3_b[layer], alpha, "ffn2_prompt")

    ns_rows = sb * n_new
    xs = _ffn_ln(x_sample.reshape(ns_rows, d), ffn1_w_up[layer], ffn1_w_down[layer], ln1_g[layer], ln1_b[layer],
                 alpha, "ffn1_sample")
    cos_s, sin_s = _rope_tables(past + jnp.arange(n_new, dtype=jnp.int32))
    tile_rows = _pick(ns_rows, (256, 128, 8))
    reps = tile_rows // n_new
    ms = _in_proj(xs, ws, bf, jnp.tile(cos_s, (reps, 1)), jnp.tile(sin_s, (reps, 1)), "in_proj_sample")
    s3 = lambda a: a.reshape(sb, n_new, a.shape[-1])
    by_head = lambda a: s3(a).reshape(sb, n_new, FOX_HEADS, HEAD_DIM).transpose(0, 2, 1, 3)

    lf_past = _lf_pages(page_table, _rows_minor(cache_fox_logf[layer]))
    lf_new = s3(ms["lf"]).transpose(0, 2, 1)
    t_cs = _round_up(past + n_new, 8 * LANES)
    c_s = _cumsum_time(jnp.pad(jnp.concatenate([lf_past, lf_new], axis=2), ((0, 0), (0, 0), (0, t_cs - past - n_new))),
                       "cumsum_sample")
    c_new = c_s[:, :, past:past + n_new]
    fox_o_s = _fox_sample(page_table, _rows_minor(cache_fox_kv[layer]), by_head(ms["fq"]), by_head(ms["fk"]),
                          by_head(ms["fv"]), c_new[..., None], c_s, c_new)
    fox_o_s = fox_o_s.transpose(0, 2, 1, 3).reshape(ns_rows, FOX_WIDTH)

    kcr_s, vcr_s, ks_all, vs_all = _nsa_pages(page_table, _rows_minor(cache_nsa_kv[layer]), s3(ms["ks"]), s3(ms["vs"]))
    kc_s = _compress(kcr_s, 0, cmp_k, "compress_k_sample")
    vc_s = _compress(vcr_s, 0, cmp_v, "compress_v_sample")
    n_cmp_s = (past + n_new) // CMP_STRIDE - 1
    n_slc_s = -(-(past + n_new) // SLC_LEN)
    nbp_s = _round_up(n_slc_s, LANES)
    nq_s = s3(ms["nq"])
    oc_s, sel_s = _cmp_select(nq_s, kc_s, vc_s, tq=n_new, n_cmp=n_cmp_s, n_slc=n_slc_s, nbp=nbp_s, q_off=past,
                              name="cmp_select_sample")
    t_slc = ks_all.shape[1]
    tk_s = _pick(t_slc, (18 * LANES, 9 * LANES, 8 * LANES, LANES))
    os_s = _attention("slc", nq_s, ks_all, vs_all, tq=n_new, tk=tk_s, q_off=past, k_off=0, extra=(sel_s,),
                      nbp=nbp_s, name="slc_sample")
    t_win = _round_up(keep + n_new, LANES)
    kw_all, vw_all = _win_rows(_rows_minor(state_win_kv[layer]), s3(ms["kw"]), s3(ms["vw"]), t_win)
    ow_s = _attention("win", nq_s, kw_all, vw_all, tq=n_new, tk=t_win, q_off=past, k_off=past - keep,
                      name="win_sample")
    flat_s = lambda a: a.reshape(ns_rows, a.shape[-1])
    xs2 = _merge_ln(xs, fox_o_s, flat_s(oc_s), flat_s(os_s), flat_s(ow_s), ms["ng"], ms["gf"], ms["gn"],
                    w_up_fox[layer], w_up_nsa[layer], w_out[layer], ln2_g[layer], ln2_b[layer], alpha, "merge_sample")
    ys = _ffn_ln(xs2, ffn2_w_up[layer], ffn2_w_down[layer], ln3_g[layer], ln3_b[layer], alpha, "ffn2_sample")

    fox_kv_p = mp["fkv"].reshape(1, bsz, seq, 2, FOX_HEADS, HEAD_DIM)
    fox_kv_s = ms["fkv"].reshape(1, sb, n_new, 2, FOX_HEADS, HEAD_DIM)
    logf_p = mp["lf"].reshape(1, bsz, seq, FOX_HEADS)
    logf_s = ms["lf"].reshape(1, sb, n_new, FOX_HEADS)
    nsa_kv_p = mp["nkv"].reshape(1, bsz, seq, 4, NSA_GROUPS, HEAD_DIM)
    nsa_kv_s = ms["nkv"].reshape(1, sb, n_new, 4, NSA_GROUPS, HEAD_DIM)
    win_rows_p = r3(mp["win"]).reshape(bsz, seq, 2, NSA_GROUPS, HEAD_DIM)
    if seq >= keep:
        win_p = win_rows_p[:, seq - keep:]
    else:
        win_p = jnp.pad(win_rows_p, ((0, 0), (keep - seq, 0), (0, 0), (0, 0), (0, 0)))
    new_win = ms["win"].reshape(sb, n_new, 2, NSA_GROUPS, HEAD_DIM).astype(state_win_kv.dtype)
    win_s = jnp.concatenate([state_win_kv[layer], new_win], axis=1)[:, -keep:]
    return (yp.reshape(bsz, seq, d), ys.reshape(sb, n_new, d), fox_kv_p, fox_kv_s, logf_p, logf_s,
            nsa_kv_p, nsa_kv_s, win_p[None], win_s[None])
```

```python
import functools

import numpy as np
import jax
import jax.numpy as jnp
from jax import lax
from jax.experimental import pallas as pl
from jax.experimental.pallas import tpu as pltpu

F32 = jnp.float32
BF16 = jnp.bfloat16

HEAD_DIM = 64
FOX_HEADS = 8
NSA_HEADS = 8
NSA_GROUPS = 2
NSA_HPG = NSA_HEADS // NSA_GROUPS
FOX_WIDTH = FOX_HEADS * HEAD_DIM
NSA_WIDTH = NSA_HEADS * HEAD_DIM
NSA_KV_WIDTH = NSA_GROUPS * HEAD_DIM
N_NSA_BRANCHES = 3
CMP_LEN = 32
CMP_STRIDE = 16
CMP_HIDDEN = 128
SLC_LEN = 64
N_SELECT = 16
WINDOW = 512
PAGE_SIZE = 128
ROPE_THETA = 10000.0
LN_EPS = 1e-5
FORCE_BONUS = 1e4
NEG_INF = -1e30
M_INIT = 0.1 * NEG_INF
SCALE = HEAD_DIM ** -0.5

LANES = 128
N_PAIRS = 4
MIB = 1024 * 1024
HIGHEST = lax.Precision.HIGHEST
NT_DIMS = (((1,), (1,)), ((), ()))


def _params(semantics, vmem_mib):
    return pltpu.CompilerParams(dimension_semantics=semantics, vmem_limit_bytes=vmem_mib * MIB)


def _pick(n, candidates):
    for c in candidates:
        if n % c == 0:
            return c
    return n


def _round_up(n, m):
    return (n + m - 1) // m * m


def _layer_norm(y, g, b):
    mu = jnp.mean(y, axis=-1, keepdims=True)
    d = y - mu
    var = jnp.mean(d * d, axis=-1, keepdims=True)
    return d * lax.rsqrt(var + LN_EPS) * g + b


def _const_spec(shape):
    nd = len(shape)
    return pl.BlockSpec(shape, lambda *_: (0,) * nd, pipeline_mode=pl.Buffered(1))


def _ffn_ln_kernel(x_ref, wa_ref, wb_ref, wd_ref, g_ref, b_ref, o_ref, acc_ref, *, alpha, n_chunks):
    x = x_ref[...]
    xb = x.astype(BF16)
    acc_ref[...] = jnp.zeros_like(acc_ref)

    def body(c, carry):
        a = jnp.dot(xb, wa_ref[c], preferred_element_type=F32)
        b = jnp.dot(xb, wb_ref[c], preferred_element_type=F32)
        h = (a * jax.nn.sigmoid(a) * b).astype(BF16)
        acc_ref[...] += jnp.dot(h, wd_ref[c], preferred_element_type=F32)
        return carry

    lax.fori_loop(0, n_chunks, body, 0)
    o_ref[...] = _layer_norm(alpha * x + 0.5 * acc_ref[...], g_ref[...], b_ref[...])


def _ffn_ln(x, w_up, w_down, g, b, alpha, name):
    n, d = x.shape
    f = w_down.shape[0]
    fc = _pick(f, (256, 128))
    nc = f // fc
    wa = w_up[:, :f].astype(BF16).reshape(d, nc, fc).transpose(1, 0, 2)
    wb = w_up[:, f:].astype(BF16).reshape(d, nc, fc).transpose(1, 0, 2)
    wd = w_down.astype(BF16).reshape(nc, fc, d)
    tm = _pick(n, (1024, 512, 256, 128, 8))
    return pl.pallas_call(
        functools.partial(_ffn_ln_kernel, alpha=alpha, n_chunks=nc),
        grid=(n // tm,),
        in_specs=[pl.BlockSpec((tm, d), lambda i: (i, 0)),
                  _const_spec((nc, d, fc)), _const_spec((nc, d, fc)), _const_spec((nc, fc, d)),
                  _const_spec((1, d)), _const_spec((1, d))],
        out_specs=pl.BlockSpec((tm, d), lambda i: (i, 0)),
        out_shape=jax.ShapeDtypeStruct((n, d), F32),
        scratch_shapes=[pltpu.VMEM((tm, d), F32)],
        compiler_params=_params(("parallel",), 56),
        name=name,
    )(x, wa, wb, wd, g.reshape(1, d), b.reshape(1, d))


def _rope_tables(pos):
    half = HEAD_DIM // 2
    inv_freq = ROPE_THETA ** (-jnp.arange(half, dtype=F32) / half)
    ang = pos.astype(F32)[:, None] * inv_freq[None, :]
    cos, sin = jnp.cos(ang), jnp.sin(ang)
    cos64 = jnp.concatenate([cos, cos], axis=-1)
    sin64 = jnp.concatenate([-sin, sin], axis=-1)
    return jnp.tile(cos64, (1, LANES // HEAD_DIM)), jnp.tile(sin64, (1, LANES // HEAD_DIM))


def _in_proj_kernel(x_ref, wfq, wfkv, wsm, wnq, wnkv, wwin, wgf, wgn, bf_ref, cos_ref, sin_ref,
                    fq_o, fkv_o, fk_o, fv_o, lf_o, nq_o, nkv_o, ks_o, vs_o, win_o, kw_o, vw_o,
                    ng_o, gf_o, gn_o):
    xb = x_ref[...].astype(BF16)
    cos = cos_ref[...]
    sin = sin_ref[...]
    lane = lax.broadcasted_iota(jnp.int32, (1, LANES), 1)
    low_half = (lane % HEAD_DIM) < (HEAD_DIM // 2)

    def rope(v):
        partner = jnp.where(low_half, pltpu.roll(v, LANES - HEAD_DIM // 2, 1), pltpu.roll(v, HEAD_DIM // 2, 1))
        return v * cos + partner * sin

    def proj(w_ref):
        return jnp.dot(xb, w_ref[...], preferred_element_type=F32)

    fq_o[...] = (proj(wfq) * SCALE).astype(BF16)
    fkv = proj(wfkv)
    fkv_o[...] = fkv
    fk_o[...] = fkv[:, :FOX_WIDTH].astype(BF16)
    fv_o[...] = fkv[:, FOX_WIDTH:].astype(BF16)

    sm = proj(wsm)
    z = sm[:, :LANES] + bf_ref[...]
    logf = jnp.minimum(z, 0.0) - jnp.log1p(jnp.exp(-jnp.abs(z)))
    lf_o[...] = logf[:, :FOX_HEADS]
    ng_o[...] = sm[:, LANES:]

    nq = proj(wnq)
    for r in range(N_PAIRS):
        nq_o[:, r * LANES:(r + 1) * LANES] = (rope(nq[:, r * LANES:(r + 1) * LANES]) * SCALE).astype(BF16)

    nkv = proj(wnkv)
    k_cmp = rope(nkv[:, 0:LANES])
    k_slc = rope(nkv[:, 2 * LANES:3 * LANES])
    v_slc = nkv[:, 3 * LANES:4 * LANES]
    nkv_o[:, 0:LANES] = k_cmp
    nkv_o[:, LANES:2 * LANES] = nkv[:, LANES:2 * LANES]
    nkv_o[:, 2 * LANES:3 * LANES] = k_slc
    nkv_o[:, 3 * LANES:4 * LANES] = v_slc
    ks_o[...] = k_slc.astype(BF16)
    vs_o[...] = v_slc.astype(BF16)

    win = proj(wwin)
    k_win = rope(win[:, :LANES])
    v_win = win[:, LANES:]
    win_o[:, :LANES] = k_win
    win_o[:, LANES:] = v_win
    kw_o[...] = k_win.astype(BF16)
    vw_o[...] = v_win.astype(BF16)

    gf_o[...] = proj(wgf)
    gn_o[...] = proj(wgn)


def _nsa_perm():
    perm = np.zeros(NSA_WIDTH, np.int32)
    for r in range(NSA_HPG):
        for g in range(NSA_GROUPS):
            for d in range(HEAD_DIM):
                perm[r * LANES + g * HEAD_DIM + d] = (g * NSA_HPG + r) * HEAD_DIM + d
    return perm


def _split_w_in(w_in, b_fgate):
    d = w_in.shape[0]
    sizes = (FOX_WIDTH, FOX_WIDTH, FOX_WIDTH, FOX_HEADS, NSA_WIDTH) + (NSA_KV_WIDTH,) * 6 + (
        N_NSA_BRANCHES * NSA_HEADS, d, d)
    offs = np.concatenate([[0], np.cumsum(sizes)])
    col = lambda i, j=None: w_in[:, offs[i]:offs[(i if j is None else j) + 1]]
    pad = lambda w: jnp.pad(w, ((0, 0), (0, LANES - w.shape[1])))
    ws = dict(
        wfq=col(0), wfkv=col(1, 2),
        wsm=jnp.concatenate([pad(col(3)), pad(col(11))], axis=1),
        wnq=col(4)[:, _nsa_perm()], wnkv=col(5, 8), wwin=col(9, 10), wgf=col(12), wgn=col(13))
    ws = {k: v.astype(BF16) for k, v in ws.items()}
    bf = jnp.pad(b_fgate.astype(F32), (0, LANES - FOX_HEADS)).reshape(1, LANES)
    return ws, bf


def _in_proj(x, ws, bf, cos_tab, sin_tab, name):
    n, d = x.shape
    n_tab = cos_tab.shape[0]
    tm = _pick(n_tab, (256, 128, 8))
    tab_tiles = n_tab // tm
    row = lambda w: pl.BlockSpec((tm, w), lambda i: (i, 0))
    tab = pl.BlockSpec((tm, LANES), lambda i: (i % tab_tiles, 0))
    names = ("wfq", "wfkv", "wsm", "wnq", "wnkv", "wwin", "wgf", "wgn")
    outs = [("fq", FOX_WIDTH, BF16), ("fkv", 2 * FOX_WIDTH, F32), ("fk", FOX_WIDTH, BF16), ("fv", FOX_WIDTH, BF16),
            ("lf", FOX_HEADS, F32), ("nq", NSA_WIDTH, BF16), ("nkv", 4 * NSA_KV_WIDTH, F32),
            ("ks", NSA_KV_WIDTH, BF16), ("vs", NSA_KV_WIDTH, BF16), ("win", 2 * NSA_KV_WIDTH, F32),
            ("kw", NSA_KV_WIDTH, BF16), ("vw", NSA_KV_WIDTH, BF16), ("ng", LANES, F32), ("gf", d, F32), ("gn", d, F32)]
    res = pl.pallas_call(
        _in_proj_kernel,
        grid=(n // tm,),
        in_specs=[row(d)] + [_const_spec(ws[k].shape) for k in names] + [_const_spec((1, LANES)), tab, tab],
        out_specs=[row(w) for _, w, _ in outs],
        out_shape=[jax.ShapeDtypeStruct((n, w), dt) for _, w, dt in outs],
        compiler_params=_params(("parallel",), 56),
        name=name,
    )(x, *[ws[k] for k in names], bf, cos_tab, sin_tab)
    return {k: v for (k, _, _), v in zip(outs, res)}


def _cumsum_kernel(x_ref, tri_ref, low_ref, o_ref):
    x = x_ref[0]
    within = lax.dot_general(x, tri_ref[...], (((1,), (0,)), ((), ())), precision=HIGHEST, preferred_element_type=F32)
    tot = jnp.broadcast_to(within[:, LANES - 1:LANES], within.shape)
    before = lax.dot_general(low_ref[...], tot, (((1,), (0,)), ((), ())), precision=HIGHEST, preferred_element_type=F32)
    o_ref[0] = within + before


def _cumsum_time(logf_t, name):
    b, h, t = logf_t.shape
    r = t // LANES
    rows = h * r
    idx = np.arange(LANES)
    tri = jnp.asarray((idx[:, None] <= idx[None, :]).astype(np.float32))
    ridx = np.arange(rows)
    low = jnp.asarray(((ridx[None, :] < ridx[:, None]) & (ridx[None, :] // r == ridx[:, None] // r)).astype(np.float32))
    out = pl.pallas_call(
        _cumsum_kernel,
        grid=(b,),
        in_specs=[pl.BlockSpec((1, rows, LANES), lambda i: (i, 0, 0)), _const_spec((LANES, LANES)),
                  _const_spec((rows, rows))],
        out_specs=pl.BlockSpec((1, rows, LANES), lambda i: (i, 0, 0)),
        out_shape=jax.ShapeDtypeStruct((b, rows, LANES), F32),
        compiler_params=_params(("parallel",), 32),
        name=name,
    )(logf_t.reshape(b, rows, LANES), tri, low)
    return out.reshape(b, h, t)


FLAG_FIRST, FLAG_LAST, FLAG_EDGE = 1, 2, 4


def _tile_schedule(mode, nq, nkt, tq, tk, q_off, k_off):
    qi_tab, kt_tab, flag_tab = [], [], []
    for qi in range(nq):
        q_lo, q_hi = q_off + qi * tq, q_off + (qi + 1) * tq - 1
        tiles = []
        for kt in range(nkt):
            k_lo, k_hi = k_off + kt * tk, k_off + (kt + 1) * tk - 1
            if mode == "win":
                visible = (q_hi - k_lo >= 0) and (q_lo - k_hi < WINDOW)
                interior = (q_lo - k_hi >= 0) and (q_hi - k_lo < WINDOW)
            else:
                visible = k_lo <= q_hi
                interior = k_hi <= q_lo
            if visible:
                tiles.append((kt, 0 if interior else FLAG_EDGE))
        assert tiles, "every query tile sees at least one key tile"
        for n, (kt, flag) in enumerate(tiles):
            qi_tab.append(qi)
            kt_tab.append(kt)
            flag_tab.append(flag | (FLAG_FIRST if n == 0 else 0) | (FLAG_LAST if n == len(tiles) - 1 else 0))
    as_i32 = lambda v: jnp.asarray(np.asarray(v, np.int32))
    return as_i32(qi_tab), as_i32(kt_tab), as_i32(flag_tab)


def _attn_kernel(qi_tab, kt_tab, flag_tab, *refs, mode, tq, tk, pps, q_off, k_off, nbp):
    if mode == "fox":
        q_ref, k_ref, v_ref, cc_ref, cr_ref, o_ref, qm_ref, m_ref, l_ref, acc_ref, cq_ref = refs
    elif mode == "slc":
        q_ref, k_ref, v_ref, sel_ref, o_ref, qm_ref, m_ref, l_ref, acc_ref = refs
    else:
        q_ref, k_ref, v_ref, o_ref, qm_ref, m_ref, l_ref, acc_ref = refs
    step = pl.program_id(2)
    qi = qi_tab[step]
    kt = kt_tab[step]
    flags = flag_tab[step]
    lane = lax.broadcasted_iota(jnp.int32, (1, LANES), 1)
    left = lane < HEAD_DIM
    n_chunks = tk // LANES
    heads = [(pr, hh) for pr in range(pps) for hh in range(2)]

    @pl.when((flags & FLAG_FIRST) != 0)
    def _():
        m_ref[...] = jnp.full_like(m_ref, M_INIT)
        l_ref[...] = jnp.zeros_like(l_ref)
        acc_ref[...] = jnp.zeros_like(acc_ref)
        for pr, hh in heads:
            qp = q_ref[0, :, pr * LANES:(pr + 1) * LANES]
            qm_ref[2 * pr + hh] = jnp.where(left if hh == 0 else jnp.logical_not(left), qp, jnp.zeros_like(qp))
            if mode == "fox":
                cq_ref[2 * pr + hh] = jnp.broadcast_to(cc_ref[0, 0][:, hh:hh + 1], (tq, LANES))

    def tile(edge):
        v = v_ref[0]
        if edge:
            q_pos = q_off + qi * tq + lax.broadcasted_iota(jnp.int32, (tq, 1), 0)
        if mode == "slc":
            blk_row = lax.broadcasted_iota(jnp.int32, (nbp, tk), 0)
            blk_key = (kt * tk + lax.broadcasted_iota(jnp.int32, (nbp, tk), 1)) // SLC_LEN
            expand = (blk_row == blk_key).astype(BF16)
            sel_bias = [jnp.dot(sel_ref[0, :, g * nbp:(g + 1) * nbp], expand, preferred_element_type=F32)
                        for g in range(NSA_GROUPS)]
        for pr, hh in heads:
            hd = 2 * pr + hh
            k = k_ref[0, :, pr * LANES:(pr + 1) * LANES] if mode == "fox" else k_ref[0]
            s = lax.dot_general(qm_ref[hd], k, NT_DIMS, preferred_element_type=F32)
            chunks = []
            for c in range(n_chunks):
                t = s[:, c * LANES:(c + 1) * LANES]
                if mode == "fox":
                    t = t + (cq_ref[hd] - cr_ref[0, 0][hh:hh + 1, c * LANES:(c + 1) * LANES])
                if mode == "slc":
                    t = t + sel_bias[hh][:, c * LANES:(c + 1) * LANES]
                if edge:
                    k_pos = k_off + kt * tk + c * LANES + lane
                    if mode == "win":
                        rel = q_pos - k_pos
                        t = jnp.where((rel >= 0) & (rel < WINDOW), t, NEG_INF)
                    else:
                        t = jnp.where(k_pos <= q_pos, t, NEG_INF)
                chunks.append(t)
            mx = chunks[0]
            for t in chunks[1:]:
                mx = jnp.maximum(mx, t)
            m_prev = m_ref[hd]
            m_new = jnp.maximum(m_prev, jnp.max(mx, axis=1, keepdims=True))
            alpha = jnp.exp(m_prev - m_new)
            ps = [jnp.exp(t - m_new) for t in chunks]
            lsum = ps[0]
            for p in ps[1:]:
                lsum = lsum + p
            l_ref[hd] = alpha * l_ref[hd] + lsum
            pb = jnp.concatenate([p.astype(BF16) for p in ps], axis=1) if n_chunks > 1 else ps[0].astype(BF16)
            vv = v[:, pr * LANES:(pr + 1) * LANES] if mode == "fox" else v
            acc_ref[hd] = alpha * acc_ref[hd] + jnp.dot(pb, vv, preferred_element_type=F32)
            m_ref[hd] = m_new

    pl.when((flags & FLAG_EDGE) != 0)(lambda: tile(True))
    pl.when((flags & FLAG_EDGE) == 0)(lambda: tile(False))

    @pl.when((flags & FLAG_LAST) != 0)
    def _():
        for pr in range(pps):
            outs = []
            for hh in range(2):
                total = jnp.sum(l_ref[2 * pr + hh], axis=1, keepdims=True)
                outs.append(acc_ref[2 * pr + hh] / jnp.maximum(total, 1e-30))
            o_ref[0, :, pr * LANES:(pr + 1) * LANES] = jnp.where(left, outs[0], outs[1])


def _attention(mode, q, k, v, *, tq, tk, q_off, k_off, extra=(), nbp=0, name):
    b, t_q, _ = q.shape
    t_k = k.shape[1]
    pps = 1 if mode == "fox" else N_PAIRS
    qi_tab, kt_tab, flag_tab = _tile_schedule(mode, t_q // tq, t_k // tk, tq, tk, q_off, k_off)
    n_steps = qi_tab.shape[0]
    width = pps * LANES
    q_spec = pl.BlockSpec((1, tq, width), lambda bi, p, s, qt, kt, fl: (bi, qt[s], p))
    kv_spec = pl.BlockSpec((1, tk, LANES), lambda bi, p, s, qt, kt, fl: (bi, kt[s], p))
    in_specs = [q_spec, kv_spec, kv_spec]
    scratch = [pltpu.VMEM((2 * pps, tq, LANES), BF16), pltpu.VMEM((2 * pps, tq, LANES), F32),
               pltpu.VMEM((2 * pps, tq, LANES), F32), pltpu.VMEM((2 * pps, tq, LANES), F32)]
    if mode == "fox":
        in_specs += [pl.BlockSpec((1, 1, tq, 2), lambda bi, p, s, qt, kt, fl: (bi, p, qt[s], 0)),
                     pl.BlockSpec((1, 1, 2, tk), lambda bi, p, s, qt, kt, fl: (bi, p, 0, kt[s]))]
        scratch += [pltpu.VMEM((2 * pps, tq, LANES), F32)]
    elif mode == "slc":
        in_specs += [pl.BlockSpec((1, tq, NSA_GROUPS * nbp), lambda bi, p, s, qt, kt, fl: (bi, qt[s], 0))]
    grid_spec = pltpu.PrefetchScalarGridSpec(
        num_scalar_prefetch=3, grid=(b, N_PAIRS // pps, n_steps), in_specs=in_specs,
        out_specs=pl.BlockSpec((1, tq, width), lambda bi, p, s, qt, kt, fl: (bi, qt[s], p)),
        scratch_shapes=scratch)
    return pl.pallas_call(
        functools.partial(_attn_kernel, mode=mode, tq=tq, tk=tk, pps=pps, q_off=q_off, k_off=k_off, nbp=nbp),
        grid_spec=grid_spec,
        out_shape=jax.ShapeDtypeStruct((b, t_q, N_PAIRS * LANES), F32),
        compiler_params=_params(("parallel", "parallel", "arbitrary"), 48),
        name=name,
    )(qi_tab, kt_tab, flag_tab, q, k, v, *extra)


def _compress_rows(row_of_half, pa_ref, pb_ref, wa_ref, wb_ref, w2_ref, ns):
    u = jnp.concatenate([row_of_half(r) for r in range(CMP_STRIDE)], axis=1)
    first = jnp.dot((u + pa_ref[...]).astype(BF16), wa_ref[...], preferred_element_type=F32)
    second = jnp.dot((u + pb_ref[...]).astype(BF16), wb_ref[...], preferred_element_type=F32)
    pre = first + pltpu.roll(second, ns - 1, 0)
    h = (pre * jax.nn.sigmoid(pre)).astype(BF16)
    return jnp.dot(h, w2_ref[...], preferred_element_type=F32).astype(BF16)


def _compress_kernel(rows_ref, pa_ref, pb_ref, wa_ref, wb_ref, w2_ref, o_ref, *, ns):
    o_ref[0] = _compress_rows(lambda r: rows_ref[0, pl.ds(r, ns, stride=CMP_STRIDE), :],
                              pa_ref, pb_ref, wa_ref, wb_ref, w2_ref, ns)


def _compress_weights(pos, w1, w2):
    ratio = CMP_LEN // CMP_STRIDE
    assert ratio == 2
    w1r = w1.reshape(CMP_LEN, HEAD_DIM, CMP_HIDDEN)
    zeros = jnp.zeros((CMP_STRIDE, HEAD_DIM, CMP_HIDDEN), w1.dtype)

    def half(rows):
        g0 = jnp.concatenate([jnp.stack([rows, zeros], axis=1).reshape(-1, CMP_HIDDEN),
                              jnp.stack([zeros, rows], axis=1).reshape(-1, CMP_HIDDEN)], axis=1)
        return g0.astype(BF16)

    def pos_tab(p):
        return jnp.stack([p, p], axis=1).reshape(1, -1).astype(F32)

    z2 = jnp.zeros_like(w2)
    w2d = jnp.concatenate([jnp.concatenate([w2, z2], axis=1), jnp.concatenate([z2, w2], axis=1)], axis=0)
    return (pos_tab(pos[:CMP_STRIDE]), pos_tab(pos[CMP_STRIDE:]), half(w1r[:CMP_STRIDE]), half(w1r[CMP_STRIDE:]),
            w2d.astype(BF16))


def _compress(rows, col_block, weights, name):
    b, t, _ = rows.shape
    ns = t // CMP_STRIDE
    width = CMP_STRIDE * LANES
    pa, pb, wa, wb, w2d = weights
    return pl.pallas_call(
        functools.partial(_compress_kernel, ns=ns),
        grid=(b,),
        in_specs=[pl.BlockSpec((1, t, LANES), lambda i: (i, 0, col_block)), _const_spec((1, width)),
                  _const_spec((1, width)), _const_spec(wa.shape), _const_spec(wb.shape), _const_spec(w2d.shape)],
        out_specs=pl.BlockSpec((1, ns, LANES), lambda i: (i, 0, 0)),
        out_shape=jax.ShapeDtypeStruct((b, ns, LANES), BF16),
        compiler_params=_params(("parallel",), 48),
        name=name,
    )(rows, pa, pb, wa, wb, w2d)


def _cmp_select_kernel(q_ref, kc_ref, vc_ref, cov_ref, oc_ref, sel_ref, imp_ref, vt_ref, *,
                       sps, tq, rows_p, nc, n_cmp, n_slc, nbp, q_off, n_sel):
    qi = pl.program_id(1)
    lane = lax.broadcasted_iota(jnp.int32, (1, LANES), 1)
    left = lane < HEAD_DIM
    q_pos = q_off + qi * tq + lax.broadcasted_iota(jnp.int32, (tq, 1), 0)
    n_idx = lax.broadcasted_iota(jnp.int32, (1, nc), 1)
    cmask = ((n_idx * CMP_STRIDE + CMP_LEN - 1) <= q_pos) & (n_idx < n_cmp)
    cov = cov_ref[...]
    if rows_p > sps * tq:
        imp_ref[...] = jnp.zeros_like(imp_ref)

    def attend(sq, carry):
        kc = kc_ref[sq]
        vc = vc_ref[sq]
        imp = [jnp.zeros((tq, nbp), F32) for _ in range(NSA_GROUPS)]
        for r in range(N_PAIRS):
            qp = q_ref[sq, :, r * LANES:(r + 1) * LANES]
            halves = []
            for g in range(NSA_GROUPS):
                qh = jnp.where(left if g == 0 else jnp.logical_not(left), qp, jnp.zeros_like(qp))
                s = lax.dot_general(qh, kc, NT_DIMS, preferred_element_type=F32)
                s = jnp.where(cmask, s, NEG_INF)
                m = jnp.max(s, axis=1, keepdims=True)
                e = jnp.where(cmask, jnp.exp(s - m), 0.0)
                pb = (e / jnp.maximum(jnp.sum(e, axis=1, keepdims=True), 1e-30)).astype(BF16)
                halves.append(jnp.dot(pb, vc, preferred_element_type=F32))
                imp[g] = imp[g] + jnp.dot(pb, cov, preferred_element_type=F32)
            oc_ref[sq, :, r * LANES:(r + 1) * LANES] = jnp.where(left, halves[0], halves[1])
        first_row = pl.multiple_of(sq * tq, tq)
        for g in range(NSA_GROUPS):
            imp_ref[g, pl.ds(first_row, tq), :] = imp[g]
        return carry

    if sps == 1:
        attend(0, 0)
    else:
        lax.fori_loop(0, sps, attend, 0)

    pos = q_off + qi * tq + lax.broadcasted_iota(jnp.int32, (rows_p, 1), 0) % tq
    blk = lax.broadcasted_iota(jnp.int32, (1, nbp), 1)
    cur = pos // SLC_LEN
    forced = (blk == 0) | (blk == cur) | (blk == cur - 1)
    valid = (blk * SLC_LEN <= pos) & (blk < n_slc)
    nbr = _round_up(n_slc, 8)
    row = lax.broadcasted_iota(jnp.int32, (nbr, rows_p), 0)
    n_live = jnp.minimum(n_slc, (q_off + (qi + 1) * tq - 1) // SLC_LEN + 1)
    for g in range(NSA_GROUPS):
        val = jnp.where(valid, jnp.where(forced, imp_ref[g] + FORCE_BONUS, imp_ref[g]), NEG_INF)
        vt_ref[...] = val.T

        def body(i, cnt):
            vi = vt_ref[pl.ds(i, 1), :]
            vt = vt_ref[0:nbr, :]
            ahead = (vi > vt) | ((vi == vt) & (i < row))
            return cnt + ahead.astype(F32)

        cnt = lax.fori_loop(0, n_live, body, jnp.zeros((nbr, rows_p), F32))
        bias = jnp.where(cnt < n_sel, 0.0, NEG_INF)
        if nbp > nbr:
            bias = jnp.concatenate([bias, jnp.full((nbp - nbr, rows_p), NEG_INF, F32)], axis=0)
        bias_t = bias.T
        for sq in range(sps):
            sel_ref[sq, :, g * nbp:(g + 1) * nbp] = bias_t[sq * tq:(sq + 1) * tq].astype(BF16)


def _coverage(n_cmp, nc, n_slc, nbp):
    c0 = np.arange(nc) * CMP_STRIDE
    s0 = np.arange(nbp) * SLC_LEN
    lo = np.maximum(c0[:, None], s0[None, :])
    hi = np.minimum(c0[:, None] + CMP_LEN, s0[None, :] + SLC_LEN)
    cov = np.maximum(hi - lo, 0).astype(np.float32) / CMP_LEN
    cov[n_cmp:, :] = 0.0
    cov[:, n_slc:] = 0.0
    return jnp.asarray(cov, dtype=BF16)


def _cmp_select(q, kc, vc, *, tq, sps, n_cmp, n_slc, nbp, q_off, name):
    b, t_q, _ = q.shape
    nc = kc.shape[1]
    rows_p = _round_up(sps * tq, LANES)
    cov = _coverage(n_cmp, nc, n_slc, nbp)
    kern = functools.partial(_cmp_select_kernel, sps=sps, tq=tq, rows_p=rows_p, nc=nc, n_cmp=n_cmp, n_slc=n_slc,
                             nbp=nbp, q_off=q_off, n_sel=min(N_SELECT, n_slc))
    return pl.pallas_call(
        kern,
        grid=(b // sps, t_q // tq),
        in_specs=[pl.BlockSpec((sps, tq, N_PAIRS * LANES), lambda bi, qi: (bi, qi, 0)),
                  pl.BlockSpec((sps, nc, LANES), lambda bi, qi: (bi, 0, 0)),
                  pl.BlockSpec((sps, nc, LANES), lambda bi, qi: (bi, 0, 0)),
                  _const_spec((nc, nbp))],
        out_specs=[pl.BlockSpec((sps, tq, N_PAIRS * LANES), lambda bi, qi: (bi, qi, 0)),
                   pl.BlockSpec((sps, tq, NSA_GROUPS * nbp), lambda bi, qi: (bi, qi, 0))],
        out_shape=[jax.ShapeDtypeStruct((b, t_q, N_PAIRS * LANES), F32),
                   jax.ShapeDtypeStruct((b, t_q, NSA_GROUPS * nbp), BF16)],
        scratch_shapes=[pltpu.VMEM((NSA_GROUPS, rows_p, nbp), F32), pltpu.VMEM((nbp, rows_p), F32)],
        compiler_params=_params(("parallel", "parallel"), 40),
        name=name,
    )(q, kc, vc, cov)


def _merge_kernel(x_ref, fo_ref, oc_ref, os_ref, ow_ref, ng_ref, gf_ref, gn_ref, wuf_ref, wun_ref, wout_ref,
                  eg_ref, g_ref, b_ref, o_ref, *, alpha):
    gates = jax.nn.sigmoid(ng_ref[...])
    gx = lax.dot_general(gates, eg_ref[...], (((1,), (0,)), ((), ())), precision=HIGHEST, preferred_element_type=F32)
    w = NSA_WIDTH
    nsa_o = gx[:, :w] * oc_ref[...] + gx[:, w:2 * w] * os_ref[...] + gx[:, 2 * w:] * ow_ref[...]
    up_f = jnp.dot(fo_ref[...].astype(BF16), wuf_ref[...], preferred_element_type=F32)
    up_n = jnp.dot(nsa_o.astype(BF16), wun_ref[...], preferred_element_type=F32)
    mixed = jax.nn.sigmoid(gf_ref[...]) * up_f + jax.nn.sigmoid(gn_ref[...]) * up_n
    mix = jnp.dot(mixed.astype(BF16), wout_ref[...], preferred_element_type=F32)
    o_ref[...] = _layer_norm(alpha * x_ref[...] + mix, g_ref[...], b_ref[...])


def _gate_expand():
    perm = _nsa_perm()
    e = np.zeros((LANES, N_NSA_BRANCHES * NSA_WIDTH), np.float32)
    for br in range(N_NSA_BRANCHES):
        for pos in range(NSA_WIDTH):
            head = perm[pos] // HEAD_DIM
            e[br * NSA_HEADS + head, br * NSA_WIDTH + pos] = 1.0
    return jnp.asarray(e)


def _merge_ln(x, fox_o, o_c, o_s, o_w, ng, gf, gn, w_up_fox, w_up_nsa, w_out, g, b, alpha, name):
    n, d = x.shape
    tm = _pick(n, (256, 128, 8))
    row = lambda w: pl.BlockSpec((tm, w), lambda i: (i, 0))
    wuf = w_up_fox.astype(BF16)
    wun = w_up_nsa[_nsa_perm(), :].astype(BF16)
    wout = w_out.astype(BF16)
    eg = _gate_expand()
    return pl.pallas_call(
        functools.partial(_merge_kernel, alpha=alpha),
        grid=(n // tm,),
        in_specs=[row(d), row(FOX_WIDTH), row(NSA_WIDTH), row(NSA_WIDTH), row(NSA_WIDTH), row(LANES), row(d), row(d),
                  _const_spec(wuf.shape), _const_spec(wun.shape), _const_spec(wout.shape), _const_spec(eg.shape),
                  _const_spec((1, d)), _const_spec((1, d))],
        out_specs=row(d),
        out_shape=jax.ShapeDtypeStruct((n, d), F32),
        compiler_params=_params(("parallel",), 48),
        name=name,
    )(x, fox_o, o_c, o_s, o_w, ng, gf, gn, wuf, wun, wout, eg, g.reshape(1, d), b.reshape(1, d))


def _page_specs(block, group, n_pages, second=0):
    nd = len(block)

    def spec(slot):
        return pl.BlockSpec(block, lambda si, j, pt: (pt[si * n_pages + jnp.minimum(j * group + slot, n_pages - 1)],
                                                      second) + (0,) * (nd - 2))

    return [spec(slot) for slot in range(group)]


def _lf_pages_kernel(pt_ref, *refs, group):
    o_ref = refs[group]
    for i in range(group):
        o_ref[0, :, i * PAGE_SIZE:(i + 1) * PAGE_SIZE] = refs[i][0]


def _lf_pages(page_table, cache_lf_t):
    s, n_pages = page_table.shape
    group = _pick(n_pages, (16, 8, 4, 2, 1))
    grid_spec = pltpu.PrefetchScalarGridSpec(
        num_scalar_prefetch=1, grid=(s, n_pages // group),
        in_specs=_page_specs((1, FOX_HEADS, PAGE_SIZE), group, n_pages),
        out_specs=pl.BlockSpec((1, FOX_HEADS, group * PAGE_SIZE), lambda si, j, pt: (si, 0, j)))
    return pl.pallas_call(
        functools.partial(_lf_pages_kernel, group=group),
        grid_spec=grid_spec,
        out_shape=jax.ShapeDtypeStruct((s, FOX_HEADS, n_pages * PAGE_SIZE), F32),
        compiler_params=_params(("parallel", "arbitrary"), 32),
        name="lf_pages",
    )(page_table.reshape(-1), *([cache_lf_t] * group))


def _fox_sample_kernel(pt_ref, q_ref, cc_ref, cr_ref, crn_ref, kn_ref, vn_ref, *refs, group, n_steps, n_new):
    pages = refs[:group]
    o_ref, m_ref, l_ref, acc_ref = refs[group:]
    j = pl.program_id(1)

    @pl.when(j == 0)
    def _():
        m_ref[...] = jnp.full_like(m_ref, M_INIT)
        l_ref[...] = jnp.zeros_like(l_ref)
        acc_ref[...] = jnp.zeros_like(acc_ref)

    def update(s, value_dots):
        m_prev = m_ref[...]
        m_new = jnp.maximum(m_prev, jnp.max(s, axis=1, keepdims=True))
        p = jnp.exp(s - m_new)
        alpha = jnp.exp(m_prev - m_new)
        l_ref[...] = alpha * l_ref[...] + jnp.sum(p, axis=1, keepdims=True)
        acc_ref[...] = alpha * acc_ref[...] + jnp.concatenate(
            [value_dots(h, p[h * n_new:(h + 1) * n_new].astype(BF16)) for h in range(FOX_HEADS)], axis=0)
        m_ref[...] = m_new

    def tiles(kv, h):
        return jnp.concatenate([pages[i][0, kv, h].astype(BF16) for i in range(group)], axis=1)

    scores = [jnp.dot(q_ref[0, h], tiles(0, h), preferred_element_type=F32) + (cc_ref[0, h] - cr_ref[0, h:h + 1, :])
              for h in range(FOX_HEADS)]
    update(jnp.concatenate(scores, axis=0),
           lambda h, pb: lax.dot_general(pb, tiles(1, h), NT_DIMS, preferred_element_type=F32))

    @pl.when(j == n_steps - 1)
    def _():
        t_q = lax.broadcasted_iota(jnp.int32, (n_new, n_new), 0)
        t_k = lax.broadcasted_iota(jnp.int32, (n_new, n_new), 1)
        new_scores = []
        for h in range(FOX_HEADS):
            s = lax.dot_general(q_ref[0, h], kn_ref[0, h], NT_DIMS, preferred_element_type=F32)
            new_scores.append(jnp.where(t_k <= t_q, s + (cc_ref[0, h] - crn_ref[0, h:h + 1, :]), NEG_INF))
        update(jnp.concatenate(new_scores, axis=0),
               lambda h, pb: jnp.dot(pb, vn_ref[0, h], preferred_element_type=F32))
        o_ref[0] = acc_ref[...] / jnp.maximum(l_ref[...], 1e-30)


def _fox_sample(page_table, cache_kv_t, q_h, k_new_h, v_new_h, c_q, c_row, c_new):
    s, n_pages = page_table.shape
    n_new = q_h.shape[2]
    group = _pick(n_pages, (8, 4, 2, 1))
    n_steps = n_pages // group
    rows = FOX_HEADS * n_new
    seq4 = lambda shape: pl.BlockSpec(shape, lambda si, j, pt: (si, 0, 0, 0))
    grid_spec = pltpu.PrefetchScalarGridSpec(
        num_scalar_prefetch=1, grid=(s, n_steps),
        in_specs=[seq4((1, FOX_HEADS, n_new, HEAD_DIM)), seq4((1, FOX_HEADS, n_new, 1)),
                  pl.BlockSpec((1, FOX_HEADS, group * PAGE_SIZE), lambda si, j, pt: (si, 0, j)),
                  pl.BlockSpec((1, FOX_HEADS, n_new), lambda si, j, pt: (si, 0, 0)),
                  seq4((1, FOX_HEADS, n_new, HEAD_DIM)), seq4((1, FOX_HEADS, n_new, HEAD_DIM))]
        + _page_specs((1, 2, FOX_HEADS, HEAD_DIM, PAGE_SIZE), group, n_pages),
        out_specs=pl.BlockSpec((1, rows, HEAD_DIM), lambda si, j, pt: (si, 0, 0)),
        scratch_shapes=[pltpu.VMEM((rows, 1), F32), pltpu.VMEM((rows, 1), F32), pltpu.VMEM((rows, HEAD_DIM), F32)])
    out = pl.pallas_call(
        functools.partial(_fox_sample_kernel, group=group, n_steps=n_steps, n_new=n_new),
        grid_spec=grid_spec,
        out_shape=jax.ShapeDtypeStruct((s, rows, HEAD_DIM), F32),
        compiler_params=_params(("parallel", "arbitrary"), 40),
        name="fox_sample",
    )(page_table.reshape(-1), q_h, c_q, c_row, c_new, k_new_h, v_new_h, *([cache_kv_t] * group))
    return out.reshape(s, FOX_HEADS, n_new, HEAD_DIM)


def _to_rows(tile):
    g, d, n = tile.shape
    return tile.reshape(g * d, n).T


def _compress_pages_kernel(pt_ref, pak, pbk, wak, wbk, w2k, pav, pbv, wav, wbv, w2v, *refs, group, n_steps, ns):
    pages = refs[:group]
    kc_o, vc_o, rk_ref, rv_ref = refs[group:]
    j = pl.program_id(1)
    for i in range(group):
        first = pl.multiple_of((j * group + i) * PAGE_SIZE, PAGE_SIZE)
        rk_ref[pl.ds(first, PAGE_SIZE), :] = _to_rows(pages[i][0, 0])
        rv_ref[pl.ds(first, PAGE_SIZE), :] = _to_rows(pages[i][0, 1])

    @pl.when(j == n_steps - 1)
    def _():
        half_rows = lambda ref: (lambda r: ref[pl.ds(r, ns, stride=CMP_STRIDE), :])
        kc_o[0] = _compress_rows(half_rows(rk_ref), pak, pbk, wak, wbk, w2k, ns)
        vc_o[0] = _compress_rows(half_rows(rv_ref), pav, pbv, wav, wbv, w2v, ns)


def _compress_pages(page_table, cache_kv_t, cmp_k, cmp_v):
    s, n_pages = page_table.shape
    group = _pick(n_pages, (8, 4, 2, 1))
    n_steps = n_pages // group
    past = n_pages * PAGE_SIZE
    ns = past // CMP_STRIDE
    consts = list(cmp_k) + list(cmp_v)
    out = pl.BlockSpec((1, ns, LANES), lambda si, j, pt: (si, 0, 0))
    grid_spec = pltpu.PrefetchScalarGridSpec(
        num_scalar_prefetch=1, grid=(s, n_steps),
        in_specs=[_const_spec(c.shape) for c in consts]
        + _page_specs((1, 2, NSA_GROUPS, HEAD_DIM, PAGE_SIZE), group, n_pages, second=0),
        out_specs=[out, out],
        scratch_shapes=[pltpu.VMEM((past, LANES), F32), pltpu.VMEM((past, LANES), F32)])
    return pl.pallas_call(
        functools.partial(_compress_pages_kernel, group=group, n_steps=n_steps, ns=ns),
        grid_spec=grid_spec,
        out_shape=[jax.ShapeDtypeStruct((s, ns, LANES), BF16), jax.ShapeDtypeStruct((s, ns, LANES), BF16)],
        compiler_params=_params(("parallel", "arbitrary"), 48),
        name="compress_pages",
    )(page_table.reshape(-1), *consts, *([cache_kv_t] * group))


def _slc_sample_kernel(pt_ref, q_ref, sel_ref, kn_ref, vn_ref, *refs, group, n_steps, n_new, nbp, n_slc):
    pages = refs[:group]
    o_ref, m_ref, l_ref, acc_ref = refs[group:]
    j = pl.program_id(1)
    n_keys = group * PAGE_SIZE
    rows = NSA_GROUPS * NSA_HPG * n_new

    @pl.when(j == 0)
    def _():
        m_ref[...] = jnp.full_like(m_ref, M_INIT)
        l_ref[...] = jnp.zeros_like(l_ref)
        acc_ref[...] = jnp.zeros_like(acc_ref)

    def group_rows(per_query):
        return jnp.concatenate([per_query[g] for g in range(NSA_GROUPS) for _ in range(NSA_HPG)], axis=0)

    def update(s, values):
        m_prev = m_ref[...]
        m_new = jnp.maximum(m_prev, jnp.max(s, axis=1, keepdims=True))
        p = jnp.exp(s - m_new)
        alpha = jnp.exp(m_prev - m_new)
        l_ref[...] = alpha * l_ref[...] + jnp.sum(p, axis=1, keepdims=True)
        acc_ref[...] = alpha * acc_ref[...] + values(p.astype(BF16))
        m_ref[...] = m_new

    def tiles(c):
        return jnp.concatenate([pages[i][0, c].reshape(NSA_KV_WIDTH, PAGE_SIZE).astype(BF16) for i in range(group)],
                               axis=1)

    q = q_ref[0]
    blk_row = lax.broadcasted_iota(jnp.int32, (nbp, n_keys), 0)
    blk_key = (j * n_keys + lax.broadcasted_iota(jnp.int32, (nbp, n_keys), 1)) // SLC_LEN
    expand = (blk_row == blk_key).astype(BF16)
    bias = [jnp.dot(sel_ref[0, :, g * nbp:(g + 1) * nbp], expand, preferred_element_type=F32)
            for g in range(NSA_GROUPS)]
    update(jnp.dot(q, tiles(0), preferred_element_type=F32) + group_rows(bias),
           lambda pb: lax.dot_general(pb, tiles(1), NT_DIMS, preferred_element_type=F32))

    @pl.when(j == n_steps - 1)
    def _():
        t_q = lax.broadcasted_iota(jnp.int32, (rows, n_new), 0) % n_new
        t_k = lax.broadcasted_iota(jnp.int32, (rows, n_new), 1)
        last_blk = [sel_ref[0, :, g * nbp + n_slc - 1:g * nbp + n_slc].astype(F32) for g in range(NSA_GROUPS)]
        s = lax.dot_general(q, kn_ref[0], NT_DIMS, preferred_element_type=F32) + group_rows(last_blk)
        update(jnp.where(t_k <= t_q, s, NEG_INF), lambda pb: jnp.dot(pb, vn_ref[0], preferred_element_type=F32))
        o_ref[0] = acc_ref[...] / jnp.maximum(l_ref[...], 1e-30)


def _slc_sample(page_table, cache_kv_t, q_rows, sel, ks_new, vs_new, *, nbp, n_slc):
    s, n_pages = page_table.shape
    n_new = ks_new.shape[1]
    rows = q_rows.shape[1]
    group = _pick(n_pages, (8, 4, 2, 1))
    n_steps = n_pages // group
    seq = lambda shape: pl.BlockSpec(shape, lambda si, j, pt: (si, 0, 0))
    grid_spec = pltpu.PrefetchScalarGridSpec(
        num_scalar_prefetch=1, grid=(s, n_steps),
        in_specs=[seq((1, rows, LANES)), seq((1, n_new, NSA_GROUPS * nbp)), seq((1, n_new, LANES)),
                  seq((1, n_new, LANES))]
        + _page_specs((1, 2, NSA_GROUPS, HEAD_DIM, PAGE_SIZE), group, n_pages, second=1),
        out_specs=seq((1, rows, LANES)),
        scratch_shapes=[pltpu.VMEM((rows, 1), F32), pltpu.VMEM((rows, 1), F32), pltpu.VMEM((rows, LANES), F32)])
    return pl.pallas_call(
        functools.partial(_slc_sample_kernel, group=group, n_steps=n_steps, n_new=n_new, nbp=nbp, n_slc=n_slc),
        grid_spec=grid_spec,
        out_shape=jax.ShapeDtypeStruct((s, rows, LANES), F32),
        compiler_params=_params(("parallel", "arbitrary"), 40),
        name="slc_sample",
    )(page_table.reshape(-1), q_rows, sel, ks_new, vs_new, *([cache_kv_t] * group))


def _win_rows_kernel(st_ref, kwnew_ref, vwnew_ref, kw_o, vw_o, *, keep, n_new):
    for c, new_ref, o_ref in ((0, kwnew_ref, kw_o), (1, vwnew_ref, vw_o)):
        o_ref[0] = jnp.zeros(o_ref.shape[1:], BF16)
        o_ref[0, 0:keep, :] = _to_rows(st_ref[0, c]).astype(BF16)
        o_ref[0, keep:keep + n_new, :] = new_ref[0]


def _win_rows(state_t, kw_new, vw_new, t_all):
    s, keep = state_t.shape[0], state_t.shape[-1]
    n_new = kw_new.shape[1]
    seq = lambda si: (si, 0, 0)
    return pl.pallas_call(
        functools.partial(_win_rows_kernel, keep=keep, n_new=n_new),
        grid=(s,),
        in_specs=[pl.BlockSpec((1, 2, NSA_GROUPS, HEAD_DIM, keep), lambda si: (si, 0, 0, 0, 0)),
                  pl.BlockSpec((1, n_new, LANES), seq), pl.BlockSpec((1, n_new, LANES), seq)],
        out_specs=[pl.BlockSpec((1, t_all, LANES), seq), pl.BlockSpec((1, t_all, LANES), seq)],
        out_shape=[jax.ShapeDtypeStruct((s, t_all, LANES), BF16), jax.ShapeDtypeStruct((s, t_all, LANES), BF16)],
        compiler_params=_params(("parallel",), 32),
        name="win_rows",
    )(state_t, kw_new, vw_new)


def _pair_views(c):
    b, h, t = c.shape
    c4 = c.reshape(b, h // 2, 2, t)
    return c4.transpose(0, 1, 3, 2), c4


def _rows_minor(a):
    return jnp.moveaxis(a, 1, -1)


def kernel(x_prompt, x_sample, cache_fox_kv, cache_fox_logf, cache_nsa_kv, state_win_kv, page_table, ln1_g, ln1_b, ffn1_w_up, ffn1_w_down, w_in, b_fgate, cmp_pos_k, cmp_wk1, cmp_wk2, cmp_pos_v, cmp_wv1, cmp_wv2, w_up_fox, w_up_nsa, w_out, ln2_g, ln2_b, ffn2_w_up, ffn2_w_down, ln3_g, ln3_b):
    depth = ln1_g.shape[0]
    assert depth == 1, "single-layer step"
    alpha = (2.0 * depth) ** 0.25
    bsz, seq, d = x_prompt.shape
    sb, n_new, _ = x_sample.shape
    n_pages = page_table.shape[1]
    past = n_pages * PAGE_SIZE
    keep = state_win_kv.shape[2]
    assert seq % 256 == 0 and past % SLC_LEN == 0 and n_new < CMP_STRIDE and keep == WINDOW
    layer = 0

    ws, bf = _split_w_in(w_in[layer], b_fgate[layer])
    cmp_k = _compress_weights(cmp_pos_k[layer], cmp_wk1[layer], cmp_wk2[layer])
    cmp_v = _compress_weights(cmp_pos_v[layer], cmp_wv1[layer], cmp_wv2[layer])

    n = bsz * seq
    xp = _ffn_ln(x_prompt.reshape(n, d), ffn1_w_up[layer], ffn1_w_down[layer], ln1_g[layer], ln1_b[layer], alpha,
                 "ffn1_prompt")
    cos_p, sin_p = _rope_tables(jnp.arange(seq, dtype=jnp.int32))
    mp = _in_proj(xp, ws, bf, cos_p, sin_p, "in_proj_prompt")
    r3 = lambda a: a.reshape(bsz, seq, a.shape[-1])

    c_p = _cumsum_time(r3(mp["lf"]).transpose(0, 2, 1), "cumsum_prompt")
    cc_p, cr_p = _pair_views(c_p)
    tq = _pick(seq, (512, 256))
    tk = _pick(seq, (512, 256))
    fox_o = _attention("fox", r3(mp["fq"]), r3(mp["fk"]), r3(mp["fv"]), tq=tq, tk=tk, q_off=0, k_off=0,
                       extra=(cc_p, cr_p), name="fox_prompt")

    nkv3 = r3(mp["nkv"])
    kc_p = _compress(nkv3, 0, cmp_k, "compress_k_prompt")
    vc_p = _compress(nkv3, 1, cmp_v, "compress_v_prompt")
    n_cmp_p = seq // CMP_STRIDE - 1
    n_slc_p = -(-seq // SLC_LEN)
    nbp_p = _round_up(n_slc_p, LANES)
    nq3 = r3(mp["nq"])
    oc_p, sel_p = _cmp_select(nq3, kc_p, vc_p, tq=256, sps=1, n_cmp=n_cmp_p, n_slc=n_slc_p, nbp=nbp_p, q_off=0,
                              name="cmp_select_prompt")
    os_p = _attention("slc", nq3, r3(mp["ks"]), r3(mp["vs"]), tq=tq, tk=tk, q_off=0, k_off=0, extra=(sel_p,),
                      nbp=nbp_p, name="slc_prompt")
    ow_p = _attention("win", nq3, r3(mp["kw"]), r3(mp["vw"]), tq=256, tk=256, q_off=0, k_off=0, name="win_prompt")
    flat = lambda a: a.reshape(n, a.shape[-1])
    xp2 = _merge_ln(xp, flat(fox_o), flat(oc_p), flat(os_p), flat(ow_p), mp["ng"], mp["gf"], mp["gn"],
                    w_up_fox[layer], w_up_nsa[layer], w_out[layer], ln2_g[layer], ln2_b[layer], alpha, "merge_prompt")
    yp = _ffn_ln(xp2, ffn2_w_up[layer], ffn2_w_down[layer], ln3_g[layer], ln3_b[layer], alpha, "ffn2_prompt")

    ns_rows = sb * n_new
    xs = _ffn_ln(x_sample.reshape(ns_rows, d), ffn1_w_up[layer], ffn1_w_down[layer], ln1_g[layer], ln1_b[layer],
                 alpha, "ffn1_sample")
    cos_s, sin_s = _rope_tables(past + jnp.arange(n_new, dtype=jnp.int32))
    tile_rows = _pick(ns_rows, (256, 128, 8))
    reps = tile_rows // n_new
    ms = _in_proj(xs, ws, bf, jnp.tile(cos_s, (reps, 1)), jnp.tile(sin_s, (reps, 1)), "in_proj_sample")
    s3 = lambda a: a.reshape(sb, n_new, a.shape[-1])
    by_head = lambda a: s3(a).reshape(sb, n_new, FOX_HEADS, HEAD_DIM).transpose(0, 2, 1, 3)

    lf_past = _lf_pages(page_table, _rows_minor(cache_fox_logf[layer]))
    lf_new = s3(ms["lf"]).transpose(0, 2, 1)
    t_cs = _round_up(past + n_new, 8 * LANES)
    c_s = _cumsum_time(jnp.pad(jnp.concatenate([lf_past, lf_new], axis=2), ((0, 0), (0, 0), (0, t_cs - past - n_new))),
                       "cumsum_sample")
    c_new = c_s[:, :, past:past + n_new]
    fox_o_s = _fox_sample(page_table, _rows_minor(cache_fox_kv[layer]), by_head(ms["fq"]), by_head(ms["fk"]),
                          by_head(ms["fv"]), c_new[..., None], c_s, c_new)
    fox_o_s = fox_o_s.transpose(0, 2, 1, 3).reshape(ns_rows, FOX_WIDTH)

    nsa_t = _rows_minor(cache_nsa_kv[layer])
    kc_s, vc_s = _compress_pages(page_table, nsa_t, cmp_k, cmp_v)
    n_cmp_s = (past + n_new) // CMP_STRIDE - 1
    n_slc_s = -(-(past + n_new) // SLC_LEN)
    nbp_s = _round_up(n_slc_s, LANES)
    nq_s = s3(ms["nq"])
    sps = _pick(sb, tuple(c for c in (16, 8, 4, 2, 1) if c * n_new <= LANES))
    oc_s, sel_s = _cmp_select(nq_s, kc_s, vc_s, tq=n_new, sps=sps, n_cmp=n_cmp_s, n_slc=n_slc_s, nbp=nbp_s,
                              q_off=past, name="cmp_select_sample")
    q_ghtd = nq_s.reshape(sb, n_new, NSA_HPG, NSA_GROUPS, HEAD_DIM).transpose(0, 3, 2, 1, 4)
    zero = jnp.zeros_like(q_ghtd[:, 0])
    q_rows = jnp.concatenate([jnp.concatenate([q_ghtd[:, 0], zero], axis=-1),
                              jnp.concatenate([zero, q_ghtd[:, 1]], axis=-1)], axis=1).reshape(sb, -1, LANES)
    o_rows = _slc_sample(page_table, nsa_t, q_rows, sel_s, s3(ms["ks"]), s3(ms["vs"]), nbp=nbp_s, n_slc=n_slc_s)
    o_rows = o_rows.reshape(sb, NSA_GROUPS, NSA_HPG, n_new, LANES)
    os_s = jnp.stack([o_rows[:, 0, :, :, :HEAD_DIM], o_rows[:, 1, :, :, HEAD_DIM:]], axis=3)
    os_s = os_s.transpose(0, 2, 1, 3, 4).reshape(sb, n_new, NSA_WIDTH)
    t_win = _round_up(keep + n_new, LANES)
    kw_all, vw_all = _win_rows(_rows_minor(state_win_kv[layer]), s3(ms["kw"]), s3(ms["vw"]), t_win)
    ow_s = _attention("win", nq_s, kw_all, vw_all, tq=n_new, tk=t_win, q_off=past, k_off=past - keep,
                      name="win_sample")
    flat_s = lambda a: a.reshape(ns_rows, a.shape[-1])
    xs2 = _merge_ln(xs, fox_o_s, flat_s(oc_s), flat_s(os_s), flat_s(ow_s), ms["ng"], ms["gf"], ms["gn"],
                    w_up_fox[layer], w_up_nsa[layer], w_out[layer], ln2_g[layer], ln2_b[layer], alpha, "merge_sample")
    ys = _ffn_ln(xs2, ffn2_w_up[layer], ffn2_w_down[layer], ln3_g[layer], ln3_b[layer], alpha, "ffn2_sample")

    fox_kv_p = mp["fkv"].reshape(1, bsz, seq, 2, FOX_HEADS, HEAD_DIM)
    fox_kv_s = ms["fkv"].reshape(1, sb, n_new, 2, FOX_HEADS, HEAD_DIM)
    logf_p = mp["lf"].reshape(1, bsz, seq, FOX_HEADS)
    logf_s = ms["lf"].reshape(1, sb, n_new, FOX_HEADS)
    nsa_kv_p = mp["nkv"].reshape(1, bsz, seq, 4, NSA_GROUPS, HEAD_DIM)
    nsa_kv_s = ms["nkv"].reshape(1, sb, n_new, 4, NSA_GROUPS, HEAD_DIM)
    win_rows_p = r3(mp["win"]).reshape(bsz, seq, 2, NSA_GROUPS, HEAD_DIM)
    if seq >= keep:
        win_p = win_rows_p[:, seq - keep:]
    else:
        win_p = jnp.pad(win_rows_p, ((0, 0), (keep - seq, 0), (0, 0), (0, 0), (0, 0)))
    new_win = ms["win"].reshape(sb, n_new, 2, NSA_GROUPS, HEAD_DIM).astype(state_win_kv.dtype)
    win_s = jnp.concatenate([state_win_kv[layer], new_win], axis=1)[:, -keep:]
    return (yp.reshape(bsz, seq, d), ys.reshape(sb, n_new, d), fox_kv_p, fox_kv_s, logf_p, logf_s,
            nsa_kv_p, nsa_kv_s, win_p[None], win_s[None])
```

```python
import functools

import numpy as np
import jax
import jax.numpy as jnp
from jax import lax
from jax.experimental import pallas as pl
from jax.experimental.pallas import tpu as pltpu

F32 = jnp.float32
BF16 = jnp.bfloat16

HEAD_DIM = 64
FOX_HEADS = 8
NSA_HEADS = 8
NSA_GROUPS = 2
NSA_HPG = NSA_HEADS // NSA_GROUPS
FOX_WIDTH = FOX_HEADS * HEAD_DIM
NSA_WIDTH = NSA_HEADS * HEAD_DIM
NSA_KV_WIDTH = NSA_GROUPS * HEAD_DIM
N_NSA_BRANCHES = 3
CMP_LEN = 32
CMP_STRIDE = 16
CMP_HIDDEN = 128
SLC_LEN = 64
N_SELECT = 16
WINDOW = 512
PAGE_SIZE = 128
ROPE_THETA = 10000.0
LN_EPS = 1e-5
FORCE_BONUS = 1e4
NEG_INF = -1e30
M_INIT = 0.1 * NEG_INF
LOG2E = 1.4426950408889634
QSCALE = HEAD_DIM ** -0.5 * LOG2E
SUM_LANE = HEAD_DIM

LANES = 128
N_PAIRS = 4
MIB = 1024 * 1024
HIGHEST = lax.Precision.HIGHEST
NT_DIMS = (((1,), (1,)), ((), ()))


def _params(semantics, vmem_mib):
    return pltpu.CompilerParams(dimension_semantics=semantics, vmem_limit_bytes=vmem_mib * MIB)


def _pick(n, candidates):
    for c in candidates:
        if n % c == 0:
            return c
    return n


def _round_up(n, m):
    return (n + m - 1) // m * m


def _layer_norm(y, g, b):
    mu = jnp.mean(y, axis=-1, keepdims=True)
    d = y - mu
    var = jnp.mean(d * d, axis=-1, keepdims=True)
    return d * lax.rsqrt(var + LN_EPS) * g + b


def _const_spec(shape):
    nd = len(shape)
    return pl.BlockSpec(shape, lambda *_: (0,) * nd, pipeline_mode=pl.Buffered(1))


def _ffn_ln_kernel(x_ref, wa_ref, wb_ref, wd_ref, g_ref, b_ref, o_ref, acc_ref, *, alpha, n_chunks):
    x = x_ref[...]
    xb = x.astype(BF16)
    acc_ref[...] = jnp.zeros_like(acc_ref)

    def body(c, carry):
        a = jnp.dot(xb, wa_ref[c], preferred_element_type=F32)
        b = jnp.dot(xb, wb_ref[c], preferred_element_type=F32)
        h = (a * jax.nn.sigmoid(a) * b).astype(BF16)
        acc_ref[...] += jnp.dot(h, wd_ref[c], preferred_element_type=F32)
        return carry

    lax.fori_loop(0, n_chunks, body, 0)
    o_ref[...] = _layer_norm(alpha * x + 0.5 * acc_ref[...], g_ref[...], b_ref[...])


def _ffn_ln(x, w_up, w_down, g, b, alpha, name):
    n, d = x.shape
    f = w_down.shape[0]
    fc = _pick(f, (256, 128))
    nc = f // fc
    wa = w_up[:, :f].astype(BF16).reshape(d, nc, fc).transpose(1, 0, 2)
    wb = w_up[:, f:].astype(BF16).reshape(d, nc, fc).transpose(1, 0, 2)
    wd = w_down.astype(BF16).reshape(nc, fc, d)
    tm = _pick(n, (1024, 512, 256, 128, 8))
    return pl.pallas_call(
        functools.partial(_ffn_ln_kernel, alpha=alpha, n_chunks=nc),
        grid=(n // tm,),
        in_specs=[pl.BlockSpec((tm, d), lambda i: (i, 0)),
                  _const_spec((nc, d, fc)), _const_spec((nc, d, fc)), _const_spec((nc, fc, d)),
                  _const_spec((1, d)), _const_spec((1, d))],
        out_specs=pl.BlockSpec((tm, d), lambda i: (i, 0)),
        out_shape=jax.ShapeDtypeStruct((n, d), F32),
        scratch_shapes=[pltpu.VMEM((tm, d), F32)],
        compiler_params=_params(("parallel",), 56),
        name=name,
    )(x, wa, wb, wd, g.reshape(1, d), b.reshape(1, d))


def _rope_tables(pos):
    half = HEAD_DIM // 2
    inv_freq = ROPE_THETA ** (-jnp.arange(half, dtype=F32) / half)
    ang = pos.astype(F32)[:, None] * inv_freq[None, :]
    cos, sin = jnp.cos(ang), jnp.sin(ang)
    cos64 = jnp.concatenate([cos, cos], axis=-1)
    sin64 = jnp.concatenate([-sin, sin], axis=-1)
    return jnp.tile(cos64, (1, LANES // HEAD_DIM)), jnp.tile(sin64, (1, LANES // HEAD_DIM))


def _in_proj_kernel(x_ref, wfq, wfkv, wsm, wnq, wnkv, wwin, wgf, wgn, bf_ref, cos_ref, sin_ref,
                    fq_o, fkv_o, fk_o, fv_o, lf_o, nq_o, nkv_o, ks_o, vs_o, win_o, kw_o, vw_o,
                    ng_o, gf_o, gn_o):
    xb = x_ref[...].astype(BF16)
    cos = cos_ref[...]
    sin = sin_ref[...]
    lane = lax.broadcasted_iota(jnp.int32, (1, LANES), 1)
    low_half = (lane % HEAD_DIM) < (HEAD_DIM // 2)

    def rope(v):
        partner = jnp.where(low_half, pltpu.roll(v, LANES - HEAD_DIM // 2, 1), pltpu.roll(v, HEAD_DIM // 2, 1))
        return v * cos + partner * sin

    def proj(w_ref):
        return jnp.dot(xb, w_ref[...], preferred_element_type=F32)

    fq_o[...] = (proj(wfq) * QSCALE).astype(BF16)
    fkv = proj(wfkv)
    fkv_o[...] = fkv
    fk_o[...] = fkv[:, :FOX_WIDTH].astype(BF16)
    fv_o[...] = fkv[:, FOX_WIDTH:].astype(BF16)

    sm = proj(wsm)
    z = sm[:, :LANES] + bf_ref[...]
    logf = jnp.minimum(z, 0.0) - jnp.log1p(jnp.exp(-jnp.abs(z)))
    lf_o[...] = logf[:, :FOX_HEADS]
    ng_o[...] = sm[:, LANES:]

    nq = proj(wnq)
    for r in range(N_PAIRS):
        nq_o[:, r * LANES:(r + 1) * LANES] = (rope(nq[:, r * LANES:(r + 1) * LANES]) * QSCALE).astype(BF16)

    nkv = proj(wnkv)
    k_cmp = rope(nkv[:, 0:LANES])
    k_slc = rope(nkv[:, 2 * LANES:3 * LANES])
    v_slc = nkv[:, 3 * LANES:4 * LANES]
    nkv_o[:, 0:LANES] = k_cmp
    nkv_o[:, LANES:2 * LANES] = nkv[:, LANES:2 * LANES]
    nkv_o[:, 2 * LANES:3 * LANES] = k_slc
    nkv_o[:, 3 * LANES:4 * LANES] = v_slc
    ks_o[...] = k_slc.astype(BF16)
    vs_o[...] = v_slc.astype(BF16)

    win = proj(wwin)
    k_win = rope(win[:, :LANES])
    v_win = win[:, LANES:]
    win_o[:, :LANES] = k_win
    win_o[:, LANES:] = v_win
    kw_o[...] = k_win.astype(BF16)
    vw_o[...] = v_win.astype(BF16)

    gf_o[...] = proj(wgf)
    gn_o[...] = proj(wgn)


def _nsa_perm():
    perm = np.zeros(NSA_WIDTH, np.int32)
    for r in range(NSA_HPG):
        for g in range(NSA_GROUPS):
            for d in range(HEAD_DIM):
                perm[r * LANES + g * HEAD_DIM + d] = (g * NSA_HPG + r) * HEAD_DIM + d
    return perm


def _split_w_in(w_in, b_fgate):
    d = w_in.shape[0]
    sizes = (FOX_WIDTH, FOX_WIDTH, FOX_WIDTH, FOX_HEADS, NSA_WIDTH) + (NSA_KV_WIDTH,) * 6 + (
        N_NSA_BRANCHES * NSA_HEADS, d, d)
    offs = np.concatenate([[0], np.cumsum(sizes)])
    col = lambda i, j=None: w_in[:, offs[i]:offs[(i if j is None else j) + 1]]
    pad = lambda w: jnp.pad(w, ((0, 0), (0, LANES - w.shape[1])))
    ws = dict(
        wfq=col(0), wfkv=col(1, 2),
        wsm=jnp.concatenate([pad(col(3)), pad(col(11))], axis=1),
        wnq=col(4)[:, _nsa_perm()], wnkv=col(5, 8), wwin=col(9, 10), wgf=col(12), wgn=col(13))
    ws = {k: v.astype(BF16) for k, v in ws.items()}
    bf = jnp.pad(b_fgate.astype(F32), (0, LANES - FOX_HEADS)).reshape(1, LANES)
    return ws, bf


def _in_proj(x, ws, bf, cos_tab, sin_tab, name):
    n, d = x.shape
    n_tab = cos_tab.shape[0]
    tm = _pick(n_tab, (256, 128, 8))
    tab_tiles = n_tab // tm
    row = lambda w: pl.BlockSpec((tm, w), lambda i: (i, 0))
    tab = pl.BlockSpec((tm, LANES), lambda i: (i % tab_tiles, 0))
    names = ("wfq", "wfkv", "wsm", "wnq", "wnkv", "wwin", "wgf", "wgn")
    outs = [("fq", FOX_WIDTH, BF16), ("fkv", 2 * FOX_WIDTH, F32), ("fk", FOX_WIDTH, BF16), ("fv", FOX_WIDTH, BF16),
            ("lf", FOX_HEADS, F32), ("nq", NSA_WIDTH, BF16), ("nkv", 4 * NSA_KV_WIDTH, F32),
            ("ks", NSA_KV_WIDTH, BF16), ("vs", NSA_KV_WIDTH, BF16), ("win", 2 * NSA_KV_WIDTH, F32),
            ("kw", NSA_KV_WIDTH, BF16), ("vw", NSA_KV_WIDTH, BF16), ("ng", LANES, F32), ("gf", d, F32), ("gn", d, F32)]
    res = pl.pallas_call(
        _in_proj_kernel,
        grid=(n // tm,),
        in_specs=[row(d)] + [_const_spec(ws[k].shape) for k in names] + [_const_spec((1, LANES)), tab, tab],
        out_specs=[row(w) for _, w, _ in outs],
        out_shape=[jax.ShapeDtypeStruct((n, w), dt) for _, w, dt in outs],
        compiler_params=_params(("parallel",), 56),
        name=name,
    )(x, *[ws[k] for k in names], bf, cos_tab, sin_tab)
    return {k: v for (k, _, _), v in zip(outs, res)}


def _cumsum_kernel(x_ref, tri_ref, low_ref, o_ref):
    x = x_ref[0]
    within = lax.dot_general(x, tri_ref[...], (((1,), (0,)), ((), ())), precision=HIGHEST, preferred_element_type=F32)
    tot = jnp.broadcast_to(within[:, LANES - 1:LANES], within.shape)
    before = lax.dot_general(low_ref[...], tot, (((1,), (0,)), ((), ())), precision=HIGHEST, preferred_element_type=F32)
    o_ref[0] = within + before


def _cumsum_time(logf_t, name):
    b, h, t = logf_t.shape
    r = t // LANES
    rows = h * r
    idx = np.arange(LANES)
    tri = jnp.asarray((idx[:, None] <= idx[None, :]).astype(np.float32))
    ridx = np.arange(rows)
    low = jnp.asarray(((ridx[None, :] < ridx[:, None]) & (ridx[None, :] // r == ridx[:, None] // r)).astype(np.float32))
    out = pl.pallas_call(
        _cumsum_kernel,
        grid=(b,),
        in_specs=[pl.BlockSpec((1, rows, LANES), lambda i: (i, 0, 0)), _const_spec((LANES, LANES)),
                  _const_spec((rows, rows))],
        out_specs=pl.BlockSpec((1, rows, LANES), lambda i: (i, 0, 0)),
        out_shape=jax.ShapeDtypeStruct((b, rows, LANES), F32),
        compiler_params=_params(("parallel",), 32),
        name=name,
    )(logf_t.reshape(b, rows, LANES), tri, low)
    return out.reshape(b, h, t)


FLAG_FIRST, FLAG_LAST, FLAG_EDGE = 1, 2, 4


def _tile_schedule(mode, nq, nkt, tq, tk, q_off, k_off):
    qi_tab, kt_tab, flag_tab = [], [], []
    for qi in range(nq):
        q_lo, q_hi = q_off + qi * tq, q_off + (qi + 1) * tq - 1
        tiles = []
        for kt in range(nkt):
            k_lo, k_hi = k_off + kt * tk, k_off + (kt + 1) * tk - 1
            if mode == "win":
                visible = (q_hi - k_lo >= 0) and (q_lo - k_hi < WINDOW)
                interior = (q_lo - k_hi >= 0) and (q_hi - k_lo < WINDOW)
            else:
                visible = k_lo <= q_hi
                interior = k_hi <= q_lo
            if visible:
                tiles.append((kt, 0 if interior else FLAG_EDGE))
        assert tiles, "every query tile sees at least one key tile"
        for n, (kt, flag) in enumerate(tiles):
            qi_tab.append(qi)
            kt_tab.append(kt)
            flag_tab.append(flag | (FLAG_FIRST if n == 0 else 0) | (FLAG_LAST if n == len(tiles) - 1 else 0))
    as_i32 = lambda v: jnp.asarray(np.asarray(v, np.int32))
    return as_i32(qi_tab), as_i32(kt_tab), as_i32(flag_tab)


def _attn_kernel(qi_tab, kt_tab, flag_tab, q_ref, k_ref, v_ref, o_ref, m_ref, acc_ref, *,
                 mode, tq, tk, nh, n_kv, q_off, k_off):
    step = pl.program_id(1)
    qi = qi_tab[step]
    kt = kt_tab[step]
    flags = flag_tab[step]
    lane = lax.broadcasted_iota(jnp.int32, (1, LANES), 1)
    n_chunks = tk // LANES

    @pl.when((flags & FLAG_FIRST) != 0)
    def _():
        m_ref[...] = jnp.full_like(m_ref, M_INIT)
        acc_ref[...] = jnp.zeros_like(acc_ref)

    def tile(edge):
        if edge:
            q_pos = q_off + qi * tq + lax.broadcasted_iota(jnp.int32, (tq, 1), 0)
        scores = lambda i: lax.dot_general(q_ref[0, i], k_ref[0, i % n_kv], NT_DIMS, preferred_element_type=F32)

        def values(i, alpha, pb):
            acc_ref[i] = alpha * acc_ref[i] + jnp.dot(pb, v_ref[0, i % n_kv], preferred_element_type=F32)

        s_next = scores(0)
        pending = None
        for i in range(nh):
            s = s_next
            if i + 1 < nh:
                s_next = scores(i + 1)
            if pending is not None:
                values(*pending)
            chunks = []
            for c in range(n_chunks):
                t = s[:, c * LANES:(c + 1) * LANES]
                if edge:
                    k_pos = k_off + kt * tk + c * LANES + lane
                    if mode == "win":
                        rel = q_pos - k_pos
                        t = jnp.where((rel >= 0) & (rel < WINDOW), t, NEG_INF)
                    else:
                        t = jnp.where(k_pos <= q_pos, t, NEG_INF)
                chunks.append(t)
            mx = chunks[0]
            for t in chunks[1:]:
                mx = jnp.maximum(mx, t)
            m_prev = m_ref[i]
            m_new = jnp.maximum(m_prev, jnp.max(mx, axis=1, keepdims=True))
            alpha = jnp.exp2(m_prev - m_new)
            ps = [jnp.exp2(t - m_new).astype(BF16) for t in chunks]
            pb = jnp.concatenate(ps, axis=1) if n_chunks > 1 else ps[0]
            m_ref[i] = m_new
            pending = (i, alpha, pb)
        values(*pending)

    pl.when((flags & FLAG_EDGE) != 0)(lambda: tile(True))
    pl.when((flags & FLAG_EDGE) == 0)(lambda: tile(False))

    @pl.when((flags & FLAG_LAST) != 0)
    def _():
        for pr in range(nh // 2):
            outs = []
            for side in range(2):
                acc = acc_ref[2 * pr + side]
                outs.append(acc / jnp.maximum(acc[:, SUM_LANE:SUM_LANE + 1], 1e-30))
            o_ref[0, :, pr * LANES:(pr + 1) * LANES] = jnp.where(lane < HEAD_DIM, outs[0],
                                                                  pltpu.roll(outs[1], HEAD_DIM, 1))


def _attention(mode, q, k, v, *, tq, tk, q_off, k_off, name):
    b, nh, t_q, _ = q.shape
    n_kv, t_k = k.shape[1], k.shape[2]
    qi_tab, kt_tab, flag_tab = _tile_schedule(mode, t_q // tq, t_k // tk, tq, tk, q_off, k_off)
    n_steps = qi_tab.shape[0]
    q_spec = pl.BlockSpec((1, nh, tq, LANES), lambda bi, s, qt, kt, fl: (bi, 0, qt[s], 0))
    kv_spec = pl.BlockSpec((1, n_kv, tk, LANES), lambda bi, s, qt, kt, fl: (bi, 0, kt[s], 0))
    grid_spec = pltpu.PrefetchScalarGridSpec(
        num_scalar_prefetch=3, grid=(b, n_steps), in_specs=[q_spec, kv_spec, kv_spec],
        out_specs=pl.BlockSpec((1, tq, nh // 2 * LANES), lambda bi, s, qt, kt, fl: (bi, qt[s], 0)),
        scratch_shapes=[pltpu.VMEM((nh, tq, LANES), F32), pltpu.VMEM((nh, tq, LANES), F32)])
    return pl.pallas_call(
        functools.partial(_attn_kernel, mode=mode, tq=tq, tk=tk, nh=nh, n_kv=n_kv, q_off=q_off, k_off=k_off),
        grid_spec=grid_spec,
        out_shape=jax.ShapeDtypeStruct((b, t_q, nh // 2 * LANES), F32),
        compiler_params=_params(("parallel", "arbitrary"), 48),
        name=name,
    )(qi_tab, kt_tab, flag_tab, q, k, v)


def _limbs(x):
    hi = x.astype(BF16)
    r1 = x - hi.astype(F32)
    mid = r1.astype(BF16)
    lo = (r1 - mid.astype(F32)).astype(BF16)
    return [hi, mid, lo]


def _head_major(a, n_heads):
    b, t, _ = a.shape
    return a.reshape(b, t, n_heads, HEAD_DIM).transpose(0, 2, 1, 3)


def _augment(head_rows, extras):
    b, h, t, _ = head_rows.shape
    cols = [jnp.broadcast_to(e, (b, h, t))[..., None].astype(BF16) for e in extras]
    pad = jnp.zeros((b, h, t, LANES - HEAD_DIM - len(cols)), BF16)
    return jnp.concatenate([head_rows] + cols + [pad], axis=-1)


def _compress_rows(row_of_half, pa_ref, pb_ref, wa_ref, wb_ref, w2_ref, ns):
    u = jnp.concatenate([row_of_half(r) for r in range(CMP_STRIDE)], axis=1)
    first = jnp.dot((u + pa_ref[...]).astype(BF16), wa_ref[...], preferred_element_type=F32)
    second = jnp.dot((u + pb_ref[...]).astype(BF16), wb_ref[...], preferred_element_type=F32)
    pre = first + pltpu.roll(second, ns - 1, 0)
    h = (pre * jax.nn.sigmoid(pre)).astype(BF16)
    return jnp.dot(h, w2_ref[...], preferred_element_type=F32).astype(BF16)


def _compress_kernel(rows_ref, pa_ref, pb_ref, wa_ref, wb_ref, w2_ref, o_ref, *, ns):
    o_ref[0] = _compress_rows(lambda r: rows_ref[0, pl.ds(r, ns, stride=CMP_STRIDE), :],
                              pa_ref, pb_ref, wa_ref, wb_ref, w2_ref, ns)


def _compress_weights(pos, w1, w2):
    ratio = CMP_LEN // CMP_STRIDE
    assert ratio == 2
    w1r = w1.reshape(CMP_LEN, HEAD_DIM, CMP_HIDDEN)
    zeros = jnp.zeros((CMP_STRIDE, HEAD_DIM, CMP_HIDDEN), w1.dtype)

    def half(rows):
        g0 = jnp.concatenate([jnp.stack([rows, zeros], axis=1).reshape(-1, CMP_HIDDEN),
                              jnp.stack([zeros, rows], axis=1).reshape(-1, CMP_HIDDEN)], axis=1)
        return g0.astype(BF16)

    def pos_tab(p):
        return jnp.stack([p, p], axis=1).reshape(1, -1).astype(F32)

    z2 = jnp.zeros_like(w2)
    w2d = jnp.concatenate([jnp.concatenate([w2, z2], axis=1), jnp.concatenate([z2, w2], axis=1)], axis=0)
    return (pos_tab(pos[:CMP_STRIDE]), pos_tab(pos[CMP_STRIDE:]), half(w1r[:CMP_STRIDE]), half(w1r[CMP_STRIDE:]),
            w2d.astype(BF16))


def _compress(rows, col_block, weights, name):
    b, t, _ = rows.shape
    ns = t // CMP_STRIDE
    width = CMP_STRIDE * LANES
    pa, pb, wa, wb, w2d = weights
    return pl.pallas_call(
        functools.partial(_compress_kernel, ns=ns),
        grid=(b,),
        in_specs=[pl.BlockSpec((1, t, LANES), lambda i: (i, 0, col_block)), _const_spec((1, width)),
                  _const_spec((1, width)), _const_spec(wa.shape), _const_spec(wb.shape), _const_spec(w2d.shape)],
        out_specs=pl.BlockSpec((1, ns, LANES), lambda i: (i, 0, 0)),
        out_shape=jax.ShapeDtypeStruct((b, ns, LANES), BF16),
        compiler_params=_params(("parallel",), 48),
        name=name,
    )(rows, pa, pb, wa, wb, w2d)


def _cmp_select_kernel(q_ref, kc_ref, vc_ref, cov_ref, oc_ref, sel_ref, imp_ref, vt_ref, *,
                       sps, tq, rows_p, nc, n_cmp, n_slc, nbp, q_off, n_sel):
    qi = pl.program_id(1)
    lane = lax.broadcasted_iota(jnp.int32, (1, LANES), 1)
    left = lane < HEAD_DIM
    q_pos = q_off + qi * tq + lax.broadcasted_iota(jnp.int32, (tq, 1), 0)
    n_idx = lax.broadcasted_iota(jnp.int32, (1, nc), 1)
    cmask = ((n_idx * CMP_STRIDE + CMP_LEN - 1) <= q_pos) & (n_idx < n_cmp)
    cov = cov_ref[...]
    if rows_p > sps * tq:
        imp_ref[...] = jnp.zeros_like(imp_ref)

    def attend(sq, carry):
        kc = kc_ref[sq]
        vc = vc_ref[sq]
        imp = [jnp.zeros((tq, nbp), F32) for _ in range(NSA_GROUPS)]
        for r in range(N_PAIRS):
            qp = q_ref[sq, :, r * LANES:(r + 1) * LANES]
            halves = []
            for g in range(NSA_GROUPS):
                qh = jnp.where(left if g == 0 else jnp.logical_not(left), qp, jnp.zeros_like(qp))
                s = lax.dot_general(qh, kc, NT_DIMS, preferred_element_type=F32)
                s = jnp.where(cmask, s, NEG_INF)
                m = jnp.max(s, axis=1, keepdims=True)
                e = jnp.where(cmask, jnp.exp2(s - m), 0.0)
                pb = (e / jnp.maximum(jnp.sum(e, axis=1, keepdims=True), 1e-30)).astype(BF16)
                halves.append(jnp.dot(pb, vc, preferred_element_type=F32))
                imp[g] = imp[g] + jnp.dot(pb, cov, preferred_element_type=F32)
            oc_ref[sq, :, r * LANES:(r + 1) * LANES] = jnp.where(left, halves[0], halves[1])
        first_row = pl.multiple_of(sq * tq, tq)
        for g in range(NSA_GROUPS):
            imp_ref[g, pl.ds(first_row, tq), :] = imp[g]
        return carry

    if sps == 1:
        attend(0, 0)
    else:
        lax.fori_loop(0, sps, attend, 0)

    pos = q_off + qi * tq + lax.broadcasted_iota(jnp.int32, (rows_p, 1), 0) % tq
    blk = lax.broadcasted_iota(jnp.int32, (1, nbp), 1)
    cur = pos // SLC_LEN
    forced = (blk == 0) | (blk == cur) | (blk == cur - 1)
    valid = (blk * SLC_LEN <= pos) & (blk < n_slc)
    nbr = _round_up(n_slc, 8)
    row = lax.broadcasted_iota(jnp.int32, (nbr, rows_p), 0)
    n_live = jnp.minimum(n_slc, (q_off + (qi + 1) * tq - 1) // SLC_LEN + 1)
    for g in range(NSA_GROUPS):
        val = jnp.where(valid, jnp.where(forced, imp_ref[g] + FORCE_BONUS, imp_ref[g]), NEG_INF)
        vt_ref[...] = val.T

        def body(i, cnt):
            vi = vt_ref[pl.ds(i, 1), :]
            vt = vt_ref[0:nbr, :]
            ahead = (vi > vt) | ((vi == vt) & (i < row))
            return cnt + ahead.astype(F32)

        cnt = lax.fori_loop(0, n_live, body, jnp.zeros((nbr, rows_p), F32))
        bias = jnp.where(cnt < n_sel, 0.0, NEG_INF)
        if nbp > nbr:
            bias = jnp.concatenate([bias, jnp.full((nbp - nbr, rows_p), NEG_INF, F32)], axis=0)
        bias_t = bias.T
        for sq in range(sps):
            sel_ref[sq, :, g * nbp:(g + 1) * nbp] = bias_t[sq * tq:(sq + 1) * tq].astype(BF16)


def _coverage(n_cmp, nc, n_slc, nbp):
    c0 = np.arange(nc) * CMP_STRIDE
    s0 = np.arange(nbp) * SLC_LEN
    lo = np.maximum(c0[:, None], s0[None, :])
    hi = np.minimum(c0[:, None] + CMP_LEN, s0[None, :] + SLC_LEN)
    cov = np.maximum(hi - lo, 0).astype(np.float32) / CMP_LEN
    cov[n_cmp:, :] = 0.0
    cov[:, n_slc:] = 0.0
    return jnp.asarray(cov, dtype=BF16)


def _cmp_select(q, kc, vc, *, tq, sps, n_cmp, n_slc, nbp, q_off, name):
    b, t_q, _ = q.shape
    nc = kc.shape[1]
    rows_p = _round_up(sps * tq, LANES)
    cov = _coverage(n_cmp, nc, n_slc, nbp)
    kern = functools.partial(_cmp_select_kernel, sps=sps, tq=tq, rows_p=rows_p, nc=nc, n_cmp=n_cmp, n_slc=n_slc,
                             nbp=nbp, q_off=q_off, n_sel=min(N_SELECT, n_slc))
    return pl.pallas_call(
        kern,
        grid=(b // sps, t_q // tq),
        in_specs=[pl.BlockSpec((sps, tq, N_PAIRS * LANES), lambda bi, qi: (bi, qi, 0)),
                  pl.BlockSpec((sps, nc, LANES), lambda bi, qi: (bi, 0, 0)),
                  pl.BlockSpec((sps, nc, LANES), lambda bi, qi: (bi, 0, 0)),
                  _const_spec((nc, nbp))],
        out_specs=[pl.BlockSpec((sps, tq, N_PAIRS * LANES), lambda bi, qi: (bi, qi, 0)),
                   pl.BlockSpec((sps, tq, NSA_GROUPS * nbp), lambda bi, qi: (bi, qi, 0))],
        out_shape=[jax.ShapeDtypeStruct((b, t_q, N_PAIRS * LANES), F32),
                   jax.ShapeDtypeStruct((b, t_q, NSA_GROUPS * nbp), BF16)],
        scratch_shapes=[pltpu.VMEM((NSA_GROUPS, rows_p, nbp), F32), pltpu.VMEM((nbp, rows_p), F32)],
        compiler_params=_params(("parallel", "parallel"), 40),
        name=name,
    )(q, kc, vc, cov)


def _merge_kernel(x_ref, fo_ref, oc_ref, os_ref, ow_ref, ng_ref, gf_ref, gn_ref, wuf_ref, wun_ref, wout_ref,
                  eg_ref, g_ref, b_ref, o_ref, *, alpha):
    gates = jax.nn.sigmoid(ng_ref[...])
    gx = lax.dot_general(gates, eg_ref[...], (((1,), (0,)), ((), ())), precision=HIGHEST, preferred_element_type=F32)
    w = NSA_WIDTH
    nsa_o = gx[:, :w] * oc_ref[...] + gx[:, w:2 * w] * os_ref[...] + gx[:, 2 * w:] * ow_ref[...]
    up_f = jnp.dot(fo_ref[...].astype(BF16), wuf_ref[...], preferred_element_type=F32)
    up_n = jnp.dot(nsa_o.astype(BF16), wun_ref[...], preferred_element_type=F32)
    mixed = jax.nn.sigmoid(gf_ref[...]) * up_f + jax.nn.sigmoid(gn_ref[...]) * up_n
    mix = jnp.dot(mixed.astype(BF16), wout_ref[...], preferred_element_type=F32)
    o_ref[...] = _layer_norm(alpha * x_ref[...] + mix, g_ref[...], b_ref[...])


def _gate_expand():
    perm = _nsa_perm()
    e = np.zeros((LANES, N_NSA_BRANCHES * NSA_WIDTH), np.float32)
    for br in range(N_NSA_BRANCHES):
        for pos in range(NSA_WIDTH):
            head = perm[pos] // HEAD_DIM
            e[br * NSA_HEADS + head, br * NSA_WIDTH + pos] = 1.0
    return jnp.asarray(e)


def _merge_ln(x, fox_o, o_c, o_s, o_w, ng, gf, gn, w_up_fox, w_up_nsa, w_out, g, b, alpha, name):
    n, d = x.shape
    tm = _pick(n, (256, 128, 8))
    row = lambda w: pl.BlockSpec((tm, w), lambda i: (i, 0))
    wuf = w_up_fox.astype(BF16)
    wun = w_up_nsa[_nsa_perm(), :].astype(BF16)
    wout = w_out.astype(BF16)
    eg = _gate_expand()
    return pl.pallas_call(
        functools.partial(_merge_kernel, alpha=alpha),
        grid=(n // tm,),
        in_specs=[row(d), row(FOX_WIDTH), row(NSA_WIDTH), row(NSA_WIDTH), row(NSA_WIDTH), row(LANES), row(d), row(d),
                  _const_spec(wuf.shape), _const_spec(wun.shape), _const_spec(wout.shape), _const_spec(eg.shape),
                  _const_spec((1, d)), _const_spec((1, d))],
        out_specs=row(d),
        out_shape=jax.ShapeDtypeStruct((n, d), F32),
        compiler_params=_params(("parallel",), 48),
        name=name,
    )(x, fox_o, o_c, o_s, o_w, ng, gf, gn, wuf, wun, wout, eg, g.reshape(1, d), b.reshape(1, d))


def _page_specs(block, group, n_pages, second=0):
    nd = len(block)

    def spec(slot):
        return pl.BlockSpec(block, lambda si, j, pt: (pt[si * n_pages + jnp.minimum(j * group + slot, n_pages - 1)],
                                                      second) + (0,) * (nd - 2))

    return [spec(slot) for slot in range(group)]


def _lf_pages_kernel(pt_ref, *refs, group):
    o_ref = refs[group]
    for i in range(group):
        o_ref[0, :, i * PAGE_SIZE:(i + 1) * PAGE_SIZE] = refs[i][0]


def _lf_pages(page_table, cache_lf_t):
    s, n_pages = page_table.shape
    group = _pick(n_pages, (64, 32, 16, 8, 4, 2, 1))
    grid_spec = pltpu.PrefetchScalarGridSpec(
        num_scalar_prefetch=1, grid=(s, n_pages // group),
        in_specs=_page_specs((1, FOX_HEADS, PAGE_SIZE), group, n_pages),
        out_specs=pl.BlockSpec((1, FOX_HEADS, group * PAGE_SIZE), lambda si, j, pt: (si, 0, j)))
    return pl.pallas_call(
        functools.partial(_lf_pages_kernel, group=group),
        grid_spec=grid_spec,
        out_shape=jax.ShapeDtypeStruct((s, FOX_HEADS, n_pages * PAGE_SIZE), F32),
        compiler_params=_params(("parallel", "arbitrary"), 32),
        name="lf_pages",
    )(page_table.reshape(-1), *([cache_lf_t] * group))


def _fox_sample_kernel(pt_ref, q_ref, cc_ref, cr_ref, crn_ref, kn_ref, vn_ref, *refs, group, n_steps, n_new):
    pages = refs[:group]
    o_ref, m_ref, l_ref, acc_ref = refs[group:]
    j = pl.program_id(1)

    @pl.when(j == 0)
    def _():
        m_ref[...] = jnp.full_like(m_ref, M_INIT)
        l_ref[...] = jnp.zeros_like(l_ref)
        acc_ref[...] = jnp.zeros_like(acc_ref)

    def update(s, value_dots):
        m_prev = m_ref[...]
        m_new = jnp.maximum(m_prev, jnp.max(s, axis=1, keepdims=True))
        p = jnp.exp2(s - m_new)
        alpha = jnp.exp2(m_prev - m_new)
        l_ref[...] = alpha * l_ref[...] + jnp.sum(p, axis=1, keepdims=True)
        acc_ref[...] = alpha * acc_ref[...] + jnp.concatenate(
            [value_dots(h, p[h * n_new:(h + 1) * n_new].astype(BF16)) for h in range(FOX_HEADS)], axis=0)
        m_ref[...] = m_new

    def tiles(kv, h):
        return jnp.concatenate([pages[i][0, kv, h].astype(BF16) for i in range(group)], axis=1)

    scores = [jnp.dot(q_ref[0, h], tiles(0, h), preferred_element_type=F32) + (cc_ref[0, h] - cr_ref[0, h:h + 1, :])
              for h in range(FOX_HEADS)]
    update(jnp.concatenate(scores, axis=0),
           lambda h, pb: lax.dot_general(pb, tiles(1, h), NT_DIMS, preferred_element_type=F32))

    @pl.when(j == n_steps - 1)
    def _():
        t_q = lax.broadcasted_iota(jnp.int32, (n_new, n_new), 0)
        t_k = lax.broadcasted_iota(jnp.int32, (n_new, n_new), 1)
        new_scores = []
        for h in range(FOX_HEADS):
            s = lax.dot_general(q_ref[0, h], kn_ref[0, h], NT_DIMS, preferred_element_type=F32)
            new_scores.append(jnp.where(t_k <= t_q, s + (cc_ref[0, h] - crn_ref[0, h:h + 1, :]), NEG_INF))
        update(jnp.concatenate(new_scores, axis=0),
               lambda h, pb: jnp.dot(pb, vn_ref[0, h], preferred_element_type=F32))
        o_ref[0] = acc_ref[...] / jnp.maximum(l_ref[...], 1e-30)


def _fox_sample(page_table, cache_kv_t, q_h, k_new_h, v_new_h, c_q, c_row, c_new):
    s, n_pages = page_table.shape
    n_new = q_h.shape[2]
    group = _pick(n_pages, (16, 8, 4, 2, 1))
    n_steps = n_pages // group
    rows = FOX_HEADS * n_new
    seq4 = lambda shape: pl.BlockSpec(shape, lambda si, j, pt: (si, 0, 0, 0))
    grid_spec = pltpu.PrefetchScalarGridSpec(
        num_scalar_prefetch=1, grid=(s, n_steps),
        in_specs=[seq4((1, FOX_HEADS, n_new, HEAD_DIM)), seq4((1, FOX_HEADS, n_new, 1)),
                  pl.BlockSpec((1, FOX_HEADS, group * PAGE_SIZE), lambda si, j, pt: (si, 0, j)),
                  pl.BlockSpec((1, FOX_HEADS, n_new), lambda si, j, pt: (si, 0, 0)),
                  seq4((1, FOX_HEADS, n_new, HEAD_DIM)), seq4((1, FOX_HEADS, n_new, HEAD_DIM))]
        + _page_specs((1, 2, FOX_HEADS, HEAD_DIM, PAGE_SIZE), group, n_pages),
        out_specs=pl.BlockSpec((1, rows, HEAD_DIM), lambda si, j, pt: (si, 0, 0)),
        scratch_shapes=[pltpu.VMEM((rows, 1), F32), pltpu.VMEM((rows, 1), F32), pltpu.VMEM((rows, HEAD_DIM), F32)])
    out = pl.pallas_call(
        functools.partial(_fox_sample_kernel, group=group, n_steps=n_steps, n_new=n_new),
        grid_spec=grid_spec,
        out_shape=jax.ShapeDtypeStruct((s, rows, HEAD_DIM), F32),
        compiler_params=_params(("parallel", "arbitrary"), 48),
        name="fox_sample",
    )(page_table.reshape(-1), q_h, c_q, c_row, c_new, k_new_h, v_new_h, *([cache_kv_t] * group))
    return out.reshape(s, FOX_HEADS, n_new, HEAD_DIM)


def _to_rows(tile):
    g, d, n = tile.shape
    return tile.reshape(g * d, n).T


def _compress_pages_kernel(pt_ref, pak, pbk, wak, wbk, w2k, pav, pbv, wav, wbv, w2v, *refs, group, n_steps, ns):
    pages = refs[:group]
    kc_o, vc_o, rk_ref, rv_ref = refs[group:]
    j = pl.program_id(1)
    for i in range(group):
        first = pl.multiple_of((j * group + i) * PAGE_SIZE, PAGE_SIZE)
        rk_ref[pl.ds(first, PAGE_SIZE), :] = _to_rows(pages[i][0, 0])
        rv_ref[pl.ds(first, PAGE_SIZE), :] = _to_rows(pages[i][0, 1])

    @pl.when(j == n_steps - 1)
    def _():
        half_rows = lambda ref: (lambda r: ref[pl.ds(r, ns, stride=CMP_STRIDE), :])
        kc_o[0] = _compress_rows(half_rows(rk_ref), pak, pbk, wak, wbk, w2k, ns)
        vc_o[0] = _compress_rows(half_rows(rv_ref), pav, pbv, wav, wbv, w2v, ns)


def _compress_pages(page_table, cache_kv_t, cmp_k, cmp_v):
    s, n_pages = page_table.shape
    group = _pick(n_pages, (16, 8, 4, 2, 1))
    n_steps = n_pages // group
    past = n_pages * PAGE_SIZE
    ns = past // CMP_STRIDE
    consts = list(cmp_k) + list(cmp_v)
    out = pl.BlockSpec((1, ns, LANES), lambda si, j, pt: (si, 0, 0))
    grid_spec = pltpu.PrefetchScalarGridSpec(
        num_scalar_prefetch=1, grid=(s, n_steps),
        in_specs=[_const_spec(c.shape) for c in consts]
        + _page_specs((1, 2, NSA_GROUPS, HEAD_DIM, PAGE_SIZE), group, n_pages, second=0),
        out_specs=[out, out],
        scratch_shapes=[pltpu.VMEM((past, LANES), F32), pltpu.VMEM((past, LANES), F32)])
    return pl.pallas_call(
        functools.partial(_compress_pages_kernel, group=group, n_steps=n_steps, ns=ns),
        grid_spec=grid_spec,
        out_shape=[jax.ShapeDtypeStruct((s, ns, LANES), BF16), jax.ShapeDtypeStruct((s, ns, LANES), BF16)],
        compiler_params=_params(("parallel", "arbitrary"), 48),
        name="compress_pages",
    )(page_table.reshape(-1), *consts, *([cache_kv_t] * group))


def _slc_sample_kernel(pt_ref, q_ref, sel_ref, kn_ref, vn_ref, *refs, group, n_steps, n_new, nbp, n_slc):
    pages = refs[:group]
    o_ref, m_ref, l_ref, acc_ref = refs[group:]
    j = pl.program_id(1)
    n_keys = group * PAGE_SIZE
    rows = NSA_GROUPS * NSA_HPG * n_new

    @pl.when(j == 0)
    def _():
        m_ref[...] = jnp.full_like(m_ref, M_INIT)
        l_ref[...] = jnp.zeros_like(l_ref)
        acc_ref[...] = jnp.zeros_like(acc_ref)

    def group_rows(per_query):
        return jnp.concatenate([per_query[g] for g in range(NSA_GROUPS) for _ in range(NSA_HPG)], axis=0)

    def update(s, values):
        m_prev = m_ref[...]
        m_new = jnp.maximum(m_prev, jnp.max(s, axis=1, keepdims=True))
        p = jnp.exp2(s - m_new)
        alpha = jnp.exp2(m_prev - m_new)
        l_ref[...] = alpha * l_ref[...] + jnp.sum(p, axis=1, keepdims=True)
        acc_ref[...] = alpha * acc_ref[...] + values(p.astype(BF16))
        m_ref[...] = m_new

    def tiles(c):
        return jnp.concatenate([pages[i][0, c].reshape(NSA_KV_WIDTH, PAGE_SIZE).astype(BF16) for i in range(group)],
                               axis=1)

    q = q_ref[0]
    blk_row = lax.broadcasted_iota(jnp.int32, (nbp, n_keys), 0)
    blk_key = (j * n_keys + lax.broadcasted_iota(jnp.int32, (nbp, n_keys), 1)) // SLC_LEN
    expand = (blk_row == blk_key).astype(BF16)
    bias = [jnp.dot(sel_ref[0, :, g * nbp:(g + 1) * nbp], expand, preferred_element_type=F32)
            for g in range(NSA_GROUPS)]
    update(jnp.dot(q, tiles(0), preferred_element_type=F32) + group_rows(bias),
           lambda pb: lax.dot_general(pb, tiles(1), NT_DIMS, preferred_element_type=F32))

    @pl.when(j == n_steps - 1)
    def _():
        t_q = lax.broadcasted_iota(jnp.int32, (rows, n_new), 0) % n_new
        t_k = lax.broadcasted_iota(jnp.int32, (rows, n_new), 1)
        last_blk = [sel_ref[0, :, g * nbp + n_slc - 1:g * nbp + n_slc].astype(F32) for g in range(NSA_GROUPS)]
        s = lax.dot_general(q, kn_ref[0], NT_DIMS, preferred_element_type=F32) + group_rows(last_blk)
        update(jnp.where(t_k <= t_q, s, NEG_INF), lambda pb: jnp.dot(pb, vn_ref[0], preferred_element_type=F32))
        o_ref[0] = acc_ref[...] / jnp.maximum(l_ref[...], 1e-30)


def _slc_sample(page_table, cache_kv_t, q_rows, sel, ks_new, vs_new, *, nbp, n_slc):
    s, n_pages = page_table.shape
    n_new = ks_new.shape[1]
    rows = q_rows.shape[1]
    group = _pick(n_pages, (16, 8, 4, 2, 1))
    n_steps = n_pages // group
    seq = lambda shape: pl.BlockSpec(shape, lambda si, j, pt: (si, 0, 0))
    grid_spec = pltpu.PrefetchScalarGridSpec(
        num_scalar_prefetch=1, grid=(s, n_steps),
        in_specs=[seq((1, rows, LANES)), seq((1, n_new, NSA_GROUPS * nbp)), seq((1, n_new, LANES)),
                  seq((1, n_new, LANES))]
        + _page_specs((1, 2, NSA_GROUPS, HEAD_DIM, PAGE_SIZE), group, n_pages, second=1),
        out_specs=seq((1, rows, LANES)),
        scratch_shapes=[pltpu.VMEM((rows, 1), F32), pltpu.VMEM((rows, 1), F32), pltpu.VMEM((rows, LANES), F32)])
    return pl.pallas_call(
        functools.partial(_slc_sample_kernel, group=group, n_steps=n_steps, n_new=n_new, nbp=nbp, n_slc=n_slc),
        grid_spec=grid_spec,
        out_shape=jax.ShapeDtypeStruct((s, rows, LANES), F32),
        compiler_params=_params(("parallel", "arbitrary"), 40),
        name="slc_sample",
    )(page_table.reshape(-1), q_rows, sel, ks_new, vs_new, *([cache_kv_t] * group))


def _win_rows_kernel(st_ref, kwnew_ref, vwnew_ref, kw_o, vw_o, *, keep, n_new):
    for c, new_ref, o_ref in ((0, kwnew_ref, kw_o), (1, vwnew_ref, vw_o)):
        o_ref[0] = jnp.zeros(o_ref.shape[1:], BF16)
        o_ref[0, 0:keep, :] = _to_rows(st_ref[0, c]).astype(BF16)
        o_ref[0, keep:keep + n_new, :] = new_ref[0]


def _win_rows(state_t, kw_new, vw_new, t_all):
    s, keep = state_t.shape[0], state_t.shape[-1]
    n_new = kw_new.shape[1]
    seq = lambda si: (si, 0, 0)
    return pl.pallas_call(
        functools.partial(_win_rows_kernel, keep=keep, n_new=n_new),
        grid=(s,),
        in_specs=[pl.BlockSpec((1, 2, NSA_GROUPS, HEAD_DIM, keep), lambda si: (si, 0, 0, 0, 0)),
                  pl.BlockSpec((1, n_new, LANES), seq), pl.BlockSpec((1, n_new, LANES), seq)],
        out_specs=[pl.BlockSpec((1, t_all, LANES), seq), pl.BlockSpec((1, t_all, LANES), seq)],
        out_shape=[jax.ShapeDtypeStruct((s, t_all, LANES), BF16), jax.ShapeDtypeStruct((s, t_all, LANES), BF16)],
        compiler_params=_params(("parallel",), 32),
        name="win_rows",
    )(state_t, kw_new, vw_new)


def _rows_minor(a):
    return jnp.moveaxis(a, 1, -1)


def kernel(x_prompt, x_sample, cache_fox_kv, cache_fox_logf, cache_nsa_kv, state_win_kv, page_table, ln1_g, ln1_b, ffn1_w_up, ffn1_w_down, w_in, b_fgate, cmp_pos_k, cmp_wk1, cmp_wk2, cmp_pos_v, cmp_wv1, cmp_wv2, w_up_fox, w_up_nsa, w_out, ln2_g, ln2_b, ffn2_w_up, ffn2_w_down, ln3_g, ln3_b):
    depth = ln1_g.shape[0]
    assert depth == 1, "single-layer step"
    alpha = (2.0 * depth) ** 0.25
    bsz, seq, d = x_prompt.shape
    sb, n_new, _ = x_sample.shape
    n_pages = page_table.shape[1]
    past = n_pages * PAGE_SIZE
    keep = state_win_kv.shape[2]
    assert seq % 256 == 0 and past % SLC_LEN == 0 and n_new < CMP_STRIDE and keep == WINDOW
    layer = 0

    ws, bf = _split_w_in(w_in[layer], b_fgate[layer])
    cmp_k = _compress_weights(cmp_pos_k[layer], cmp_wk1[layer], cmp_wk2[layer])
    cmp_v = _compress_weights(cmp_pos_v[layer], cmp_wv1[layer], cmp_wv2[layer])

    n = bsz * seq
    xp = _ffn_ln(x_prompt.reshape(n, d), ffn1_w_up[layer], ffn1_w_down[layer], ln1_g[layer], ln1_b[layer], alpha,
                 "ffn1_prompt")
    cos_p, sin_p = _rope_tables(jnp.arange(seq, dtype=jnp.int32))
    mp = _in_proj(xp, ws, bf, cos_p, sin_p, "in_proj_prompt")
    r3 = lambda a: a.reshape(bsz, seq, a.shape[-1])

    c_p = _cumsum_time(r3(mp["lf"]).transpose(0, 2, 1), "cumsum_prompt")
    tq = _pick(seq, (512, 256))
    tk = _pick(seq, (512, 256))
    c_limbs = _limbs(c_p * LOG2E)
    q_fox = _augment(_head_major(r3(mp["fq"]), FOX_HEADS), [1.0] * 3 + c_limbs)
    k_fox = _augment(_head_major(r3(mp["fk"]), FOX_HEADS), [-c for c in c_limbs] + [1.0] * 3)
    v_fox = _augment(_head_major(r3(mp["fv"]), FOX_HEADS), [1.0])
    fox_o = _attention("fox", q_fox, k_fox, v_fox, tq=tq, tk=tk, q_off=0, k_off=0, name="fox_prompt")

    nkv3 = r3(mp["nkv"])
    kc_p = _compress(nkv3, 0, cmp_k, "compress_k_prompt")
    vc_p = _compress(nkv3, 1, cmp_v, "compress_v_prompt")
    n_cmp_p = seq // CMP_STRIDE - 1
    n_slc_p = -(-seq // SLC_LEN)
    nbp_p = _round_up(n_slc_p, LANES)
    nq3 = r3(mp["nq"])
    oc_p, sel_p = _cmp_select(nq3, kc_p, vc_p, tq=256, sps=1, n_cmp=n_cmp_p, n_slc=n_slc_p, nbp=nbp_p, q_off=0,
                              name="cmp_select_prompt")
    assert n_slc_p <= LANES - HEAD_DIM
    nq_h = _head_major(nq3, NSA_HEADS)
    sel_g = sel_p.reshape(bsz, seq, NSA_GROUPS, nbp_p)[..., :LANES - HEAD_DIM].transpose(0, 2, 1, 3)
    q_slc = jnp.concatenate([nq_h, jnp.tile(sel_g, (1, NSA_HPG, 1, 1))], axis=-1)
    one_hot = (jnp.arange(seq)[:, None] // SLC_LEN == jnp.arange(LANES - HEAD_DIM)[None, :]).astype(BF16)
    ks_h = _head_major(r3(mp["ks"]), NSA_GROUPS)
    k_slc = jnp.concatenate([ks_h, jnp.broadcast_to(one_hot, ks_h.shape)], axis=-1)
    v_slc = _augment(_head_major(r3(mp["vs"]), NSA_GROUPS), [1.0])
    os_p = _attention("slc", q_slc, k_slc, v_slc, tq=tq, tk=tk, q_off=0, k_off=0, name="slc_prompt")
    q_win = _augment(nq_h, [])
    k_win = _augment(_head_major(r3(mp["kw"]), NSA_GROUPS), [])
    v_win = _augment(_head_major(r3(mp["vw"]), NSA_GROUPS), [1.0])
    ow_p = _attention("win", q_win, k_win, v_win, tq=256, tk=256, q_off=0, k_off=0, name="win_prompt")
    flat = lambda a: a.reshape(n, a.shape[-1])
    xp2 = _merge_ln(xp, flat(fox_o), flat(oc_p), flat(os_p), flat(ow_p), mp["ng"], mp["gf"], mp["gn"],
                    w_up_fox[layer], w_up_nsa[layer], w_out[layer], ln2_g[layer], ln2_b[layer], alpha, "merge_prompt")
    yp = _ffn_ln(xp2, ffn2_w_up[layer], ffn2_w_down[layer], ln3_g[layer], ln3_b[layer], alpha, "ffn2_prompt")

    ns_rows = sb * n_new
    xs = _ffn_ln(x_sample.reshape(ns_rows, d), ffn1_w_up[layer], ffn1_w_down[layer], ln1_g[layer], ln1_b[layer],
                 alpha, "ffn1_sample")
    cos_s, sin_s = _rope_tables(past + jnp.arange(n_new, dtype=jnp.int32))
    tile_rows = _pick(ns_rows, (256, 128, 8))
    reps = tile_rows // n_new
    ms = _in_proj(xs, ws, bf, jnp.tile(cos_s, (reps, 1)), jnp.tile(sin_s, (reps, 1)), "in_proj_sample")
    s3 = lambda a: a.reshape(sb, n_new, a.shape[-1])
    by_head = lambda a: s3(a).reshape(sb, n_new, FOX_HEADS, HEAD_DIM).transpose(0, 2, 1, 3)

    lf_past = _lf_pages(page_table, _rows_minor(cache_fox_logf[layer]))
    lf_new = s3(ms["lf"]).transpose(0, 2, 1)
    t_cs = _round_up(past + n_new, 8 * LANES)
    c_s = _cumsum_time(jnp.pad(jnp.concatenate([lf_past, lf_new], axis=2), ((0, 0), (0, 0), (0, t_cs - past - n_new))),
                       "cumsum_sample") * LOG2E
    c_new = c_s[:, :, past:past + n_new]
    fox_o_s = _fox_sample(page_table, _rows_minor(cache_fox_kv[layer]), by_head(ms["fq"]), by_head(ms["fk"]),
                          by_head(ms["fv"]), c_new[..., None], c_s, c_new)
    fox_o_s = fox_o_s.transpose(0, 2, 1, 3).reshape(ns_rows, FOX_WIDTH)

    nsa_t = _rows_minor(cache_nsa_kv[layer])
    kc_s, vc_s = _compress_pages(page_table, nsa_t, cmp_k, cmp_v)
    n_cmp_s = (past + n_new) // CMP_STRIDE - 1
    n_slc_s = -(-(past + n_new) // SLC_LEN)
    nbp_s = _round_up(n_slc_s, LANES)
    nq_s = s3(ms["nq"])
    sps = _pick(sb, tuple(c for c in (16, 8, 4, 2, 1) if c * n_new <= LANES))
    oc_s, sel_s = _cmp_select(nq_s, kc_s, vc_s, tq=n_new, sps=sps, n_cmp=n_cmp_s, n_slc=n_slc_s, nbp=nbp_s,
                              q_off=past, name="cmp_select_sample")
    q_ghtd = nq_s.reshape(sb, n_new, NSA_HPG, NSA_GROUPS, HEAD_DIM).transpose(0, 3, 2, 1, 4)
    zero = jnp.zeros_like(q_ghtd[:, 0])
    q_rows = jnp.concatenate([jnp.concatenate([q_ghtd[:, 0], zero], axis=-1),
                              jnp.concatenate([zero, q_ghtd[:, 1]], axis=-1)], axis=1).reshape(sb, -1, LANES)
    o_rows = _slc_sample(page_table, nsa_t, q_rows, sel_s, s3(ms["ks"]), s3(ms["vs"]), nbp=nbp_s, n_slc=n_slc_s)
    o_rows = o_rows.reshape(sb, NSA_GROUPS, NSA_HPG, n_new, LANES)
    os_s = jnp.stack([o_rows[:, 0, :, :, :HEAD_DIM], o_rows[:, 1, :, :, HEAD_DIM:]], axis=3)
    os_s = os_s.transpose(0, 2, 1, 3, 4).reshape(sb, n_new, NSA_WIDTH)
    t_win = _round_up(keep + n_new, LANES)
    kw_all, vw_all = _win_rows(_rows_minor(state_win_kv[layer]), s3(ms["kw"]), s3(ms["vw"]), t_win)
    ow_s = _attention("win", _augment(_head_major(nq_s, NSA_HEADS), []),
                      _augment(_head_major(kw_all, NSA_GROUPS), []), _augment(_head_major(vw_all, NSA_GROUPS), [1.0]),
                      tq=n_new, tk=t_win, q_off=past, k_off=past - keep, name="win_sample")
    flat_s = lambda a: a.reshape(ns_rows, a.shape[-1])
    xs2 = _merge_ln(xs, fox_o_s, flat_s(oc_s), flat_s(os_s), flat_s(ow_s), ms["ng"], ms["gf"], ms["gn"],
                    w_up_fox[layer], w_up_nsa[layer], w_out[layer], ln2_g[layer], ln2_b[layer], alpha, "merge_sample")
    ys = _ffn_ln(xs2, ffn2_w_up[layer], ffn2_w_down[layer], ln3_g[layer], ln3_b[layer], alpha, "ffn2_sample")

    fox_kv_p = mp["fkv"].reshape(1, bsz, seq, 2, FOX_HEADS, HEAD_DIM)
    fox_kv_s = ms["fkv"].reshape(1, sb, n_new, 2, FOX_HEADS, HEAD_DIM)
    logf_p = mp["lf"].reshape(1, bsz, seq, FOX_HEADS)
    logf_s = ms["lf"].reshape(1, sb, n_new, FOX_HEADS)
    nsa_kv_p = mp["nkv"].reshape(1, bsz, seq, 4, NSA_GROUPS, HEAD_DIM)
    nsa_kv_s = ms["nkv"].reshape(1, sb, n_new, 4, NSA_GROUPS, HEAD_DIM)
    win_rows_p = r3(mp["win"]).reshape(bsz, seq, 2, NSA_GROUPS, HEAD_DIM)
    if seq >= keep:
        win_p = win_rows_p[:, seq - keep:]
    else:
        win_p = jnp.pad(win_rows_p, ((0, 0), (keep - seq, 0), (0, 0), (0, 0), (0, 0)))
    new_win = ms["win"].reshape(sb, n_new, 2, NSA_GROUPS, HEAD_DIM).astype(state_win_kv.dtype)
    win_s = jnp.concatenate([state_win_kv[layer], new_win], axis=1)[:, -keep:]
    return (yp.reshape(bsz, seq, d), ys.reshape(sb, n_new, d), fox_kv_p, fox_kv_s, logf_p, logf_s,
            nsa_kv_p, nsa_kv_s, win_p[None], win_s[None])
```

```python
import functools

import numpy as np
import jax
import jax.numpy as jnp
from jax import lax
from jax.experimental import pallas as pl
from jax.experimental.pallas import tpu as pltpu

F32 = jnp.float32
BF16 = jnp.bfloat16

HEAD_DIM = 64
FOX_HEADS = 8
NSA_HEADS = 8
NSA_GROUPS = 2
NSA_HPG = NSA_HEADS // NSA_GROUPS
FOX_WIDTH = FOX_HEADS * HEAD_DIM
NSA_WIDTH = NSA_HEADS * HEAD_DIM
NSA_KV_WIDTH = NSA_GROUPS * HEAD_DIM
N_NSA_BRANCHES = 3
CMP_LEN = 32
CMP_STRIDE = 16
CMP_HIDDEN = 128
SLC_LEN = 64
N_SELECT = 16
WINDOW = 512
PAGE_SIZE = 128
ROPE_THETA = 10000.0
LN_EPS = 1e-5
FORCE_BONUS = 1e4
NEG_INF = -1e30
M_INIT = 0.1 * NEG_INF
LOG2E = 1.4426950408889634
QSCALE = HEAD_DIM ** -0.5 * LOG2E
SUM_LANE = HEAD_DIM

LANES = 128
N_PAIRS = 4
MIB = 1024 * 1024
HIGHEST = lax.Precision.HIGHEST
NT_DIMS = (((1,), (1,)), ((), ()))


def _params(semantics, vmem_mib):
    return pltpu.CompilerParams(dimension_semantics=semantics, vmem_limit_bytes=vmem_mib * MIB)


def _pick(n, candidates):
    for c in candidates:
        if n % c == 0:
            return c
    return n


def _round_up(n, m):
    return (n + m - 1) // m * m


def _layer_norm(y, g, b):
    mu = jnp.mean(y, axis=-1, keepdims=True)
    d = y - mu
    var = jnp.mean(d * d, axis=-1, keepdims=True)
    return d * lax.rsqrt(var + LN_EPS) * g + b


def _const_spec(shape):
    nd = len(shape)
    return pl.BlockSpec(shape, lambda *_: (0,) * nd, pipeline_mode=pl.Buffered(1))


def _ffn_ln_kernel(x_ref, wa_ref, wb_ref, wd_ref, g_ref, b_ref, o_ref, acc_ref, *, alpha, n_chunks):
    x = x_ref[...]
    xb = x.astype(BF16)
    acc_ref[...] = jnp.zeros_like(acc_ref)

    def body(c, carry):
        a = jnp.dot(xb, wa_ref[c], preferred_element_type=F32)
        b = jnp.dot(xb, wb_ref[c], preferred_element_type=F32)
        h = (a * jax.nn.sigmoid(a) * b).astype(BF16)
        acc_ref[...] += jnp.dot(h, wd_ref[c], preferred_element_type=F32)
        return carry

    lax.fori_loop(0, n_chunks, body, 0)
    o_ref[...] = _layer_norm(alpha * x + 0.5 * acc_ref[...], g_ref[...], b_ref[...])


def _ffn_ln(x, w_up, w_down, g, b, alpha, name):
    n, d = x.shape
    f = w_down.shape[0]
    fc = _pick(f, (256, 128))
    nc = f // fc
    wa = w_up[:, :f].astype(BF16).reshape(d, nc, fc).transpose(1, 0, 2)
    wb = w_up[:, f:].astype(BF16).reshape(d, nc, fc).transpose(1, 0, 2)
    wd = w_down.astype(BF16).reshape(nc, fc, d)
    tm = _pick(n, (1024, 512, 256, 128, 8))
    return pl.pallas_call(
        functools.partial(_ffn_ln_kernel, alpha=alpha, n_chunks=nc),
        grid=(n // tm,),
        in_specs=[pl.BlockSpec((tm, d), lambda i: (i, 0)),
                  _const_spec((nc, d, fc)), _const_spec((nc, d, fc)), _const_spec((nc, fc, d)),
                  _const_spec((1, d)), _const_spec((1, d))],
        out_specs=pl.BlockSpec((tm, d), lambda i: (i, 0)),
        out_shape=jax.ShapeDtypeStruct((n, d), F32),
        scratch_shapes=[pltpu.VMEM((tm, d), F32)],
        compiler_params=_params(("parallel",), 56),
        name=name,
    )(x, wa, wb, wd, g.reshape(1, d), b.reshape(1, d))


def _rope_tables(pos):
    half = HEAD_DIM // 2
    inv_freq = ROPE_THETA ** (-jnp.arange(half, dtype=F32) / half)
    ang = pos.astype(F32)[:, None] * inv_freq[None, :]
    cos, sin = jnp.cos(ang), jnp.sin(ang)
    cos64 = jnp.concatenate([cos, cos], axis=-1)
    sin64 = jnp.concatenate([-sin, sin], axis=-1)
    return jnp.tile(cos64, (1, LANES // HEAD_DIM)), jnp.tile(sin64, (1, LANES // HEAD_DIM))


def _in_proj_kernel(x_ref, wfq, wfkv, wsm, wnq, wnkv, wwin, wgf, wgn, bf_ref, cos_ref, sin_ref,
                    fq_o, fkv_o, fk_o, fv_o, lf_o, nq_o, nkv_o, ks_o, vs_o, win_o, kw_o, vw_o,
                    ng_o, gf_o, gn_o):
    xb = x_ref[...].astype(BF16)
    cos = cos_ref[...]
    sin = sin_ref[...]
    lane = lax.broadcasted_iota(jnp.int32, (1, LANES), 1)
    low_half = (lane % HEAD_DIM) < (HEAD_DIM // 2)

    def rope(v):
        partner = jnp.where(low_half, pltpu.roll(v, LANES - HEAD_DIM // 2, 1), pltpu.roll(v, HEAD_DIM // 2, 1))
        return v * cos + partner * sin

    def proj(w_ref):
        return jnp.dot(xb, w_ref[...], preferred_element_type=F32)

    fq_o[...] = (proj(wfq) * QSCALE).astype(BF16)
    fkv = proj(wfkv)
    fkv_o[...] = fkv
    fk_o[...] = fkv[:, :FOX_WIDTH].astype(BF16)
    fv_o[...] = fkv[:, FOX_WIDTH:].astype(BF16)

    sm = proj(wsm)
    z = sm[:, :LANES] + bf_ref[...]
    logf = jnp.minimum(z, 0.0) - jnp.log1p(jnp.exp(-jnp.abs(z)))
    lf_o[...] = logf[:, :FOX_HEADS]
    ng_o[...] = sm[:, LANES:]

    nq = proj(wnq)
    for r in range(N_PAIRS):
        nq_o[:, r * LANES:(r + 1) * LANES] = (rope(nq[:, r * LANES:(r + 1) * LANES]) * QSCALE).astype(BF16)

    nkv = proj(wnkv)
    k_cmp = rope(nkv[:, 0:LANES])
    k_slc = rope(nkv[:, 2 * LANES:3 * LANES])
    v_slc = nkv[:, 3 * LANES:4 * LANES]
    nkv_o[:, 0:LANES] = k_cmp
    nkv_o[:, LANES:2 * LANES] = nkv[:, LANES:2 * LANES]
    nkv_o[:, 2 * LANES:3 * LANES] = k_slc
    nkv_o[:, 3 * LANES:4 * LANES] = v_slc
    ks_o[...] = k_slc.astype(BF16)
    vs_o[...] = v_slc.astype(BF16)

    win = proj(wwin)
    k_win = rope(win[:, :LANES])
    v_win = win[:, LANES:]
    win_o[:, :LANES] = k_win
    win_o[:, LANES:] = v_win
    kw_o[...] = k_win.astype(BF16)
    vw_o[...] = v_win.astype(BF16)

    gf_o[...] = proj(wgf)
    gn_o[...] = proj(wgn)


def _nsa_perm():
    perm = np.zeros(NSA_WIDTH, np.int32)
    for r in range(NSA_HPG):
        for g in range(NSA_GROUPS):
            for d in range(HEAD_DIM):
                perm[r * LANES + g * HEAD_DIM + d] = (g * NSA_HPG + r) * HEAD_DIM + d
    return perm


def _split_w_in(w_in, b_fgate):
    d = w_in.shape[0]
    sizes = (FOX_WIDTH, FOX_WIDTH, FOX_WIDTH, FOX_HEADS, NSA_WIDTH) + (NSA_KV_WIDTH,) * 6 + (
        N_NSA_BRANCHES * NSA_HEADS, d, d)
    offs = np.concatenate([[0], np.cumsum(sizes)])
    col = lambda i, j=None: w_in[:, offs[i]:offs[(i if j is None else j) + 1]]
    pad = lambda w: jnp.pad(w, ((0, 0), (0, LANES - w.shape[1])))
    ws = dict(
        wfq=col(0), wfkv=col(1, 2),
        wsm=jnp.concatenate([pad(col(3)), pad(col(11))], axis=1),
        wnq=col(4)[:, _nsa_perm()], wnkv=col(5, 8), wwin=col(9, 10), wgf=col(12), wgn=col(13))
    ws = {k: v.astype(BF16) for k, v in ws.items()}
    bf = jnp.pad(b_fgate.astype(F32), (0, LANES - FOX_HEADS)).reshape(1, LANES)
    return ws, bf


def _in_proj(x, ws, bf, cos_tab, sin_tab, name):
    n, d = x.shape
    n_tab = cos_tab.shape[0]
    tm = _pick(n_tab, (256, 128, 8))
    tab_tiles = n_tab // tm
    row = lambda w: pl.BlockSpec((tm, w), lambda i: (i, 0))
    tab = pl.BlockSpec((tm, LANES), lambda i: (i % tab_tiles, 0))
    names = ("wfq", "wfkv", "wsm", "wnq", "wnkv", "wwin", "wgf", "wgn")
    outs = [("fq", FOX_WIDTH, BF16), ("fkv", 2 * FOX_WIDTH, F32), ("fk", FOX_WIDTH, BF16), ("fv", FOX_WIDTH, BF16),
            ("lf", FOX_HEADS, F32), ("nq", NSA_WIDTH, BF16), ("nkv", 4 * NSA_KV_WIDTH, F32),
            ("ks", NSA_KV_WIDTH, BF16), ("vs", NSA_KV_WIDTH, BF16), ("win", 2 * NSA_KV_WIDTH, F32),
            ("kw", NSA_KV_WIDTH, BF16), ("vw", NSA_KV_WIDTH, BF16), ("ng", LANES, F32), ("gf", d, F32), ("gn", d, F32)]
    res = pl.pallas_call(
        _in_proj_kernel,
        grid=(n // tm,),
        in_specs=[row(d)] + [_const_spec(ws[k].shape) for k in names] + [_const_spec((1, LANES)), tab, tab],
        out_specs=[row(w) for _, w, _ in outs],
        out_shape=[jax.ShapeDtypeStruct((n, w), dt) for _, w, dt in outs],
        compiler_params=_params(("parallel",), 56),
        name=name,
    )(x, *[ws[k] for k in names], bf, cos_tab, sin_tab)
    return {k: v for (k, _, _), v in zip(outs, res)}


def _cumsum_kernel(x_ref, tri_ref, low_ref, o_ref):
    x = x_ref[0]
    within = lax.dot_general(x, tri_ref[...], (((1,), (0,)), ((), ())), precision=HIGHEST, preferred_element_type=F32)
    tot = jnp.broadcast_to(within[:, LANES - 1:LANES], within.shape)
    before = lax.dot_general(low_ref[...], tot, (((1,), (0,)), ((), ())), precision=HIGHEST, preferred_element_type=F32)
    o_ref[0] = within + before


def _cumsum_time(logf_t, name):
    b, h, t = logf_t.shape
    r = t // LANES
    rows = h * r
    idx = np.arange(LANES)
    tri = jnp.asarray((idx[:, None] <= idx[None, :]).astype(np.float32))
    ridx = np.arange(rows)
    low = jnp.asarray(((ridx[None, :] < ridx[:, None]) & (ridx[None, :] // r == ridx[:, None] // r)).astype(np.float32))
    out = pl.pallas_call(
        _cumsum_kernel,
        grid=(b,),
        in_specs=[pl.BlockSpec((1, rows, LANES), lambda i: (i, 0, 0)), _const_spec((LANES, LANES)),
                  _const_spec((rows, rows))],
        out_specs=pl.BlockSpec((1, rows, LANES), lambda i: (i, 0, 0)),
        out_shape=jax.ShapeDtypeStruct((b, rows, LANES), F32),
        compiler_params=_params(("parallel",), 32),
        name=name,
    )(logf_t.reshape(b, rows, LANES), tri, low)
    return out.reshape(b, h, t)


FLAG_FIRST, FLAG_LAST, FLAG_EDGE = 1, 2, 4


def _tile_schedule(mode, nq, nkt, tq, tk, q_off, k_off):
    qi_tab, kt_tab, flag_tab = [], [], []
    for qi in range(nq):
        q_lo, q_hi = q_off + qi * tq, q_off + (qi + 1) * tq - 1
        tiles = []
        for kt in range(nkt):
            k_lo, k_hi = k_off + kt * tk, k_off + (kt + 1) * tk - 1
            if mode == "win":
                visible = (q_hi - k_lo >= 0) and (q_lo - k_hi < WINDOW)
                interior = (q_lo - k_hi >= 0) and (q_hi - k_lo < WINDOW)
            else:
                visible = k_lo <= q_hi
                interior = k_hi <= q_lo
            if visible:
                tiles.append((kt, 0 if interior else FLAG_EDGE))
        assert tiles, "every query tile sees at least one key tile"
        for n, (kt, flag) in enumerate(tiles):
            qi_tab.append(qi)
            kt_tab.append(kt)
            flag_tab.append(flag | (FLAG_FIRST if n == 0 else 0) | (FLAG_LAST if n == len(tiles) - 1 else 0))
    as_i32 = lambda v: jnp.asarray(np.asarray(v, np.int32))
    return as_i32(qi_tab), as_i32(kt_tab), as_i32(flag_tab)


def _attn_kernel(qi_tab, kt_tab, flag_tab, q_ref, k_ref, v_ref, o_ref, m_ref, acc_ref, *,
                 mode, tq, tk, nh, n_kv, q_off, k_off):
    step = pl.program_id(1)
    qi = qi_tab[step]
    kt = kt_tab[step]
    flags = flag_tab[step]
    lane = lax.broadcasted_iota(jnp.int32, (1, LANES), 1)
    n_chunks = tk // LANES

    @pl.when((flags & FLAG_FIRST) != 0)
    def _():
        m_ref[...] = jnp.full_like(m_ref, M_INIT)
        acc_ref[...] = jnp.zeros_like(acc_ref)

    def tile(edge):
        if edge:
            q_pos = q_off + qi * tq + lax.broadcasted_iota(jnp.int32, (tq, 1), 0)
        scores = lambda i: lax.dot_general(q_ref[0, i], k_ref[0, i % n_kv], NT_DIMS, preferred_element_type=F32)

        def values(i, alpha, pb):
            acc_ref[i] = alpha * acc_ref[i] + jnp.dot(pb, v_ref[0, i % n_kv], preferred_element_type=F32)

        s_next = scores(0)
        pending = None
        for i in range(nh):
            s = s_next
            if i + 1 < nh:
                s_next = scores(i + 1)
            if pending is not None:
                values(*pending)
            chunks = []
            for c in range(n_chunks):
                t = s[:, c * LANES:(c + 1) * LANES]
                if edge:
                    k_pos = k_off + kt * tk + c * LANES + lane
                    if mode == "win":
                        rel = q_pos - k_pos
                        t = jnp.where((rel >= 0) & (rel < WINDOW), t, NEG_INF)
                    else:
                        t = jnp.where(k_pos <= q_pos, t, NEG_INF)
                chunks.append(t)
            mx = chunks[0]
            for t in chunks[1:]:
                mx = jnp.maximum(mx, t)
            m_prev = m_ref[i]
            m_new = jnp.maximum(m_prev, jnp.max(mx, axis=1, keepdims=True))
            alpha = jnp.exp2(m_prev - m_new)
            ps = [jnp.exp2(t - m_new).astype(BF16) for t in chunks]
            pb = jnp.concatenate(ps, axis=1) if n_chunks > 1 else ps[0]
            m_ref[i] = m_new
            pending = (i, alpha, pb)
        values(*pending)

    pl.when((flags & FLAG_EDGE) != 0)(lambda: tile(True))
    pl.when((flags & FLAG_EDGE) == 0)(lambda: tile(False))

    @pl.when((flags & FLAG_LAST) != 0)
    def _():
        for pr in range(nh // 2):
            outs = []
            for side in range(2):
                acc = acc_ref[2 * pr + side]
                outs.append(acc / jnp.maximum(acc[:, SUM_LANE:SUM_LANE + 1], 1e-30))
            o_ref[0, :, pr * LANES:(pr + 1) * LANES] = jnp.where(lane < HEAD_DIM, outs[0],
                                                                  pltpu.roll(outs[1], HEAD_DIM, 1))


def _attention(mode, q, k, v, *, tq, tk, q_off, k_off, name):
    b, nh, t_q, _ = q.shape
    n_kv, t_k = k.shape[1], k.shape[2]
    qi_tab, kt_tab, flag_tab = _tile_schedule(mode, t_q // tq, t_k // tk, tq, tk, q_off, k_off)
    n_steps = qi_tab.shape[0]
    q_spec = pl.BlockSpec((1, nh, tq, LANES), lambda bi, s, qt, kt, fl: (bi, 0, qt[s], 0))
    kv_spec = pl.BlockSpec((1, n_kv, tk, LANES), lambda bi, s, qt, kt, fl: (bi, 0, kt[s], 0))
    grid_spec = pltpu.PrefetchScalarGridSpec(
        num_scalar_prefetch=3, grid=(b, n_steps), in_specs=[q_spec, kv_spec, kv_spec],
        out_specs=pl.BlockSpec((1, tq, nh // 2 * LANES), lambda bi, s, qt, kt, fl: (bi, qt[s], 0)),
        scratch_shapes=[pltpu.VMEM((nh, tq, LANES), F32), pltpu.VMEM((nh, tq, LANES), F32)])
    return pl.pallas_call(
        functools.partial(_attn_kernel, mode=mode, tq=tq, tk=tk, nh=nh, n_kv=n_kv, q_off=q_off, k_off=k_off),
        grid_spec=grid_spec,
        out_shape=jax.ShapeDtypeStruct((b, t_q, nh // 2 * LANES), F32),
        compiler_params=_params(("parallel", "arbitrary"), 48),
        name=name,
    )(qi_tab, kt_tab, flag_tab, q, k, v)


def _limbs(x):
    hi = x.astype(BF16).astype(F32)
    mid = (x - hi).astype(BF16).astype(F32)
    lo = (x - hi - mid).astype(BF16).astype(F32)
    return hi, mid, lo


def _head_of_pair(pair, side):
    return pair if side == 0 else pltpu.roll(pair, HEAD_DIM, 1)


def _augment_fox_kernel(q_ref, k_ref, v_ref, c_ref, qa_ref, ka_ref, va_ref):
    lane = lax.broadcasted_iota(jnp.int32, (1, LANES), 1)
    data = lane < HEAD_DIM
    c = c_ref[0] * LOG2E
    for h in range(FOX_HEADS):
        p, side = divmod(h, 2)
        hi, mid, lo = _limbs(c[:, h:h + 1])
        q_extra = jnp.where(lane < HEAD_DIM + 3, 1.0,
                            jnp.where(lane == HEAD_DIM + 3, hi,
                                      jnp.where(lane == HEAD_DIM + 4, mid, jnp.where(lane == HEAD_DIM + 5, lo, 0.0))))
        k_extra = jnp.where(lane == HEAD_DIM, -hi,
                            jnp.where(lane == HEAD_DIM + 1, -mid,
                                      jnp.where(lane == HEAD_DIM + 2, -lo, jnp.where(lane < HEAD_DIM + 6, 1.0, 0.0))))
        v_extra = jnp.where(lane == SUM_LANE, 1.0, 0.0)
        cols = slice(p * LANES, (p + 1) * LANES)
        qa_ref[0, h] = jnp.where(data, _head_of_pair(q_ref[0, :, cols].astype(F32), side), q_extra).astype(BF16)
        ka_ref[0, h] = jnp.where(data, _head_of_pair(k_ref[0, :, cols].astype(F32), side), k_extra).astype(BF16)
        va_ref[0, h] = jnp.where(data, _head_of_pair(v_ref[0, :, cols].astype(F32), side), v_extra).astype(BF16)


def _augment_fox(q, k, v, c_col):
    b, t, w = q.shape
    tr = _pick(t, (512, 256, 128, 8))
    row = pl.BlockSpec((1, tr, w), lambda bi, i: (bi, i, 0))
    out = pl.BlockSpec((1, FOX_HEADS, tr, LANES), lambda bi, i: (bi, 0, i, 0))
    shape = jax.ShapeDtypeStruct((b, FOX_HEADS, t, LANES), BF16)
    return pl.pallas_call(
        _augment_fox_kernel,
        grid=(b, t // tr),
        in_specs=[row, row, row, pl.BlockSpec((1, tr, FOX_HEADS), lambda bi, i: (bi, i, 0))],
        out_specs=[out, out, out],
        out_shape=[shape, shape, shape],
        compiler_params=_params(("parallel", "parallel"), 32),
        name="augment_fox",
    )(q, k, v, c_col)


def _augment_nsa_kernel(nq_ref, sel_ref, ks_ref, vs_ref, kw_ref, vw_ref, qa_ref, ksa_ref, vsa_ref, kwa_ref, vwa_ref,
                        *, tr, nbp):
    lane = lax.broadcasted_iota(jnp.int32, (1, LANES), 1)
    data = lane < HEAD_DIM
    t_row = pl.program_id(1) * tr + lax.broadcasted_iota(jnp.int32, (tr, 1), 0)
    block_hot = jnp.where(t_row // SLC_LEN == lane - HEAD_DIM, 1.0, 0.0)
    ones_lane = jnp.where(lane == SUM_LANE, 1.0, 0.0)
    for g in range(NSA_GROUPS):
        bias = pltpu.roll(sel_ref[0, :, g * nbp:g * nbp + LANES].astype(F32), HEAD_DIM, 1)
        for r in range(NSA_HPG):
            qp = nq_ref[0, :, r * LANES:(r + 1) * LANES].astype(F32)
            qa_ref[0, NSA_GROUPS * r + g] = jnp.where(data, _head_of_pair(qp, g), bias).astype(BF16)
        ksa_ref[0, g] = jnp.where(data, _head_of_pair(ks_ref[0].astype(F32), g), block_hot).astype(BF16)
        vsa_ref[0, g] = jnp.where(data, _head_of_pair(vs_ref[0].astype(F32), g), ones_lane).astype(BF16)
        kwa_ref[0, g] = jnp.where(data, _head_of_pair(kw_ref[0].astype(F32), g), 0.0).astype(BF16)
        vwa_ref[0, g] = jnp.where(data, _head_of_pair(vw_ref[0].astype(F32), g), ones_lane).astype(BF16)


def _augment_nsa(nq, sel, ks, vs, kw, vw, nbp):
    b, t, w = nq.shape
    tr = _pick(t, (512, 256, 128, 8))
    row = lambda width: pl.BlockSpec((1, tr, width), lambda bi, i: (bi, i, 0))
    out = lambda heads: pl.BlockSpec((1, heads, tr, LANES), lambda bi, i: (bi, 0, i, 0))
    shape = lambda heads: jax.ShapeDtypeStruct((b, heads, t, LANES), BF16)
    return pl.pallas_call(
        functools.partial(_augment_nsa_kernel, tr=tr, nbp=nbp),
        grid=(b, t // tr),
        in_specs=[row(w), row(NSA_GROUPS * nbp)] + [row(LANES)] * 4,
        out_specs=[out(NSA_HEADS)] + [out(NSA_GROUPS)] * 4,
        out_shape=[shape(NSA_HEADS)] + [shape(NSA_GROUPS)] * 4,
        compiler_params=_params(("parallel", "parallel"), 32),
        name="augment_nsa",
    )(nq, sel, ks, vs, kw, vw)


def _head_major(a, n_heads):
    b, t, _ = a.shape
    return a.reshape(b, t, n_heads, HEAD_DIM).transpose(0, 2, 1, 3)


def _augment(head_rows, extras):
    b, h, t, _ = head_rows.shape
    cols = [jnp.broadcast_to(e, (b, h, t))[..., None].astype(BF16) for e in extras]
    pad = jnp.zeros((b, h, t, LANES - HEAD_DIM - len(cols)), BF16)
    return jnp.concatenate([head_rows] + cols + [pad], axis=-1)


def _compress_rows(row_of_half, pa_ref, pb_ref, wa_ref, wb_ref, w2_ref, ns):
    u = jnp.concatenate([row_of_half(r) for r in range(CMP_STRIDE)], axis=1)
    first = jnp.dot((u + pa_ref[...]).astype(BF16), wa_ref[...], preferred_element_type=F32)
    second = jnp.dot((u + pb_ref[...]).astype(BF16), wb_ref[...], preferred_element_type=F32)
    pre = first + pltpu.roll(second, ns - 1, 0)
    h = (pre * jax.nn.sigmoid(pre)).astype(BF16)
    return jnp.dot(h, w2_ref[...], preferred_element_type=F32).astype(BF16)


def _compress_kernel(rows_ref, pa_ref, pb_ref, wa_ref, wb_ref, w2_ref, o_ref, *, ns):
    o_ref[0] = _compress_rows(lambda r: rows_ref[0, pl.ds(r, ns, stride=CMP_STRIDE), :],
                              pa_ref, pb_ref, wa_ref, wb_ref, w2_ref, ns)


def _compress_weights(pos, w1, w2):
    ratio = CMP_LEN // CMP_STRIDE
    assert ratio == 2
    w1r = w1.reshape(CMP_LEN, HEAD_DIM, CMP_HIDDEN)
    zeros = jnp.zeros((CMP_STRIDE, HEAD_DIM, CMP_HIDDEN), w1.dtype)

    def half(rows):
        g0 = jnp.concatenate([jnp.stack([rows, zeros], axis=1).reshape(-1, CMP_HIDDEN),
                              jnp.stack([zeros, rows], axis=1).reshape(-1, CMP_HIDDEN)], axis=1)
        return g0.astype(BF16)

    def pos_tab(p):
        return jnp.stack([p, p], axis=1).reshape(1, -1).astype(F32)

    z2 = jnp.zeros_like(w2)
    w2d = jnp.concatenate([jnp.concatenate([w2, z2], axis=1), jnp.concatenate([z2, w2], axis=1)], axis=0)
    return (pos_tab(pos[:CMP_STRIDE]), pos_tab(pos[CMP_STRIDE:]), half(w1r[:CMP_STRIDE]), half(w1r[CMP_STRIDE:]),
            w2d.astype(BF16))


def _compress(rows, col_block, weights, name):
    b, t, _ = rows.shape
    ns = t // CMP_STRIDE
    width = CMP_STRIDE * LANES
    pa, pb, wa, wb, w2d = weights
    return pl.pallas_call(
        functools.partial(_compress_kernel, ns=ns),
        grid=(b,),
        in_specs=[pl.BlockSpec((1, t, LANES), lambda i: (i, 0, col_block)), _const_spec((1, width)),
                  _const_spec((1, width)), _const_spec(wa.shape), _const_spec(wb.shape), _const_spec(w2d.shape)],
        out_specs=pl.BlockSpec((1, ns, LANES), lambda i: (i, 0, 0)),
        out_shape=jax.ShapeDtypeStruct((b, ns, LANES), BF16),
        compiler_params=_params(("parallel",), 48),
        name=name,
    )(rows, pa, pb, wa, wb, w2d)


def _cmp_select_kernel(q_ref, kc_ref, vc_ref, cov_ref, oc_ref, sel_ref, imp_ref, vt_ref, *,
                       sps, tq, rows_p, nc, n_cmp, n_slc, nbp, q_off, n_sel):
    qi = pl.program_id(1)
    lane = lax.broadcasted_iota(jnp.int32, (1, LANES), 1)
    left = lane < HEAD_DIM
    q_pos = q_off + qi * tq + lax.broadcasted_iota(jnp.int32, (tq, 1), 0)
    n_idx = lax.broadcasted_iota(jnp.int32, (1, nc), 1)
    cmask = ((n_idx * CMP_STRIDE + CMP_LEN - 1) <= q_pos) & (n_idx < n_cmp)
    cov = cov_ref[...]
    if rows_p > sps * tq:
        imp_ref[...] = jnp.zeros_like(imp_ref)

    def attend(sq, carry):
        kc = kc_ref[sq]
        vc = vc_ref[sq]
        imp = [jnp.zeros((tq, nbp), F32) for _ in range(NSA_GROUPS)]
        for r in range(N_PAIRS):
            qp = q_ref[sq, :, r * LANES:(r + 1) * LANES]
            halves = []
            for g in range(NSA_GROUPS):
                qh = jnp.where(left if g == 0 else jnp.logical_not(left), qp, jnp.zeros_like(qp))
                s = lax.dot_general(qh, kc, NT_DIMS, preferred_element_type=F32)
                s = jnp.where(cmask, s, NEG_INF)
                m = jnp.max(s, axis=1, keepdims=True)
                e = jnp.where(cmask, jnp.exp2(s - m), 0.0)
                pb = (e / jnp.maximum(jnp.sum(e, axis=1, keepdims=True), 1e-30)).astype(BF16)
                halves.append(jnp.dot(pb, vc, preferred_element_type=F32))
                imp[g] = imp[g] + jnp.dot(pb, cov, preferred_element_type=F32)
            oc_ref[sq, :, r * LANES:(r + 1) * LANES] = jnp.where(left, halves[0], halves[1])
        first_row = pl.multiple_of(sq * tq, tq)
        for g in range(NSA_GROUPS):
            imp_ref[g, pl.ds(first_row, tq), :] = imp[g]
        return carry

    if sps == 1:
        attend(0, 0)
    else:
        lax.fori_loop(0, sps, attend, 0)

    pos = q_off + qi * tq + lax.broadcasted_iota(jnp.int32, (rows_p, 1), 0) % tq
    blk = lax.broadcasted_iota(jnp.int32, (1, nbp), 1)
    cur = pos // SLC_LEN
    forced = (blk == 0) | (blk == cur) | (blk == cur - 1)
    valid = (blk * SLC_LEN <= pos) & (blk < n_slc)
    nbr = _round_up(n_slc, 8)
    row = lax.broadcasted_iota(jnp.int32, (nbr, rows_p), 0)
    n_live = jnp.minimum(n_slc, (q_off + (qi + 1) * tq - 1) // SLC_LEN + 1)
    for g in range(NSA_GROUPS):
        val = jnp.where(valid, jnp.where(forced, imp_ref[g] + FORCE_BONUS, imp_ref[g]), NEG_INF)
        vt_ref[...] = val.T

        def body(i, cnt):
            vi = vt_ref[pl.ds(i, 1), :]
            vt = vt_ref[0:nbr, :]
            ahead = (vi > vt) | ((vi == vt) & (i < row))
            return cnt + ahead.astype(F32)

        cnt = lax.fori_loop(0, n_live, body, jnp.zeros((nbr, rows_p), F32))
        bias = jnp.where(cnt < n_sel, 0.0, NEG_INF)
        if nbp > nbr:
            bias = jnp.concatenate([bias, jnp.full((nbp - nbr, rows_p), NEG_INF, F32)], axis=0)
        bias_t = bias.T
        for sq in range(sps):
            sel_ref[sq, :, g * nbp:(g + 1) * nbp] = bias_t[sq * tq:(sq + 1) * tq].astype(BF16)


def _coverage(n_cmp, nc, n_slc, nbp):
    c0 = np.arange(nc) * CMP_STRIDE
    s0 = np.arange(nbp) * SLC_LEN
    lo = np.maximum(c0[:, None], s0[None, :])
    hi = np.minimum(c0[:, None] + CMP_LEN, s0[None, :] + SLC_LEN)
    cov = np.maximum(hi - lo, 0).astype(np.float32) / CMP_LEN
    cov[n_cmp:, :] = 0.0
    cov[:, n_slc:] = 0.0
    return jnp.asarray(cov, dtype=BF16)


def _cmp_select(q, kc, vc, *, tq, sps, n_cmp, n_slc, nbp, q_off, name):
    b, t_q, _ = q.shape
    nc = kc.shape[1]
    rows_p = _round_up(sps * tq, LANES)
    cov = _coverage(n_cmp, nc, n_slc, nbp)
    kern = functools.partial(_cmp_select_kernel, sps=sps, tq=tq, rows_p=rows_p, nc=nc, n_cmp=n_cmp, n_slc=n_slc,
                             nbp=nbp, q_off=q_off, n_sel=min(N_SELECT, n_slc))
    return pl.pallas_call(
        kern,
        grid=(b // sps, t_q // tq),
        in_specs=[pl.BlockSpec((sps, tq, N_PAIRS * LANES), lambda bi, qi: (bi, qi, 0)),
                  pl.BlockSpec((sps, nc, LANES), lambda bi, qi: (bi, 0, 0)),
                  pl.BlockSpec((sps, nc, LANES), lambda bi, qi: (bi, 0, 0)),
                  _const_spec((nc, nbp))],
        out_specs=[pl.BlockSpec((sps, tq, N_PAIRS * LANES), lambda bi, qi: (bi, qi, 0)),
                   pl.BlockSpec((sps, tq, NSA_GROUPS * nbp), lambda bi, qi: (bi, qi, 0))],
        out_shape=[jax.ShapeDtypeStruct((b, t_q, N_PAIRS * LANES), F32),
                   jax.ShapeDtypeStruct((b, t_q, NSA_GROUPS * nbp), BF16)],
        scratch_shapes=[pltpu.VMEM((NSA_GROUPS, rows_p, nbp), F32), pltpu.VMEM((nbp, rows_p), F32)],
        compiler_params=_params(("parallel", "parallel"), 40),
        name=name,
    )(q, kc, vc, cov)


def _merge_kernel(x_ref, fo_ref, oc_ref, os_ref, ow_ref, ng_ref, gf_ref, gn_ref, wuf_ref, wun_ref, wout_ref,
                  eg_ref, g_ref, b_ref, o_ref, *, alpha):
    gates = jax.nn.sigmoid(ng_ref[...])
    gx = lax.dot_general(gates, eg_ref[...], (((1,), (0,)), ((), ())), precision=HIGHEST, preferred_element_type=F32)
    w = NSA_WIDTH
    nsa_o = gx[:, :w] * oc_ref[...] + gx[:, w:2 * w] * os_ref[...] + gx[:, 2 * w:] * ow_ref[...]
    up_f = jnp.dot(fo_ref[...].astype(BF16), wuf_ref[...], preferred_element_type=F32)
    up_n = jnp.dot(nsa_o.astype(BF16), wun_ref[...], preferred_element_type=F32)
    mixed = jax.nn.sigmoid(gf_ref[...]) * up_f + jax.nn.sigmoid(gn_ref[...]) * up_n
    mix = jnp.dot(mixed.astype(BF16), wout_ref[...], preferred_element_type=F32)
    o_ref[...] = _layer_norm(alpha * x_ref[...] + mix, g_ref[...], b_ref[...])


def _gate_expand():
    perm = _nsa_perm()
    e = np.zeros((LANES, N_NSA_BRANCHES * NSA_WIDTH), np.float32)
    for br in range(N_NSA_BRANCHES):
        for pos in range(NSA_WIDTH):
            head = perm[pos] // HEAD_DIM
            e[br * NSA_HEADS + head, br * NSA_WIDTH + pos] = 1.0
    return jnp.asarray(e)


def _merge_ln(x, fox_o, o_c, o_s, o_w, ng, gf, gn, w_up_fox, w_up_nsa, w_out, g, b, alpha, name):
    n, d = x.shape
    tm = _pick(n, (256, 128, 8))
    row = lambda w: pl.BlockSpec((tm, w), lambda i: (i, 0))
    wuf = w_up_fox.astype(BF16)
    wun = w_up_nsa[_nsa_perm(), :].astype(BF16)
    wout = w_out.astype(BF16)
    eg = _gate_expand()
    return pl.pallas_call(
        functools.partial(_merge_kernel, alpha=alpha),
        grid=(n // tm,),
        in_specs=[row(d), row(FOX_WIDTH), row(NSA_WIDTH), row(NSA_WIDTH), row(NSA_WIDTH), row(LANES), row(d), row(d),
                  _const_spec(wuf.shape), _const_spec(wun.shape), _const_spec(wout.shape), _const_spec(eg.shape),
                  _const_spec((1, d)), _const_spec((1, d))],
        out_specs=row(d),
        out_shape=jax.ShapeDtypeStruct((n, d), F32),
        compiler_params=_params(("parallel",), 48),
        name=name,
    )(x, fox_o, o_c, o_s, o_w, ng, gf, gn, wuf, wun, wout, eg, g.reshape(1, d), b.reshape(1, d))


def _page_specs(block, group, n_pages, second=0):
    nd = len(block)

    def spec(slot):
        return pl.BlockSpec(block, lambda si, j, pt: (pt[si * n_pages + jnp.minimum(j * group + slot, n_pages - 1)],
                                                      second) + (0,) * (nd - 2))

    return [spec(slot) for slot in range(group)]


def _lf_pages_kernel(pt_ref, *refs, group):
    o_ref = refs[group]
    for i in range(group):
        o_ref[0, :, i * PAGE_SIZE:(i + 1) * PAGE_SIZE] = refs[i][0]


def _lf_pages(page_table, cache_lf_t):
    s, n_pages = page_table.shape
    group = _pick(n_pages, (64, 32, 16, 8, 4, 2, 1))
    grid_spec = pltpu.PrefetchScalarGridSpec(
        num_scalar_prefetch=1, grid=(s, n_pages // group),
        in_specs=_page_specs((1, FOX_HEADS, PAGE_SIZE), group, n_pages),
        out_specs=pl.BlockSpec((1, FOX_HEADS, group * PAGE_SIZE), lambda si, j, pt: (si, 0, j)))
    return pl.pallas_call(
        functools.partial(_lf_pages_kernel, group=group),
        grid_spec=grid_spec,
        out_shape=jax.ShapeDtypeStruct((s, FOX_HEADS, n_pages * PAGE_SIZE), F32),
        compiler_params=_params(("parallel", "arbitrary"), 32),
        name="lf_pages",
    )(page_table.reshape(-1), *([cache_lf_t] * group))


def _fox_sample_kernel(pt_ref, q_ref, cc_ref, cr_ref, crn_ref, kn_ref, vn_ref, *refs, group, n_steps, n_new):
    pages = refs[:group]
    o_ref, m_ref, l_ref, acc_ref = refs[group:]
    j = pl.program_id(1)

    @pl.when(j == 0)
    def _():
        m_ref[...] = jnp.full_like(m_ref, M_INIT)
        l_ref[...] = jnp.zeros_like(l_ref)
        acc_ref[...] = jnp.zeros_like(acc_ref)

    def update(s, value_dots):
        m_prev = m_ref[...]
        m_new = jnp.maximum(m_prev, jnp.max(s, axis=1, keepdims=True))
        p = jnp.exp2(s - m_new)
        alpha = jnp.exp2(m_prev - m_new)
        l_ref[...] = alpha * l_ref[...] + jnp.sum(p, axis=1, keepdims=True)
        acc_ref[...] = alpha * acc_ref[...] + jnp.concatenate(
            [value_dots(h, p[h * n_new:(h + 1) * n_new].astype(BF16)) for h in range(FOX_HEADS)], axis=0)
        m_ref[...] = m_new

    def tiles(kv, h):
        return jnp.concatenate([pages[i][0, kv, h].astype(BF16) for i in range(group)], axis=1)

    scores = [jnp.dot(q_ref[0, h], tiles(0, h), preferred_element_type=F32) + (cc_ref[0, h] - cr_ref[0, h:h + 1, :])
              for h in range(FOX_HEADS)]
    update(jnp.concatenate(scores, axis=0),
           lambda h, pb: lax.dot_general(pb, tiles(1, h), NT_DIMS, preferred_element_type=F32))

    @pl.when(j == n_steps - 1)
    def _():
        t_q = lax.broadcasted_iota(jnp.int32, (n_new, n_new), 0)
        t_k = lax.broadcasted_iota(jnp.int32, (n_new, n_new), 1)
        new_scores = []
        for h in range(FOX_HEADS):
            s = lax.dot_general(q_ref[0, h], kn_ref[0, h], NT_DIMS, preferred_element_type=F32)
            new_scores.append(jnp.where(t_k <= t_q, s + (cc_ref[0, h] - crn_ref[0, h:h + 1, :]), NEG_INF))
        update(jnp.concatenate(new_scores, axis=0),
               lambda h, pb: jnp.dot(pb, vn_ref[0, h], preferred_element_type=F32))
        o_ref[0] = acc_ref[...] / jnp.maximum(l_ref[...], 1e-30)


def _fox_sample(page_table, cache_kv_t, q_h, k_new_h, v_new_h, c_q, c_row, c_new):
    s, n_pages = page_table.shape
    n_new = q_h.shape[2]
    group = _pick(n_pages, (16, 8, 4, 2, 1))
    n_steps = n_pages // group
    rows = FOX_HEADS * n_new
    seq4 = lambda shape: pl.BlockSpec(shape, lambda si, j, pt: (si, 0, 0, 0))
    grid_spec = pltpu.PrefetchScalarGridSpec(
        num_scalar_prefetch=1, grid=(s, n_steps),
        in_specs=[seq4((1, FOX_HEADS, n_new, HEAD_DIM)), seq4((1, FOX_HEADS, n_new, 1)),
                  pl.BlockSpec((1, FOX_HEADS, group * PAGE_SIZE), lambda si, j, pt: (si, 0, j)),
                  pl.BlockSpec((1, FOX_HEADS, n_new), lambda si, j, pt: (si, 0, 0)),
                  seq4((1, FOX_HEADS, n_new, HEAD_DIM)), seq4((1, FOX_HEADS, n_new, HEAD_DIM))]
        + _page_specs((1, 2, FOX_HEADS, HEAD_DIM, PAGE_SIZE), group, n_pages),
        out_specs=pl.BlockSpec((1, rows, HEAD_DIM), lambda si, j, pt: (si, 0, 0)),
        scratch_shapes=[pltpu.VMEM((rows, 1), F32), pltpu.VMEM((rows, 1), F32), pltpu.VMEM((rows, HEAD_DIM), F32)])
    out = pl.pallas_call(
        functools.partial(_fox_sample_kernel, group=group, n_steps=n_steps, n_new=n_new),
        grid_spec=grid_spec,
        out_shape=jax.ShapeDtypeStruct((s, rows, HEAD_DIM), F32),
        compiler_params=_params(("parallel", "arbitrary"), 48),
        name="fox_sample",
    )(page_table.reshape(-1), q_h, c_q, c_row, c_new, k_new_h, v_new_h, *([cache_kv_t] * group))
    return out.reshape(s, FOX_HEADS, n_new, HEAD_DIM)


def _to_rows(tile):
    g, d, n = tile.shape
    return tile.reshape(g * d, n).T


def _compress_pages_kernel(pt_ref, pak, pbk, wak, wbk, w2k, pav, pbv, wav, wbv, w2v, *refs, group, n_steps, ns):
    pages = refs[:group]
    kc_o, vc_o, rk_ref, rv_ref = refs[group:]
    j = pl.program_id(1)
    for i in range(group):
        first = pl.multiple_of((j * group + i) * PAGE_SIZE, PAGE_SIZE)
        rk_ref[pl.ds(first, PAGE_SIZE), :] = _to_rows(pages[i][0, 0])
        rv_ref[pl.ds(first, PAGE_SIZE), :] = _to_rows(pages[i][0, 1])

    @pl.when(j == n_steps - 1)
    def _():
        half_rows = lambda ref: (lambda r: ref[pl.ds(r, ns, stride=CMP_STRIDE), :])
        kc_o[0] = _compress_rows(half_rows(rk_ref), pak, pbk, wak, wbk, w2k, ns)
        vc_o[0] = _compress_rows(half_rows(rv_ref), pav, pbv, wav, wbv, w2v, ns)


def _compress_pages(page_table, cache_kv_t, cmp_k, cmp_v):
    s, n_pages = page_table.shape
    group = _pick(n_pages, (16, 8, 4, 2, 1))
    n_steps = n_pages // group
    past = n_pages * PAGE_SIZE
    ns = past // CMP_STRIDE
    consts = list(cmp_k) + list(cmp_v)
    out = pl.BlockSpec((1, ns, LANES), lambda si, j, pt: (si, 0, 0))
    grid_spec = pltpu.PrefetchScalarGridSpec(
        num_scalar_prefetch=1, grid=(s, n_steps),
        in_specs=[_const_spec(c.shape) for c in consts]
        + _page_specs((1, 2, NSA_GROUPS, HEAD_DIM, PAGE_SIZE), group, n_pages, second=0),
        out_specs=[out, out],
        scratch_shapes=[pltpu.VMEM((past, LANES), F32), pltpu.VMEM((past, LANES), F32)])
    return pl.pallas_call(
        functools.partial(_compress_pages_kernel, group=group, n_steps=n_steps, ns=ns),
        grid_spec=grid_spec,
        out_shape=[jax.ShapeDtypeStruct((s, ns, LANES), BF16), jax.ShapeDtypeStruct((s, ns, LANES), BF16)],
        compiler_params=_params(("parallel", "arbitrary"), 48),
        name="compress_pages",
    )(page_table.reshape(-1), *consts, *([cache_kv_t] * group))


def _slc_sample_kernel(pt_ref, q_ref, sel_ref, kn_ref, vn_ref, *refs, group, n_steps, n_new, nbp, n_slc):
    pages = refs[:group]
    o_ref, m_ref, l_ref, acc_ref = refs[group:]
    j = pl.program_id(1)
    n_keys = group * PAGE_SIZE
    rows = NSA_GROUPS * NSA_HPG * n_new

    @pl.when(j == 0)
    def _():
        m_ref[...] = jnp.full_like(m_ref, M_INIT)
        l_ref[...] = jnp.zeros_like(l_ref)
        acc_ref[...] = jnp.zeros_like(acc_ref)

    def group_rows(per_query):
        return jnp.concatenate([per_query[g] for g in range(NSA_GROUPS) for _ in range(NSA_HPG)], axis=0)

    def update(s, values):
        m_prev = m_ref[...]
        m_new = jnp.maximum(m_prev, jnp.max(s, axis=1, keepdims=True))
        p = jnp.exp2(s - m_new)
        alpha = jnp.exp2(m_prev - m_new)
        l_ref[...] = alpha * l_ref[...] + jnp.sum(p, axis=1, keepdims=True)
        acc_ref[...] = alpha * acc_ref[...] + values(p.astype(BF16))
        m_ref[...] = m_new

    def tiles(c):
        return jnp.concatenate([pages[i][0, c].reshape(NSA_KV_WIDTH, PAGE_SIZE).astype(BF16) for i in range(group)],
                               axis=1)

    q = q_ref[0]
    blk_row = lax.broadcasted_iota(jnp.int32, (nbp, n_keys), 0)
    blk_key = (j * n_keys + lax.broadcasted_iota(jnp.int32, (nbp, n_keys), 1)) // SLC_LEN
    expand = (blk_row == blk_key).astype(BF16)
    bias = [jnp.dot(sel_ref[0, :, g * nbp:(g + 1) * nbp], expand, preferred_element_type=F32)
            for g in range(NSA_GROUPS)]
    update(jnp.dot(q, tiles(0), preferred_element_type=F32) + group_rows(bias),
           lambda pb: lax.dot_general(pb, tiles(1), NT_DIMS, preferred_element_type=F32))

    @pl.when(j == n_steps - 1)
    def _():
        t_q = lax.broadcasted_iota(jnp.int32, (rows, n_new), 0) % n_new
        t_k = lax.broadcasted_iota(jnp.int32, (rows, n_new), 1)
        last_blk = [sel_ref[0, :, g * nbp + n_slc - 1:g * nbp + n_slc].astype(F32) for g in range(NSA_GROUPS)]
        s = lax.dot_general(q, kn_ref[0], NT_DIMS, preferred_element_type=F32) + group_rows(last_blk)
        update(jnp.where(t_k <= t_q, s, NEG_INF), lambda pb: jnp.dot(pb, vn_ref[0], preferred_element_type=F32))
        o_ref[0] = acc_ref[...] / jnp.maximum(l_ref[...], 1e-30)


def _slc_sample(page_table, cache_kv_t, q_rows, sel, ks_new, vs_new, *, nbp, n_slc):
    s, n_pages = page_table.shape
    n_new = ks_new.shape[1]
    rows = q_rows.shape[1]
    group = _pick(n_pages, (16, 8, 4, 2, 1))
    n_steps = n_pages // group
    seq = lambda shape: pl.BlockSpec(shape, lambda si, j, pt: (si, 0, 0))
    grid_spec = pltpu.PrefetchScalarGridSpec(
        num_scalar_prefetch=1, grid=(s, n_steps),
        in_specs=[seq((1, rows, LANES)), seq((1, n_new, NSA_GROUPS * nbp)), seq((1, n_new, LANES)),
                  seq((1, n_new, LANES))]
        + _page_specs((1, 2, NSA_GROUPS, HEAD_DIM, PAGE_SIZE), group, n_pages, second=1),
        out_specs=seq((1, rows, LANES)),
        scratch_shapes=[pltpu.VMEM((rows, 1), F32), pltpu.VMEM((rows, 1), F32), pltpu.VMEM((rows, LANES), F32)])
    return pl.pallas_call(
        functools.partial(_slc_sample_kernel, group=group, n_steps=n_steps, n_new=n_new, nbp=nbp, n_slc=n_slc),
        grid_spec=grid_spec,
        out_shape=jax.ShapeDtypeStruct((s, rows, LANES), F32),
        compiler_params=_params(("parallel", "arbitrary"), 40),
        name="slc_sample",
    )(page_table.reshape(-1), q_rows, sel, ks_new, vs_new, *([cache_kv_t] * group))


def _win_rows_kernel(st_ref, kwnew_ref, vwnew_ref, kw_o, vw_o, *, keep, n_new):
    for c, new_ref, o_ref in ((0, kwnew_ref, kw_o), (1, vwnew_ref, vw_o)):
        o_ref[0] = jnp.zeros(o_ref.shape[1:], BF16)
        o_ref[0, 0:keep, :] = _to_rows(st_ref[0, c]).astype(BF16)
        o_ref[0, keep:keep + n_new, :] = new_ref[0]


def _win_rows(state_t, kw_new, vw_new, t_all):
    s, keep = state_t.shape[0], state_t.shape[-1]
    n_new = kw_new.shape[1]
    seq = lambda si: (si, 0, 0)
    return pl.pallas_call(
        functools.partial(_win_rows_kernel, keep=keep, n_new=n_new),
        grid=(s,),
        in_specs=[pl.BlockSpec((1, 2, NSA_GROUPS, HEAD_DIM, keep), lambda si: (si, 0, 0, 0, 0)),
                  pl.BlockSpec((1, n_new, LANES), seq), pl.BlockSpec((1, n_new, LANES), seq)],
        out_specs=[pl.BlockSpec((1, t_all, LANES), seq), pl.BlockSpec((1, t_all, LANES), seq)],
        out_shape=[jax.ShapeDtypeStruct((s, t_all, LANES), BF16), jax.ShapeDtypeStruct((s, t_all, LANES), BF16)],
        compiler_params=_params(("parallel",), 32),
        name="win_rows",
    )(state_t, kw_new, vw_new)


def _rows_minor(a):
    return jnp.moveaxis(a, 1, -1)


def kernel(x_prompt, x_sample, cache_fox_kv, cache_fox_logf, cache_nsa_kv, state_win_kv, page_table, ln1_g, ln1_b, ffn1_w_up, ffn1_w_down, w_in, b_fgate, cmp_pos_k, cmp_wk1, cmp_wk2, cmp_pos_v, cmp_wv1, cmp_wv2, w_up_fox, w_up_nsa, w_out, ln2_g, ln2_b, ffn2_w_up, ffn2_w_down, ln3_g, ln3_b):
    depth = ln1_g.shape[0]
    assert depth == 1, "single-layer step"
    alpha = (2.0 * depth) ** 0.25
    bsz, seq, d = x_prompt.shape
    sb, n_new, _ = x_sample.shape
    n_pages = page_table.shape[1]
    past = n_pages * PAGE_SIZE
    keep = state_win_kv.shape[2]
    assert seq % 256 == 0 and past % SLC_LEN == 0 and n_new < CMP_STRIDE and keep == WINDOW
    layer = 0

    ws, bf = _split_w_in(w_in[layer], b_fgate[layer])
    cmp_k = _compress_weights(cmp_pos_k[layer], cmp_wk1[layer], cmp_wk2[layer])
    cmp_v = _compress_weights(cmp_pos_v[layer], cmp_wv1[layer], cmp_wv2[layer])

    n = bsz * seq
    xp = _ffn_ln(x_prompt.reshape(n, d), ffn1_w_up[layer], ffn1_w_down[layer], ln1_g[layer], ln1_b[layer], alpha,
                 "ffn1_prompt")
    cos_p, sin_p = _rope_tables(jnp.arange(seq, dtype=jnp.int32))
    mp = _in_proj(xp, ws, bf, cos_p, sin_p, "in_proj_prompt")
    r3 = lambda a: a.reshape(bsz, seq, a.shape[-1])

    c_p = _cumsum_time(r3(mp["lf"]).transpose(0, 2, 1), "cumsum_prompt")
    tq = _pick(seq, (512, 256))
    tk = _pick(seq, (512, 256))
    q_fox, k_fox, v_fox = _augment_fox(r3(mp["fq"]), r3(mp["fk"]), r3(mp["fv"]), c_p.transpose(0, 2, 1))
    fox_o = _attention("fox", q_fox, k_fox, v_fox, tq=tq, tk=tk, q_off=0, k_off=0, name="fox_prompt")

    nkv3 = r3(mp["nkv"])
    kc_p = _compress(nkv3, 0, cmp_k, "compress_k_prompt")
    vc_p = _compress(nkv3, 1, cmp_v, "compress_v_prompt")
    n_cmp_p = seq // CMP_STRIDE - 1
    n_slc_p = -(-seq // SLC_LEN)
    nbp_p = _round_up(n_slc_p, LANES)
    nq3 = r3(mp["nq"])
    oc_p, sel_p = _cmp_select(nq3, kc_p, vc_p, tq=256, sps=1, n_cmp=n_cmp_p, n_slc=n_slc_p, nbp=nbp_p, q_off=0,
                              name="cmp_select_prompt")
    assert n_slc_p <= LANES - HEAD_DIM
    q_nsa, k_slc, v_slc, k_win, v_win = _augment_nsa(nq3, sel_p, r3(mp["ks"]), r3(mp["vs"]), r3(mp["kw"]),
                                                      r3(mp["vw"]), nbp_p)
    os_p = _attention("slc", q_nsa, k_slc, v_slc, tq=tq, tk=tk, q_off=0, k_off=0, name="slc_prompt")
    ow_p = _attention("win", q_nsa, k_win, v_win, tq=256, tk=256, q_off=0, k_off=0, name="win_prompt")
    flat = lambda a: a.reshape(n, a.shape[-1])
    xp2 = _merge_ln(xp, flat(fox_o), flat(oc_p), flat(os_p), flat(ow_p), mp["ng"], mp["gf"], mp["gn"],
                    w_up_fox[layer], w_up_nsa[layer], w_out[layer], ln2_g[layer], ln2_b[layer], alpha, "merge_prompt")
    yp = _ffn_ln(xp2, ffn2_w_up[layer], ffn2_w_down[layer], ln3_g[layer], ln3_b[layer], alpha, "ffn2_prompt")

    ns_rows = sb * n_new
    xs = _ffn_ln(x_sample.reshape(ns_rows, d), ffn1_w_up[layer], ffn1_w_down[layer], ln1_g[layer], ln1_b[layer],
                 alpha, "ffn1_sample")
    cos_s, sin_s = _rope_tables(past + jnp.arange(n_new, dtype=jnp.int32))
    tile_rows = _pick(ns_rows, (256, 128, 8))
    reps = tile_rows // n_new
    ms = _in_proj(xs, ws, bf, jnp.tile(cos_s, (reps, 1)), jnp.tile(sin_s, (reps, 1)), "in_proj_sample")
    s3 = lambda a: a.reshape(sb, n_new, a.shape[-1])
    by_head = lambda a: s3(a).reshape(sb, n_new, FOX_HEADS, HEAD_DIM).transpose(0, 2, 1, 3)

    lf_past = _lf_pages(page_table, _rows_minor(cache_fox_logf[layer]))
    lf_new = s3(ms["lf"]).transpose(0, 2, 1)
    t_cs = _round_up(past + n_new, 8 * LANES)
    c_s = _cumsum_time(jnp.pad(jnp.concatenate([lf_past, lf_new], axis=2), ((0, 0), (0, 0), (0, t_cs - past - n_new))),
                       "cumsum_sample") * LOG2E
    c_new = c_s[:, :, past:past + n_new]
    fox_o_s = _fox_sample(page_table, _rows_minor(cache_fox_kv[layer]), by_head(ms["fq"]), by_head(ms["fk"]),
                          by_head(ms["fv"]), c_new[..., None], c_s, c_new)
    fox_o_s = fox_o_s.transpose(0, 2, 1, 3).reshape(ns_rows, FOX_WIDTH)

    nsa_t = _rows_minor(cache_nsa_kv[layer])
    kc_s, vc_s = _compress_pages(page_table, nsa_t, cmp_k, cmp_v)
    n_cmp_s = (past + n_new) // CMP_STRIDE - 1
    n_slc_s = -(-(past + n_new) // SLC_LEN)
    nbp_s = _round_up(n_slc_s, LANES)
    nq_s = s3(ms["nq"])
    sps = _pick(sb, tuple(c for c in (16, 8, 4, 2, 1) if c * n_new <= LANES))
    oc_s, sel_s = _cmp_select(nq_s, kc_s, vc_s, tq=n_new, sps=sps, n_cmp=n_cmp_s, n_slc=n_slc_s, nbp=nbp_s,
                              q_off=past, name="cmp_select_sample")
    q_ghtd = nq_s.reshape(sb, n_new, NSA_HPG, NSA_GROUPS, HEAD_DIM).transpose(0, 3, 2, 1, 4)
    zero = jnp.zeros_like(q_ghtd[:, 0])
    q_rows = jnp.concatenate([jnp.concatenate([q_ghtd[:, 0], zero], axis=-1),
                              jnp.concatenate([zero, q_ghtd[:, 1]], axis=-1)], axis=1).reshape(sb, -1, LANES)
    o_rows = _slc_sample(page_table, nsa_t, q_rows, sel_s, s3(ms["ks"]), s3(ms["vs"]), nbp=nbp_s, n_slc=n_slc_s)
    o_rows = o_rows.reshape(sb, NSA_GROUPS, NSA_HPG, n_new, LANES)
    os_s = jnp.stack([o_rows[:, 0, :, :, :HEAD_DIM], o_rows[:, 1, :, :, HEAD_DIM:]], axis=3)
    os_s = os_s.transpose(0, 2, 1, 3, 4).reshape(sb, n_new, NSA_WIDTH)
    t_win = _round_up(keep + n_new, LANES)
    kw_all, vw_all = _win_rows(_rows_minor(state_win_kv[layer]), s3(ms["kw"]), s3(ms["vw"]), t_win)
    ow_s = _attention("win", _augment(_head_major(nq_s, NSA_HEADS), []),
                      _augment(_head_major(kw_all, NSA_GROUPS), []), _augment(_head_major(vw_all, NSA_GROUPS), [1.0]),
                      tq=n_new, tk=t_win, q_off=past, k_off=past - keep, name="win_sample")
    flat_s = lambda a: a.reshape(ns_rows, a.shape[-1])
    xs2 = _merge_ln(xs, fox_o_s, flat_s(oc_s), flat_s(os_s), flat_s(ow_s), ms["ng"], ms["gf"], ms["gn"],
                    w_up_fox[layer], w_up_nsa[layer], w_out[layer], ln2_g[layer], ln2_b[layer], alpha, "merge_sample")
    ys = _ffn_ln(xs2, ffn2_w_up[layer], ffn2_w_down[layer], ln3_g[layer], ln3_b[layer], alpha, "ffn2_sample")

    fox_kv_p = mp["fkv"].reshape(1, bsz, seq, 2, FOX_HEADS, HEAD_DIM)
    fox_kv_s = ms["fkv"].reshape(1, sb, n_new, 2, FOX_HEADS, HEAD_DIM)
    logf_p = mp["lf"].reshape(1, bsz, seq, FOX_HEADS)
    logf_s = ms["lf"].reshape(1, sb, n_new, FOX_HEADS)
    nsa_kv_p = mp["nkv"].reshape(1, bsz, seq, 4, NSA_GROUPS, HEAD_DIM)
    nsa_kv_s = ms["nkv"].reshape(1, sb, n_new, 4, NSA_GROUPS, HEAD_DIM)
    win_rows_p = r3(mp["win"]).reshape(bsz, seq, 2, NSA_GROUPS, HEAD_DIM)
    if seq >= keep:
        win_p = win_rows_p[:, seq - keep:]
    else:
        win_p = jnp.pad(win_rows_p, ((0, 0), (keep - seq, 0), (0, 0), (0, 0), (0, 0)))
    new_win = ms["win"].reshape(sb, n_new, 2, NSA_GROUPS, HEAD_DIM).astype(state_win_kv.dtype)
    win_s = jnp.concatenate([state_win_kv[layer], new_win], axis=1)[:, -keep:]
    return (yp.reshape(bsz, seq, d), ys.reshape(sb, n_new, d), fox_kv_p, fox_kv_s, logf_p, logf_s,
            nsa_kv_p, nsa_kv_s, win_p[None], win_s[None])
```

```python
import functools

import numpy as np
import jax
import jax.numpy as jnp
from jax import lax
from jax.experimental import pallas as pl
from jax.experimental.pallas import tpu as pltpu

F32 = jnp.float32
BF16 = jnp.bfloat16

HEAD_DIM = 64
FOX_HEADS = 8
NSA_HEADS = 8
NSA_GROUPS = 2
NSA_HPG = NSA_HEADS // NSA_GROUPS
FOX_WIDTH = FOX_HEADS * HEAD_DIM
NSA_WIDTH = NSA_HEADS * HEAD_DIM
NSA_KV_WIDTH = NSA_GROUPS * HEAD_DIM
N_NSA_BRANCHES = 3
CMP_LEN = 32
CMP_STRIDE = 16
CMP_HIDDEN = 128
SLC_LEN = 64
N_SELECT = 16
WINDOW = 512
PAGE_SIZE = 128
ROPE_THETA = 10000.0
LN_EPS = 1e-5
FORCE_BONUS = 1e4
NEG_INF = -1e30
M_INIT = 0.1 * NEG_INF
LOG2E = 1.4426950408889634
QSCALE = HEAD_DIM ** -0.5 * LOG2E
SUM_LANE = HEAD_DIM

LANES = 128
N_PAIRS = 4
MIB = 1024 * 1024
HIGHEST = lax.Precision.HIGHEST
NT_DIMS = (((1,), (1,)), ((), ()))


def _params(semantics, vmem_mib):
    return pltpu.CompilerParams(dimension_semantics=semantics, vmem_limit_bytes=vmem_mib * MIB)


def _pick(n, candidates):
    for c in candidates:
        if n % c == 0:
            return c
    return n


def _round_up(n, m):
    return (n + m - 1) // m * m


def _layer_norm(y, g, b):
    mu = jnp.mean(y, axis=-1, keepdims=True)
    d = y - mu
    var = jnp.mean(d * d, axis=-1, keepdims=True)
    return d * lax.rsqrt(var + LN_EPS) * g + b


def _const_spec(shape):
    nd = len(shape)
    return pl.BlockSpec(shape, lambda *_: (0,) * nd, pipeline_mode=pl.Buffered(1))


def _ffn_ln_kernel(x_ref, wa_ref, wb_ref, wd_ref, g_ref, b_ref, o_ref, acc_ref, *, alpha, n_chunks):
    x = x_ref[...]
    xb = x.astype(BF16)
    acc_ref[...] = jnp.zeros_like(acc_ref)

    def body(c, carry):
        a = jnp.dot(xb, wa_ref[c], preferred_element_type=F32)
        b = jnp.dot(xb, wb_ref[c], preferred_element_type=F32)
        h = (a * jax.nn.sigmoid(a) * b).astype(BF16)
        acc_ref[...] += jnp.dot(h, wd_ref[c], preferred_element_type=F32)
        return carry

    lax.fori_loop(0, n_chunks, body, 0)
    o_ref[...] = _layer_norm(alpha * x + 0.5 * acc_ref[...], g_ref[...], b_ref[...])


def _ffn_ln(x, w_up, w_down, g, b, alpha, name):
    n, d = x.shape
    f = w_down.shape[0]
    fc = _pick(f, (256, 128))
    nc = f // fc
    wa = w_up[:, :f].astype(BF16).reshape(d, nc, fc).transpose(1, 0, 2)
    wb = w_up[:, f:].astype(BF16).reshape(d, nc, fc).transpose(1, 0, 2)
    wd = w_down.astype(BF16).reshape(nc, fc, d)
    tm = _pick(n, (1024, 512, 256, 128, 8))
    return pl.pallas_call(
        functools.partial(_ffn_ln_kernel, alpha=alpha, n_chunks=nc),
        grid=(n // tm,),
        in_specs=[pl.BlockSpec((tm, d), lambda i: (i, 0)),
                  _const_spec((nc, d, fc)), _const_spec((nc, d, fc)), _const_spec((nc, fc, d)),
                  _const_spec((1, d)), _const_spec((1, d))],
        out_specs=pl.BlockSpec((tm, d), lambda i: (i, 0)),
        out_shape=jax.ShapeDtypeStruct((n, d), F32),
        scratch_shapes=[pltpu.VMEM((tm, d), F32)],
        compiler_params=_params(("parallel",), 56),
        name=name,
    )(x, wa, wb, wd, g.reshape(1, d), b.reshape(1, d))


def _rope_tables(pos):
    half = HEAD_DIM // 2
    inv_freq = ROPE_THETA ** (-jnp.arange(half, dtype=F32) / half)
    ang = pos.astype(F32)[:, None] * inv_freq[None, :]
    cos, sin = jnp.cos(ang), jnp.sin(ang)
    cos64 = jnp.concatenate([cos, cos], axis=-1)
    sin64 = jnp.concatenate([-sin, sin], axis=-1)
    return jnp.tile(cos64, (1, LANES // HEAD_DIM)), jnp.tile(sin64, (1, LANES // HEAD_DIM))


def _in_proj_kernel(x_ref, wfq, wfkv, wsm, wnq, wnkv, wwin, wgf, wgn, bf_ref, cos_ref, sin_ref,
                    fq_o, fkv_o, fk_o, fv_o, lf_o, nq_o, nkv_o, ks_o, vs_o, win_o, kw_o, vw_o,
                    ng_o, gf_o, gn_o):
    xb = x_ref[...].astype(BF16)
    cos = cos_ref[...]
    sin = sin_ref[...]
    lane = lax.broadcasted_iota(jnp.int32, (1, LANES), 1)
    low_half = (lane % HEAD_DIM) < (HEAD_DIM // 2)

    def rope(v):
        partner = jnp.where(low_half, pltpu.roll(v, LANES - HEAD_DIM // 2, 1), pltpu.roll(v, HEAD_DIM // 2, 1))
        return v * cos + partner * sin

    def proj(w_ref):
        return jnp.dot(xb, w_ref[...], preferred_element_type=F32)

    fq_o[...] = (proj(wfq) * QSCALE).astype(BF16)
    fkv = proj(wfkv)
    fkv_o[...] = fkv
    fk_o[...] = fkv[:, :FOX_WIDTH].astype(BF16)
    fv_o[...] = fkv[:, FOX_WIDTH:].astype(BF16)

    sm = proj(wsm)
    z = sm[:, :LANES] + bf_ref[...]
    logf = jnp.minimum(z, 0.0) - jnp.log1p(jnp.exp(-jnp.abs(z)))
    lf_o[...] = logf[:, :FOX_HEADS]
    ng_o[...] = sm[:, LANES:]

    nq = proj(wnq)
    for r in range(N_PAIRS):
        nq_o[:, r * LANES:(r + 1) * LANES] = (rope(nq[:, r * LANES:(r + 1) * LANES]) * QSCALE).astype(BF16)

    nkv = proj(wnkv)
    k_cmp = rope(nkv[:, 0:LANES])
    k_slc = rope(nkv[:, 2 * LANES:3 * LANES])
    v_slc = nkv[:, 3 * LANES:4 * LANES]
    nkv_o[:, 0:LANES] = k_cmp
    nkv_o[:, LANES:2 * LANES] = nkv[:, LANES:2 * LANES]
    nkv_o[:, 2 * LANES:3 * LANES] = k_slc
    nkv_o[:, 3 * LANES:4 * LANES] = v_slc
    ks_o[...] = k_slc.astype(BF16)
    vs_o[...] = v_slc.astype(BF16)

    win = proj(wwin)
    k_win = rope(win[:, :LANES])
    v_win = win[:, LANES:]
    win_o[:, :LANES] = k_win
    win_o[:, LANES:] = v_win
    kw_o[...] = k_win.astype(BF16)
    vw_o[...] = v_win.astype(BF16)

    gf_o[...] = proj(wgf)
    gn_o[...] = proj(wgn)


def _nsa_perm():
    perm = np.zeros(NSA_WIDTH, np.int32)
    for r in range(NSA_HPG):
        for g in range(NSA_GROUPS):
            for d in range(HEAD_DIM):
                perm[r * LANES + g * HEAD_DIM + d] = (g * NSA_HPG + r) * HEAD_DIM + d
    return perm


def _split_w_in(w_in, b_fgate):
    d = w_in.shape[0]
    sizes = (FOX_WIDTH, FOX_WIDTH, FOX_WIDTH, FOX_HEADS, NSA_WIDTH) + (NSA_KV_WIDTH,) * 6 + (
        N_NSA_BRANCHES * NSA_HEADS, d, d)
    offs = np.concatenate([[0], np.cumsum(sizes)])
    col = lambda i, j=None: w_in[:, offs[i]:offs[(i if j is None else j) + 1]]
    pad = lambda w: jnp.pad(w, ((0, 0), (0, LANES - w.shape[1])))
    ws = dict(
        wfq=col(0), wfkv=col(1, 2),
        wsm=jnp.concatenate([pad(col(3)), pad(col(11))], axis=1),
        wnq=col(4)[:, _nsa_perm()], wnkv=col(5, 8), wwin=col(9, 10), wgf=col(12), wgn=col(13))
    ws = {k: v.astype(BF16) for k, v in ws.items()}
    bf = jnp.pad(b_fgate.astype(F32), (0, LANES - FOX_HEADS)).reshape(1, LANES)
    return ws, bf


def _in_proj(x, ws, bf, cos_tab, sin_tab, name):
    n, d = x.shape
    n_tab = cos_tab.shape[0]
    tm = _pick(n_tab, (256, 128, 8))
    tab_tiles = n_tab // tm
    row = lambda w: pl.BlockSpec((tm, w), lambda i: (i, 0))
    tab = pl.BlockSpec((tm, LANES), lambda i: (i % tab_tiles, 0))
    names = ("wfq", "wfkv", "wsm", "wnq", "wnkv", "wwin", "wgf", "wgn")
    outs = [("fq", FOX_WIDTH, BF16), ("fkv", 2 * FOX_WIDTH, F32), ("fk", FOX_WIDTH, BF16), ("fv", FOX_WIDTH, BF16),
            ("lf", FOX_HEADS, F32), ("nq", NSA_WIDTH, BF16), ("nkv", 4 * NSA_KV_WIDTH, F32),
            ("ks", NSA_KV_WIDTH, BF16), ("vs", NSA_KV_WIDTH, BF16), ("win", 2 * NSA_KV_WIDTH, F32),
            ("kw", NSA_KV_WIDTH, BF16), ("vw", NSA_KV_WIDTH, BF16), ("ng", LANES, F32), ("gf", d, F32), ("gn", d, F32)]
    res = pl.pallas_call(
        _in_proj_kernel,
        grid=(n // tm,),
        in_specs=[row(d)] + [_const_spec(ws[k].shape) for k in names] + [_const_spec((1, LANES)), tab, tab],
        out_specs=[row(w) for _, w, _ in outs],
        out_shape=[jax.ShapeDtypeStruct((n, w), dt) for _, w, dt in outs],
        compiler_params=_params(("parallel",), 56),
        name=name,
    )(x, *[ws[k] for k in names], bf, cos_tab, sin_tab)
    return {k: v for (k, _, _), v in zip(outs, res)}


def _cumsum_kernel(x_ref, tri_ref, low_ref, o_ref):
    x = x_ref[0]
    within = lax.dot_general(x, tri_ref[...], (((1,), (0,)), ((), ())), precision=HIGHEST, preferred_element_type=F32)
    tot = jnp.broadcast_to(within[:, LANES - 1:LANES], within.shape)
    before = lax.dot_general(low_ref[...], tot, (((1,), (0,)), ((), ())), precision=HIGHEST, preferred_element_type=F32)
    o_ref[0] = within + before


def _cumsum_time(logf_t, name):
    b, h, t = logf_t.shape
    r = t // LANES
    rows = h * r
    idx = np.arange(LANES)
    tri = jnp.asarray((idx[:, None] <= idx[None, :]).astype(np.float32))
    ridx = np.arange(rows)
    low = jnp.asarray(((ridx[None, :] < ridx[:, None]) & (ridx[None, :] // r == ridx[:, None] // r)).astype(np.float32))
    out = pl.pallas_call(
        _cumsum_kernel,
        grid=(b,),
        in_specs=[pl.BlockSpec((1, rows, LANES), lambda i: (i, 0, 0)), _const_spec((LANES, LANES)),
                  _const_spec((rows, rows))],
        out_specs=pl.BlockSpec((1, rows, LANES), lambda i: (i, 0, 0)),
        out_shape=jax.ShapeDtypeStruct((b, rows, LANES), F32),
        compiler_params=_params(("parallel",), 32),
        name=name,
    )(logf_t.reshape(b, rows, LANES), tri, low)
    return out.reshape(b, h, t)


FLAG_FIRST, FLAG_LAST, FLAG_EDGE = 1, 2, 4


def _tile_schedule(mode, nq, nkt, tq, tk, q_off, k_off):
    qi_tab, kt_tab, flag_tab = [], [], []
    for qi in range(nq):
        q_lo, q_hi = q_off + qi * tq, q_off + (qi + 1) * tq - 1
        tiles = []
        for kt in range(nkt):
            k_lo, k_hi = k_off + kt * tk, k_off + (kt + 1) * tk - 1
            if mode == "win":
                visible = (q_hi - k_lo >= 0) and (q_lo - k_hi < WINDOW)
                interior = (q_lo - k_hi >= 0) and (q_hi - k_lo < WINDOW)
            else:
                visible = k_lo <= q_hi
                interior = k_hi <= q_lo
            if visible:
                tiles.append((kt, 0 if interior else FLAG_EDGE))
        assert tiles, "every query tile sees at least one key tile"
        for n, (kt, flag) in enumerate(tiles):
            qi_tab.append(qi)
            kt_tab.append(kt)
            flag_tab.append(flag | (FLAG_FIRST if n == 0 else 0) | (FLAG_LAST if n == len(tiles) - 1 else 0))
    as_i32 = lambda v: jnp.asarray(np.asarray(v, np.int32))
    return as_i32(qi_tab), as_i32(kt_tab), as_i32(flag_tab)


def _attn_kernel(qi_tab, kt_tab, flag_tab, q_ref, k_ref, v_ref, o_ref, m_ref, acc_ref, *,
                 mode, tq, tk, nh, n_kv, q_off, k_off):
    step = pl.program_id(1)
    qi = qi_tab[step]
    kt = kt_tab[step]
    flags = flag_tab[step]
    lane = lax.broadcasted_iota(jnp.int32, (1, LANES), 1)
    n_chunks = tk // LANES

    @pl.when((flags & FLAG_FIRST) != 0)
    def _():
        m_ref[...] = jnp.full_like(m_ref, M_INIT)
        acc_ref[...] = jnp.zeros_like(acc_ref)

    def tile(edge):
        if edge:
            q_pos = q_off + qi * tq + lax.broadcasted_iota(jnp.int32, (tq, 1), 0)
        scores = lambda i: lax.dot_general(q_ref[0, i], k_ref[0, i % n_kv], NT_DIMS, preferred_element_type=F32)

        def values(i, alpha, pb):
            acc_ref[i] = alpha * acc_ref[i] + jnp.dot(pb, v_ref[0, i % n_kv], preferred_element_type=F32)

        s_next = scores(0)
        pending = None
        for i in range(nh):
            s = s_next
            if i + 1 < nh:
                s_next = scores(i + 1)
            if pending is not None:
                values(*pending)
            chunks = []
            for c in range(n_chunks):
                t = s[:, c * LANES:(c + 1) * LANES]
                if edge:
                    k_pos = k_off + kt * tk + c * LANES + lane
                    if mode == "win":
                        rel = q_pos - k_pos
                        t = jnp.where((rel >= 0) & (rel < WINDOW), t, NEG_INF)
                    else:
                        t = jnp.where(k_pos <= q_pos, t, NEG_INF)
                chunks.append(t)
            mx = chunks[0]
            for t in chunks[1:]:
                mx = jnp.maximum(mx, t)
            m_prev = m_ref[i]
            m_new = jnp.maximum(m_prev, jnp.max(mx, axis=1, keepdims=True))
            alpha = jnp.exp2(m_prev - m_new)
            ps = [jnp.exp2(t - m_new).astype(BF16) for t in chunks]
            pb = jnp.concatenate(ps, axis=1) if n_chunks > 1 else ps[0]
            m_ref[i] = m_new
            pending = (i, alpha, pb)
        values(*pending)

    pl.when((flags & FLAG_EDGE) != 0)(lambda: tile(True))
    pl.when((flags & FLAG_EDGE) == 0)(lambda: tile(False))

    @pl.when((flags & FLAG_LAST) != 0)
    def _():
        for pr in range(nh // 2):
            outs = []
            for side in range(2):
                acc = acc_ref[2 * pr + side]
                outs.append(acc / jnp.maximum(acc[:, SUM_LANE:SUM_LANE + 1], 1e-30))
            o_ref[0, :, pr * LANES:(pr + 1) * LANES] = jnp.where(lane < HEAD_DIM, outs[0],
                                                                  pltpu.roll(outs[1], HEAD_DIM, 1))


def _attention(mode, q, k, v, *, tq, tk, q_off, k_off, name):
    b, nh, t_q, _ = q.shape
    n_kv, t_k = k.shape[1], k.shape[2]
    qi_tab, kt_tab, flag_tab = _tile_schedule(mode, t_q // tq, t_k // tk, tq, tk, q_off, k_off)
    n_steps = qi_tab.shape[0]
    q_spec = pl.BlockSpec((1, nh, tq, LANES), lambda bi, s, qt, kt, fl: (bi, 0, qt[s], 0))
    kv_spec = pl.BlockSpec((1, n_kv, tk, LANES), lambda bi, s, qt, kt, fl: (bi, 0, kt[s], 0))
    grid_spec = pltpu.PrefetchScalarGridSpec(
        num_scalar_prefetch=3, grid=(b, n_steps), in_specs=[q_spec, kv_spec, kv_spec],
        out_specs=pl.BlockSpec((1, tq, nh // 2 * LANES), lambda bi, s, qt, kt, fl: (bi, qt[s], 0)),
        scratch_shapes=[pltpu.VMEM((nh, tq, LANES), F32), pltpu.VMEM((nh, tq, LANES), F32)])
    return pl.pallas_call(
        functools.partial(_attn_kernel, mode=mode, tq=tq, tk=tk, nh=nh, n_kv=n_kv, q_off=q_off, k_off=k_off),
        grid_spec=grid_spec,
        out_shape=jax.ShapeDtypeStruct((b, t_q, nh // 2 * LANES), F32),
        compiler_params=_params(("parallel", "arbitrary"), 48),
        name=name,
    )(qi_tab, kt_tab, flag_tab, q, k, v)


def _limbs(x):
    hi = x.astype(BF16).astype(F32)
    mid = (x - hi).astype(BF16).astype(F32)
    lo = (x - hi - mid).astype(BF16).astype(F32)
    return hi, mid, lo


def _head_of_pair(pair, side):
    return pair if side == 0 else pltpu.roll(pair, HEAD_DIM, 1)


def _augment_fox_kernel(q_ref, k_ref, v_ref, c_ref, qa_ref, ka_ref, va_ref):
    lane = lax.broadcasted_iota(jnp.int32, (1, LANES), 1)
    data = lane < HEAD_DIM
    c = c_ref[0] * LOG2E
    for h in range(FOX_HEADS):
        p, side = divmod(h, 2)
        hi, mid, lo = _limbs(c[:, h:h + 1])
        q_extra = jnp.where(lane < HEAD_DIM + 3, 1.0,
                            jnp.where(lane == HEAD_DIM + 3, hi,
                                      jnp.where(lane == HEAD_DIM + 4, mid, jnp.where(lane == HEAD_DIM + 5, lo, 0.0))))
        k_extra = jnp.where(lane == HEAD_DIM, -hi,
                            jnp.where(lane == HEAD_DIM + 1, -mid,
                                      jnp.where(lane == HEAD_DIM + 2, -lo, jnp.where(lane < HEAD_DIM + 6, 1.0, 0.0))))
        v_extra = jnp.where(lane == SUM_LANE, 1.0, 0.0)
        cols = slice(p * LANES, (p + 1) * LANES)
        qa_ref[0, h] = jnp.where(data, _head_of_pair(q_ref[0, :, cols].astype(F32), side), q_extra).astype(BF16)
        ka_ref[0, h] = jnp.where(data, _head_of_pair(k_ref[0, :, cols].astype(F32), side), k_extra).astype(BF16)
        va_ref[0, h] = jnp.where(data, _head_of_pair(v_ref[0, :, cols].astype(F32), side), v_extra).astype(BF16)


def _augment_fox(q, k, v, c_col):
    b, t, w = q.shape
    tr = _pick(t, (512, 256, 128, 8))
    row = pl.BlockSpec((1, tr, w), lambda bi, i: (bi, i, 0))
    out = pl.BlockSpec((1, FOX_HEADS, tr, LANES), lambda bi, i: (bi, 0, i, 0))
    shape = jax.ShapeDtypeStruct((b, FOX_HEADS, t, LANES), BF16)
    return pl.pallas_call(
        _augment_fox_kernel,
        grid=(b, t // tr),
        in_specs=[row, row, row, pl.BlockSpec((1, tr, FOX_HEADS), lambda bi, i: (bi, i, 0))],
        out_specs=[out, out, out],
        out_shape=[shape, shape, shape],
        compiler_params=_params(("parallel", "parallel"), 32),
        name="augment_fox",
    )(q, k, v, c_col)


def _augment_nsa_kernel(nq_ref, sel_ref, ks_ref, vs_ref, kw_ref, vw_ref, qa_ref, ksa_ref, vsa_ref, kwa_ref, vwa_ref,
                        *, tr, nbp):
    lane = lax.broadcasted_iota(jnp.int32, (1, LANES), 1)
    data = lane < HEAD_DIM
    t_row = pl.program_id(1) * tr + lax.broadcasted_iota(jnp.int32, (tr, 1), 0)
    block_hot = jnp.where(t_row // SLC_LEN == lane - HEAD_DIM, 1.0, 0.0)
    ones_lane = jnp.where(lane == SUM_LANE, 1.0, 0.0)
    for g in range(NSA_GROUPS):
        bias = pltpu.roll(sel_ref[0, :, g * nbp:g * nbp + LANES].astype(F32), HEAD_DIM, 1)
        for r in range(NSA_HPG):
            qp = nq_ref[0, :, r * LANES:(r + 1) * LANES].astype(F32)
            qa_ref[0, NSA_GROUPS * r + g] = jnp.where(data, _head_of_pair(qp, g), bias).astype(BF16)
        ksa_ref[0, g] = jnp.where(data, _head_of_pair(ks_ref[0].astype(F32), g), block_hot).astype(BF16)
        vsa_ref[0, g] = jnp.where(data, _head_of_pair(vs_ref[0].astype(F32), g), ones_lane).astype(BF16)
        kwa_ref[0, g] = jnp.where(data, _head_of_pair(kw_ref[0].astype(F32), g), 0.0).astype(BF16)
        vwa_ref[0, g] = jnp.where(data, _head_of_pair(vw_ref[0].astype(F32), g), ones_lane).astype(BF16)


def _augment_nsa(nq, sel, ks, vs, kw, vw, nbp):
    b, t, w = nq.shape
    tr = _pick(t, (512, 256, 128, 8))
    row = lambda width: pl.BlockSpec((1, tr, width), lambda bi, i: (bi, i, 0))
    out = lambda heads: pl.BlockSpec((1, heads, tr, LANES), lambda bi, i: (bi, 0, i, 0))
    shape = lambda heads: jax.ShapeDtypeStruct((b, heads, t, LANES), BF16)
    return pl.pallas_call(
        functools.partial(_augment_nsa_kernel, tr=tr, nbp=nbp),
        grid=(b, t // tr),
        in_specs=[row(w), row(NSA_GROUPS * nbp)] + [row(LANES)] * 4,
        out_specs=[out(NSA_HEADS)] + [out(NSA_GROUPS)] * 4,
        out_shape=[shape(NSA_HEADS)] + [shape(NSA_GROUPS)] * 4,
        compiler_params=_params(("parallel", "parallel"), 32),
        name="augment_nsa",
    )(nq, sel, ks, vs, kw, vw)


def _head_major(a, n_heads):
    b, t, _ = a.shape
    return a.reshape(b, t, n_heads, HEAD_DIM).transpose(0, 2, 1, 3)


def _augment(head_rows, extras):
    b, h, t, _ = head_rows.shape
    cols = [jnp.broadcast_to(e, (b, h, t))[..., None].astype(BF16) for e in extras]
    pad = jnp.zeros((b, h, t, LANES - HEAD_DIM - len(cols)), BF16)
    return jnp.concatenate([head_rows] + cols + [pad], axis=-1)


def _compress_rows(row_of_half, pa_ref, pb_ref, wa_ref, wb_ref, w2_ref, ns):
    u = jnp.concatenate([row_of_half(r) for r in range(CMP_STRIDE)], axis=1)
    first = jnp.dot((u + pa_ref[...]).astype(BF16), wa_ref[...], preferred_element_type=F32)
    second = jnp.dot((u + pb_ref[...]).astype(BF16), wb_ref[...], preferred_element_type=F32)
    pre = first + pltpu.roll(second, ns - 1, 0)
    h = (pre * jax.nn.sigmoid(pre)).astype(BF16)
    return jnp.dot(h, w2_ref[...], preferred_element_type=F32).astype(BF16)


def _compress_kernel(rows_ref, pa_ref, pb_ref, wa_ref, wb_ref, w2_ref, o_ref, *, ns):
    o_ref[0] = _compress_rows(lambda r: rows_ref[0, pl.ds(r, ns, stride=CMP_STRIDE), :],
                              pa_ref, pb_ref, wa_ref, wb_ref, w2_ref, ns)


def _compress_weights(pos, w1, w2):
    ratio = CMP_LEN // CMP_STRIDE
    assert ratio == 2
    w1r = w1.reshape(CMP_LEN, HEAD_DIM, CMP_HIDDEN)
    zeros = jnp.zeros((CMP_STRIDE, HEAD_DIM, CMP_HIDDEN), w1.dtype)

    def half(rows):
        g0 = jnp.concatenate([jnp.stack([rows, zeros], axis=1).reshape(-1, CMP_HIDDEN),
                              jnp.stack([zeros, rows], axis=1).reshape(-1, CMP_HIDDEN)], axis=1)
        return g0.astype(BF16)

    def pos_tab(p):
        return jnp.stack([p, p], axis=1).reshape(1, -1).astype(F32)

    z2 = jnp.zeros_like(w2)
    w2d = jnp.concatenate([jnp.concatenate([w2, z2], axis=1), jnp.concatenate([z2, w2], axis=1)], axis=0)
    return (pos_tab(pos[:CMP_STRIDE]), pos_tab(pos[CMP_STRIDE:]), half(w1r[:CMP_STRIDE]), half(w1r[CMP_STRIDE:]),
            w2d.astype(BF16))


def _compress(rows, col_block, weights, name):
    b, t, _ = rows.shape
    ns = t // CMP_STRIDE
    width = CMP_STRIDE * LANES
    pa, pb, wa, wb, w2d = weights
    return pl.pallas_call(
        functools.partial(_compress_kernel, ns=ns),
        grid=(b,),
        in_specs=[pl.BlockSpec((1, t, LANES), lambda i: (i, 0, col_block)), _const_spec((1, width)),
                  _const_spec((1, width)), _const_spec(wa.shape), _const_spec(wb.shape), _const_spec(w2d.shape)],
        out_specs=pl.BlockSpec((1, ns, LANES), lambda i: (i, 0, 0)),
        out_shape=jax.ShapeDtypeStruct((b, ns, LANES), BF16),
        compiler_params=_params(("parallel",), 48),
        name=name,
    )(rows, pa, pb, wa, wb, w2d)


def _cmp_select_kernel(q_ref, kc_ref, vc_ref, cov_ref, oc_ref, sel_ref, imp_ref, vt_ref, *,
                       sps, tq, rows_p, nc, n_cmp, n_slc, nbp, q_off, n_sel):
    qi = pl.program_id(1)
    lane = lax.broadcasted_iota(jnp.int32, (1, LANES), 1)
    left = lane < HEAD_DIM
    q_pos = q_off + qi * tq + lax.broadcasted_iota(jnp.int32, (tq, 1), 0)
    n_idx = lax.broadcasted_iota(jnp.int32, (1, nc), 1)
    cmask = ((n_idx * CMP_STRIDE + CMP_LEN - 1) <= q_pos) & (n_idx < n_cmp)
    cov = cov_ref[...]
    if rows_p > sps * tq:
        imp_ref[...] = jnp.zeros_like(imp_ref)

    def attend(sq, carry):
        kc = kc_ref[sq]
        vc = vc_ref[sq]
        imp = [jnp.zeros((tq, nbp), F32) for _ in range(NSA_GROUPS)]
        heads = [(r, g) for r in range(N_PAIRS) for g in range(NSA_GROUPS)]

        def scores(r, g):
            qp = q_ref[sq, :, r * LANES:(r + 1) * LANES]
            qh = jnp.where(left if g == 0 else jnp.logical_not(left), qp, jnp.zeros_like(qp))
            return lax.dot_general(qh, kc, NT_DIMS, preferred_element_type=F32)

        halves = []
        s_next = scores(*heads[0])
        for n, (r, g) in enumerate(heads):
            s = s_next
            if n + 1 < len(heads):
                s_next = scores(*heads[n + 1])
            s = jnp.where(cmask, s, NEG_INF)
            m = jnp.maximum(jnp.max(s, axis=1, keepdims=True), M_INIT)
            e = jnp.exp2(s - m)
            inv = 1.0 / jnp.maximum(jnp.sum(e, axis=1, keepdims=True), 1e-30)
            pb = (e * inv).astype(BF16)
            halves.append(jnp.dot(pb, vc, preferred_element_type=F32))
            imp[g] = imp[g] + jnp.dot(pb, cov, preferred_element_type=F32)
            if g == NSA_GROUPS - 1:
                oc_ref[sq, :, r * LANES:(r + 1) * LANES] = jnp.where(left, halves[-2], halves[-1])
        first_row = pl.multiple_of(sq * tq, tq)
        for g in range(NSA_GROUPS):
            imp_ref[g, pl.ds(first_row, tq), :] = imp[g]
        return carry

    if sps == 1:
        attend(0, 0)
    else:
        lax.fori_loop(0, sps, attend, 0)

    pos = q_off + qi * tq + lax.broadcasted_iota(jnp.int32, (rows_p, 1), 0) % tq
    blk = lax.broadcasted_iota(jnp.int32, (1, nbp), 1)
    cur = pos // SLC_LEN
    forced = (blk == 0) | (blk == cur) | (blk == cur - 1)
    valid = (blk * SLC_LEN <= pos) & (blk < n_slc)
    nbr = _round_up(n_slc, 8)
    row = lax.broadcasted_iota(jnp.int32, (nbr, rows_p), 0)
    n_live = jnp.minimum(n_slc, (q_off + (qi + 1) * tq - 1) // SLC_LEN + 1)
    for g in range(NSA_GROUPS):
        val = jnp.where(valid, jnp.where(forced, imp_ref[g] + FORCE_BONUS, imp_ref[g]), NEG_INF)
        vt_ref[...] = val.T

        def body(i, cnt):
            vi = vt_ref[pl.ds(i, 1), :]
            vt = vt_ref[0:nbr, :]
            ahead = (vi > vt) | ((vi == vt) & (i < row))
            return cnt + ahead.astype(F32)

        cnt = lax.fori_loop(0, n_live, body, jnp.zeros((nbr, rows_p), F32))
        bias = jnp.where(cnt < n_sel, 0.0, NEG_INF)
        if nbp > nbr:
            bias = jnp.concatenate([bias, jnp.full((nbp - nbr, rows_p), NEG_INF, F32)], axis=0)
        bias_t = bias.T
        for sq in range(sps):
            sel_ref[sq, :, g * nbp:(g + 1) * nbp] = bias_t[sq * tq:(sq + 1) * tq].astype(BF16)


def _coverage(n_cmp, nc, n_slc, nbp):
    c0 = np.arange(nc) * CMP_STRIDE
    s0 = np.arange(nbp) * SLC_LEN
    lo = np.maximum(c0[:, None], s0[None, :])
    hi = np.minimum(c0[:, None] + CMP_LEN, s0[None, :] + SLC_LEN)
    cov = np.maximum(hi - lo, 0).astype(np.float32) / CMP_LEN
    cov[n_cmp:, :] = 0.0
    cov[:, n_slc:] = 0.0
    return jnp.asarray(cov, dtype=BF16)


def _cmp_select(q, kc, vc, *, tq, sps, n_cmp, n_slc, nbp, q_off, name):
    b, t_q, _ = q.shape
    nc = kc.shape[1]
    rows_p = _round_up(sps * tq, LANES)
    cov = _coverage(n_cmp, nc, n_slc, nbp)
    kern = functools.partial(_cmp_select_kernel, sps=sps, tq=tq, rows_p=rows_p, nc=nc, n_cmp=n_cmp, n_slc=n_slc,
                             nbp=nbp, q_off=q_off, n_sel=min(N_SELECT, n_slc))
    return pl.pallas_call(
        kern,
        grid=(b // sps, t_q // tq),
        in_specs=[pl.BlockSpec((sps, tq, N_PAIRS * LANES), lambda bi, qi: (bi, qi, 0)),
                  pl.BlockSpec((sps, nc, LANES), lambda bi, qi: (bi, 0, 0)),
                  pl.BlockSpec((sps, nc, LANES), lambda bi, qi: (bi, 0, 0)),
                  _const_spec((nc, nbp))],
        out_specs=[pl.BlockSpec((sps, tq, N_PAIRS * LANES), lambda bi, qi: (bi, qi, 0)),
                   pl.BlockSpec((sps, tq, NSA_GROUPS * nbp), lambda bi, qi: (bi, qi, 0))],
        out_shape=[jax.ShapeDtypeStruct((b, t_q, N_PAIRS * LANES), F32),
                   jax.ShapeDtypeStruct((b, t_q, NSA_GROUPS * nbp), BF16)],
        scratch_shapes=[pltpu.VMEM((NSA_GROUPS, rows_p, nbp), F32), pltpu.VMEM((nbp, rows_p), F32)],
        compiler_params=_params(("parallel", "parallel"), 40),
        name=name,
    )(q, kc, vc, cov)


def _merge_kernel(x_ref, fo_ref, oc_ref, os_ref, ow_ref, ng_ref, gf_ref, gn_ref, wuf_ref, wun_ref, wout_ref,
                  eg_ref, g_ref, b_ref, o_ref, *, alpha):
    gates = jax.nn.sigmoid(ng_ref[...])
    gx = lax.dot_general(gates, eg_ref[...], (((1,), (0,)), ((), ())), precision=HIGHEST, preferred_element_type=F32)
    w = NSA_WIDTH
    nsa_o = gx[:, :w] * oc_ref[...] + gx[:, w:2 * w] * os_ref[...] + gx[:, 2 * w:] * ow_ref[...]
    up_f = jnp.dot(fo_ref[...].astype(BF16), wuf_ref[...], preferred_element_type=F32)
    up_n = jnp.dot(nsa_o.astype(BF16), wun_ref[...], preferred_element_type=F32)
    mixed = jax.nn.sigmoid(gf_ref[...]) * up_f + jax.nn.sigmoid(gn_ref[...]) * up_n
    mix = jnp.dot(mixed.astype(BF16), wout_ref[...], preferred_element_type=F32)
    o_ref[...] = _layer_norm(alpha * x_ref[...] + mix, g_ref[...], b_ref[...])


def _gate_expand():
    perm = _nsa_perm()
    e = np.zeros((LANES, N_NSA_BRANCHES * NSA_WIDTH), np.float32)
    for br in range(N_NSA_BRANCHES):
        for pos in range(NSA_WIDTH):
            head = perm[pos] // HEAD_DIM
            e[br * NSA_HEADS + head, br * NSA_WIDTH + pos] = 1.0
    return jnp.asarray(e)


def _merge_ln(x, fox_o, o_c, o_s, o_w, ng, gf, gn, w_up_fox, w_up_nsa, w_out, g, b, alpha, name):
    n, d = x.shape
    tm = _pick(n, (256, 128, 8))
    row = lambda w: pl.BlockSpec((tm, w), lambda i: (i, 0))
    wuf = w_up_fox.astype(BF16)
    wun = w_up_nsa[_nsa_perm(), :].astype(BF16)
    wout = w_out.astype(BF16)
    eg = _gate_expand()
    return pl.pallas_call(
        functools.partial(_merge_kernel, alpha=alpha),
        grid=(n // tm,),
        in_specs=[row(d), row(FOX_WIDTH), row(NSA_WIDTH), row(NSA_WIDTH), row(NSA_WIDTH), row(LANES), row(d), row(d),
                  _const_spec(wuf.shape), _const_spec(wun.shape), _const_spec(wout.shape), _const_spec(eg.shape),
                  _const_spec((1, d)), _const_spec((1, d))],
        out_specs=row(d),
        out_shape=jax.ShapeDtypeStruct((n, d), F32),
        compiler_params=_params(("parallel",), 48),
        name=name,
    )(x, fox_o, o_c, o_s, o_w, ng, gf, gn, wuf, wun, wout, eg, g.reshape(1, d), b.reshape(1, d))


def _page_specs(block, group, n_pages, second=0):
    nd = len(block)

    def spec(slot):
        return pl.BlockSpec(block, lambda si, j, pt: (pt[si * n_pages + jnp.minimum(j * group + slot, n_pages - 1)],
                                                      second) + (0,) * (nd - 2))

    return [spec(slot) for slot in range(group)]


def _lf_pages_kernel(pt_ref, *refs, group):
    o_ref = refs[group]
    for i in range(group):
        o_ref[0, :, i * PAGE_SIZE:(i + 1) * PAGE_SIZE] = refs[i][0]


def _lf_pages(page_table, cache_lf_t):
    s, n_pages = page_table.shape
    group = _pick(n_pages, (64, 32, 16, 8, 4, 2, 1))
    grid_spec = pltpu.PrefetchScalarGridSpec(
        num_scalar_prefetch=1, grid=(s, n_pages // group),
        in_specs=_page_specs((1, FOX_HEADS, PAGE_SIZE), group, n_pages),
        out_specs=pl.BlockSpec((1, FOX_HEADS, group * PAGE_SIZE), lambda si, j, pt: (si, 0, j)))
    return pl.pallas_call(
        functools.partial(_lf_pages_kernel, group=group),
        grid_spec=grid_spec,
        out_shape=jax.ShapeDtypeStruct((s, FOX_HEADS, n_pages * PAGE_SIZE), F32),
        compiler_params=_params(("parallel", "arbitrary"), 32),
        name="lf_pages",
    )(page_table.reshape(-1), *([cache_lf_t] * group))


def _fox_sample_kernel(pt_ref, q_ref, cc_ref, cr_ref, crn_ref, kn_ref, vn_ref, *refs, group, n_steps, n_new):
    pages = refs[:group]
    o_ref, m_ref, l_ref, acc_ref = refs[group:]
    j = pl.program_id(1)

    @pl.when(j == 0)
    def _():
        m_ref[...] = jnp.full_like(m_ref, M_INIT)
        l_ref[...] = jnp.zeros_like(l_ref)
        acc_ref[...] = jnp.zeros_like(acc_ref)

    def update(s, value_dots):
        m_prev = m_ref[...]
        m_new = jnp.maximum(m_prev, jnp.max(s, axis=1, keepdims=True))
        p = jnp.exp2(s - m_new)
        alpha = jnp.exp2(m_prev - m_new)
        l_ref[...] = alpha * l_ref[...] + jnp.sum(p, axis=1, keepdims=True)
        acc_ref[...] = alpha * acc_ref[...] + jnp.concatenate(
            [value_dots(h, p[h * n_new:(h + 1) * n_new].astype(BF16)) for h in range(FOX_HEADS)], axis=0)
        m_ref[...] = m_new

    def tiles(kv, h):
        return jnp.concatenate([pages[i][0, kv, h].astype(BF16) for i in range(group)], axis=1)

    scores = [jnp.dot(q_ref[0, h], tiles(0, h), preferred_element_type=F32) + (cc_ref[0, h] - cr_ref[0, h:h + 1, :])
              for h in range(FOX_HEADS)]
    update(jnp.concatenate(scores, axis=0),
           lambda h, pb: lax.dot_general(pb, tiles(1, h), NT_DIMS, preferred_element_type=F32))

    @pl.when(j == n_steps - 1)
    def _():
        t_q = lax.broadcasted_iota(jnp.int32, (n_new, n_new), 0)
        t_k = lax.broadcasted_iota(jnp.int32, (n_new, n_new), 1)
        new_scores = []
        for h in range(FOX_HEADS):
            s = lax.dot_general(q_ref[0, h], kn_ref[0, h], NT_DIMS, preferred_element_type=F32)
            new_scores.append(jnp.where(t_k <= t_q, s + (cc_ref[0, h] - crn_ref[0, h:h + 1, :]), NEG_INF))
        update(jnp.concatenate(new_scores, axis=0),
               lambda h, pb: jnp.dot(pb, vn_ref[0, h], preferred_element_type=F32))
        o_ref[0] = acc_ref[...] / jnp.maximum(l_ref[...], 1e-30)


def _fox_sample(page_table, cache_kv_t, q_h, k_new_h, v_new_h, c_q, c_row, c_new):
    s, n_pages = page_table.shape
    n_new = q_h.shape[2]
    group = _pick(n_pages, (16, 8, 4, 2, 1))
    n_steps = n_pages // group
    rows = FOX_HEADS * n_new
    seq4 = lambda shape: pl.BlockSpec(shape, lambda si, j, pt: (si, 0, 0, 0))
    grid_spec = pltpu.PrefetchScalarGridSpec(
        num_scalar_prefetch=1, grid=(s, n_steps),
        in_specs=[seq4((1, FOX_HEADS, n_new, HEAD_DIM)), seq4((1, FOX_HEADS, n_new, 1)),
                  pl.BlockSpec((1, FOX_HEADS, group * PAGE_SIZE), lambda si, j, pt: (si, 0, j)),
                  pl.BlockSpec((1, FOX_HEADS, n_new), lambda si, j, pt: (si, 0, 0)),
                  seq4((1, FOX_HEADS, n_new, HEAD_DIM)), seq4((1, FOX_HEADS, n_new, HEAD_DIM))]
        + _page_specs((1, 2, FOX_HEADS, HEAD_DIM, PAGE_SIZE), group, n_pages),
        out_specs=pl.BlockSpec((1, rows, HEAD_DIM), lambda si, j, pt: (si, 0, 0)),
        scratch_shapes=[pltpu.VMEM((rows, 1), F32), pltpu.VMEM((rows, 1), F32), pltpu.VMEM((rows, HEAD_DIM), F32)])
    out = pl.pallas_call(
        functools.partial(_fox_sample_kernel, group=group, n_steps=n_steps, n_new=n_new),
        grid_spec=grid_spec,
        out_shape=jax.ShapeDtypeStruct((s, rows, HEAD_DIM), F32),
        compiler_params=_params(("parallel", "arbitrary"), 48),
        name="fox_sample",
    )(page_table.reshape(-1), q_h, c_q, c_row, c_new, k_new_h, v_new_h, *([cache_kv_t] * group))
    return out.reshape(s, FOX_HEADS, n_new, HEAD_DIM)


def _to_rows(tile):
    g, d, n = tile.shape
    return tile.reshape(g * d, n).T


def _compress_pages_kernel(pt_ref, pak, pbk, wak, wbk, w2k, pav, pbv, wav, wbv, w2v, *refs, group, n_steps, ns):
    pages = refs[:group]
    kc_o, vc_o, rk_ref, rv_ref = refs[group:]
    j = pl.program_id(1)
    for i in range(group):
        first = pl.multiple_of((j * group + i) * PAGE_SIZE, PAGE_SIZE)
        rk_ref[pl.ds(first, PAGE_SIZE), :] = _to_rows(pages[i][0, 0])
        rv_ref[pl.ds(first, PAGE_SIZE), :] = _to_rows(pages[i][0, 1])

    @pl.when(j == n_steps - 1)
    def _():
        half_rows = lambda ref: (lambda r: ref[pl.ds(r, ns, stride=CMP_STRIDE), :])
        kc_o[0] = _compress_rows(half_rows(rk_ref), pak, pbk, wak, wbk, w2k, ns)
        vc_o[0] = _compress_rows(half_rows(rv_ref), pav, pbv, wav, wbv, w2v, ns)


def _compress_pages(page_table, cache_kv_t, cmp_k, cmp_v):
    s, n_pages = page_table.shape
    group = _pick(n_pages, (32, 16, 8, 4, 2, 1))
    n_steps = n_pages // group
    past = n_pages * PAGE_SIZE
    ns = past // CMP_STRIDE
    consts = list(cmp_k) + list(cmp_v)
    out = pl.BlockSpec((1, ns, LANES), lambda si, j, pt: (si, 0, 0))
    grid_spec = pltpu.PrefetchScalarGridSpec(
        num_scalar_prefetch=1, grid=(s, n_steps),
        in_specs=[_const_spec(c.shape) for c in consts]
        + _page_specs((1, 2, NSA_GROUPS, HEAD_DIM, PAGE_SIZE), group, n_pages, second=0),
        out_specs=[out, out],
        scratch_shapes=[pltpu.VMEM((past, LANES), F32), pltpu.VMEM((past, LANES), F32)])
    return pl.pallas_call(
        functools.partial(_compress_pages_kernel, group=group, n_steps=n_steps, ns=ns),
        grid_spec=grid_spec,
        out_shape=[jax.ShapeDtypeStruct((s, ns, LANES), BF16), jax.ShapeDtypeStruct((s, ns, LANES), BF16)],
        compiler_params=_params(("parallel", "arbitrary"), 48),
        name="compress_pages",
    )(page_table.reshape(-1), *consts, *([cache_kv_t] * group))


def _slc_sample_kernel(pt_ref, q_ref, sel_ref, kn_ref, vn_ref, *refs, group, n_steps, n_new, nbp, n_slc):
    pages = refs[:group]
    o_ref, m_ref, l_ref, acc_ref = refs[group:]
    j = pl.program_id(1)
    n_keys = group * PAGE_SIZE
    rows = NSA_GROUPS * NSA_HPG * n_new

    @pl.when(j == 0)
    def _():
        m_ref[...] = jnp.full_like(m_ref, M_INIT)
        l_ref[...] = jnp.zeros_like(l_ref)
        acc_ref[...] = jnp.zeros_like(acc_ref)

    def group_rows(per_query):
        return jnp.concatenate([per_query[g] for g in range(NSA_GROUPS) for _ in range(NSA_HPG)], axis=0)

    def update(s, values):
        m_prev = m_ref[...]
        m_new = jnp.maximum(m_prev, jnp.max(s, axis=1, keepdims=True))
        p = jnp.exp2(s - m_new)
        alpha = jnp.exp2(m_prev - m_new)
        l_ref[...] = alpha * l_ref[...] + jnp.sum(p, axis=1, keepdims=True)
        acc_ref[...] = alpha * acc_ref[...] + values(p.astype(BF16))
        m_ref[...] = m_new

    def tiles(c):
        return jnp.concatenate([pages[i][0, c].reshape(NSA_KV_WIDTH, PAGE_SIZE).astype(BF16) for i in range(group)],
                               axis=1)

    q = q_ref[0]
    blk_row = lax.broadcasted_iota(jnp.int32, (nbp, n_keys), 0)
    blk_key = (j * n_keys + lax.broadcasted_iota(jnp.int32, (nbp, n_keys), 1)) // SLC_LEN
    expand = (blk_row == blk_key).astype(BF16)
    bias = [jnp.dot(sel_ref[0, :, g * nbp:(g + 1) * nbp], expand, preferred_element_type=F32)
            for g in range(NSA_GROUPS)]
    update(jnp.dot(q, tiles(0), preferred_element_type=F32) + group_rows(bias),
           lambda pb: lax.dot_general(pb, tiles(1), NT_DIMS, preferred_element_type=F32))

    @pl.when(j == n_steps - 1)
    def _():
        t_q = lax.broadcasted_iota(jnp.int32, (rows, n_new), 0) % n_new
        t_k = lax.broadcasted_iota(jnp.int32, (rows, n_new), 1)
        last_blk = [sel_ref[0, :, g * nbp + n_slc - 1:g * nbp + n_slc].astype(F32) for g in range(NSA_GROUPS)]
        s = lax.dot_general(q, kn_ref[0], NT_DIMS, preferred_element_type=F32) + group_rows(last_blk)
        update(jnp.where(t_k <= t_q, s, NEG_INF), lambda pb: jnp.dot(pb, vn_ref[0], preferred_element_type=F32))
        o_ref[0] = acc_ref[...] / jnp.maximum(l_ref[...], 1e-30)


def _slc_sample(page_table, cache_kv_t, q_rows, sel, ks_new, vs_new, *, nbp, n_slc):
    s, n_pages = page_table.shape
    n_new = ks_new.shape[1]
    rows = q_rows.shape[1]
    group = _pick(n_pages, (32, 16, 8, 4, 2, 1))
    n_steps = n_pages // group
    seq = lambda shape: pl.BlockSpec(shape, lambda si, j, pt: (si, 0, 0))
    grid_spec = pltpu.PrefetchScalarGridSpec(
        num_scalar_prefetch=1, grid=(s, n_steps),
        in_specs=[seq((1, rows, LANES)), seq((1, n_new, NSA_GROUPS * nbp)), seq((1, n_new, LANES)),
                  seq((1, n_new, LANES))]
        + _page_specs((1, 2, NSA_GROUPS, HEAD_DIM, PAGE_SIZE), group, n_pages, second=1),
        out_specs=seq((1, rows, LANES)),
        scratch_shapes=[pltpu.VMEM((rows, 1), F32), pltpu.VMEM((rows, 1), F32), pltpu.VMEM((rows, LANES), F32)])
    return pl.pallas_call(
        functools.partial(_slc_sample_kernel, group=group, n_steps=n_steps, n_new=n_new, nbp=nbp, n_slc=n_slc),
        grid_spec=grid_spec,
        out_shape=jax.ShapeDtypeStruct((s, rows, LANES), F32),
        compiler_params=_params(("parallel", "arbitrary"), 40),
        name="slc_sample",
    )(page_table.reshape(-1), q_rows, sel, ks_new, vs_new, *([cache_kv_t] * group))


def _win_rows_kernel(st_ref, kwnew_ref, vwnew_ref, kw_o, vw_o, *, keep, n_new):
    for c, new_ref, o_ref in ((0, kwnew_ref, kw_o), (1, vwnew_ref, vw_o)):
        o_ref[0] = jnp.zeros(o_ref.shape[1:], BF16)
        o_ref[0, 0:keep, :] = _to_rows(st_ref[0, c]).astype(BF16)
        o_ref[0, keep:keep + n_new, :] = new_ref[0]


def _win_rows(state_t, kw_new, vw_new, t_all):
    s, keep = state_t.shape[0], state_t.shape[-1]
    n_new = kw_new.shape[1]
    seq = lambda si: (si, 0, 0)
    return pl.pallas_call(
        functools.partial(_win_rows_kernel, keep=keep, n_new=n_new),
        grid=(s,),
        in_specs=[pl.BlockSpec((1, 2, NSA_GROUPS, HEAD_DIM, keep), lambda si: (si, 0, 0, 0, 0)),
                  pl.BlockSpec((1, n_new, LANES), seq), pl.BlockSpec((1, n_new, LANES), seq)],
        out_specs=[pl.BlockSpec((1, t_all, LANES), seq), pl.BlockSpec((1, t_all, LANES), seq)],
        out_shape=[jax.ShapeDtypeStruct((s, t_all, LANES), BF16), jax.ShapeDtypeStruct((s, t_all, LANES), BF16)],
        compiler_params=_params(("parallel",), 32),
        name="win_rows",
    )(state_t, kw_new, vw_new)


def _rows_minor(a):
    return jnp.moveaxis(a, 1, -1)


def kernel(x_prompt, x_sample, cache_fox_kv, cache_fox_logf, cache_nsa_kv, state_win_kv, page_table, ln1_g, ln1_b, ffn1_w_up, ffn1_w_down, w_in, b_fgate, cmp_pos_k, cmp_wk1, cmp_wk2, cmp_pos_v, cmp_wv1, cmp_wv2, w_up_fox, w_up_nsa, w_out, ln2_g, ln2_b, ffn2_w_up, ffn2_w_down, ln3_g, ln3_b):
    depth = ln1_g.shape[0]
    assert depth == 1, "single-layer step"
    alpha = (2.0 * depth) ** 0.25
    bsz, seq, d = x_prompt.shape
    sb, n_new, _ = x_sample.shape
    n_pages = page_table.shape[1]
    past = n_pages * PAGE_SIZE
    keep = state_win_kv.shape[2]
    assert seq % 256 == 0 and past % SLC_LEN == 0 and n_new < CMP_STRIDE and keep == WINDOW
    layer = 0

    ws, bf = _split_w_in(w_in[layer], b_fgate[layer])
    cmp_k = _compress_weights(cmp_pos_k[layer], cmp_wk1[layer], cmp_wk2[layer])
    cmp_v = _compress_weights(cmp_pos_v[layer], cmp_wv1[layer], cmp_wv2[layer])

    n = bsz * seq
    xp = _ffn_ln(x_prompt.reshape(n, d), ffn1_w_up[layer], ffn1_w_down[layer], ln1_g[layer], ln1_b[layer], alpha,
                 "ffn1_prompt")
    cos_p, sin_p = _rope_tables(jnp.arange(seq, dtype=jnp.int32))
    mp = _in_proj(xp, ws, bf, cos_p, sin_p, "in_proj_prompt")
    r3 = lambda a: a.reshape(bsz, seq, a.shape[-1])

    c_p = _cumsum_time(r3(mp["lf"]).transpose(0, 2, 1), "cumsum_prompt")
    tq = _pick(seq, (512, 256))
    tk = _pick(seq, (1024, 512, 256))
    q_fox, k_fox, v_fox = _augment_fox(r3(mp["fq"]), r3(mp["fk"]), r3(mp["fv"]), c_p.transpose(0, 2, 1))
    fox_o = _attention("fox", q_fox, k_fox, v_fox, tq=tq, tk=tk, q_off=0, k_off=0, name="fox_prompt")

    nkv3 = r3(mp["nkv"])
    kc_p = _compress(nkv3, 0, cmp_k, "compress_k_prompt")
    vc_p = _compress(nkv3, 1, cmp_v, "compress_v_prompt")
    n_cmp_p = seq // CMP_STRIDE - 1
    n_slc_p = -(-seq // SLC_LEN)
    nbp_p = _round_up(n_slc_p, LANES)
    nq3 = r3(mp["nq"])
    oc_p, sel_p = _cmp_select(nq3, kc_p, vc_p, tq=256, sps=1, n_cmp=n_cmp_p, n_slc=n_slc_p, nbp=nbp_p, q_off=0,
                              name="cmp_select_prompt")
    assert n_slc_p <= LANES - HEAD_DIM
    q_nsa, k_slc, v_slc, k_win, v_win = _augment_nsa(nq3, sel_p, r3(mp["ks"]), r3(mp["vs"]), r3(mp["kw"]),
                                                      r3(mp["vw"]), nbp_p)
    os_p = _attention("slc", q_nsa, k_slc, v_slc, tq=tq, tk=tk, q_off=0, k_off=0, name="slc_prompt")
    ow_p = _attention("win", q_nsa, k_win, v_win, tq=256, tk=256, q_off=0, k_off=0, name="win_prompt")
    flat = lambda a: a.reshape(n, a.shape[-1])
    xp2 = _merge_ln(xp, flat(fox_o), flat(oc_p), flat(os_p), flat(ow_p), mp["ng"], mp["gf"], mp["gn"],
                    w_up_fox[layer], w_up_nsa[layer], w_out[layer], ln2_g[layer], ln2_b[layer], alpha, "merge_prompt")
    yp = _ffn_ln(xp2, ffn2_w_up[layer], ffn2_w_down[layer], ln3_g[layer], ln3_b[layer], alpha, "ffn2_prompt")

    ns_rows = sb * n_new
    xs = _ffn_ln(x_sample.reshape(ns_rows, d), ffn1_w_up[layer], ffn1_w_down[layer], ln1_g[layer], ln1_b[layer],
                 alpha, "ffn1_sample")
    cos_s, sin_s = _rope_tables(past + jnp.arange(n_new, dtype=jnp.int32))
    tile_rows = _pick(ns_rows, (256, 128, 8))
    reps = tile_rows // n_new
    ms = _in_proj(xs, ws, bf, jnp.tile(cos_s, (reps, 1)), jnp.tile(sin_s, (reps, 1)), "in_proj_sample")
    s3 = lambda a: a.reshape(sb, n_new, a.shape[-1])
    by_head = lambda a: s3(a).reshape(sb, n_new, FOX_HEADS, HEAD_DIM).transpose(0, 2, 1, 3)

    lf_past = _lf_pages(page_table, _rows_minor(cache_fox_logf[layer]))
    lf_new = s3(ms["lf"]).transpose(0, 2, 1)
    t_cs = _round_up(past + n_new, 8 * LANES)
    c_s = _cumsum_time(jnp.pad(jnp.concatenate([lf_past, lf_new], axis=2), ((0, 0), (0, 0), (0, t_cs - past - n_new))),
                       "cumsum_sample") * LOG2E
    c_new = c_s[:, :, past:past + n_new]
    fox_o_s = _fox_sample(page_table, _rows_minor(cache_fox_kv[layer]), by_head(ms["fq"]), by_head(ms["fk"]),
                          by_head(ms["fv"]), c_new[..., None], c_s, c_new)
    fox_o_s = fox_o_s.transpose(0, 2, 1, 3).reshape(ns_rows, FOX_WIDTH)

    nsa_t = _rows_minor(cache_nsa_kv[layer])
    kc_s, vc_s = _compress_pages(page_table, nsa_t, cmp_k, cmp_v)
    n_cmp_s = (past + n_new) // CMP_STRIDE - 1
    n_slc_s = -(-(past + n_new) // SLC_LEN)
    nbp_s = _round_up(n_slc_s, LANES)
    nq_s = s3(ms["nq"])
    sps = _pick(sb, tuple(c for c in (16, 8, 4, 2, 1) if c * n_new <= LANES))
    oc_s, sel_s = _cmp_select(nq_s, kc_s, vc_s, tq=n_new, sps=sps, n_cmp=n_cmp_s, n_slc=n_slc_s, nbp=nbp_s,
                              q_off=past, name="cmp_select_sample")
    q_ghtd = nq_s.reshape(sb, n_new, NSA_HPG, NSA_GROUPS, HEAD_DIM).transpose(0, 3, 2, 1, 4)
    zero = jnp.zeros_like(q_ghtd[:, 0])
    q_rows = jnp.concatenate([jnp.concatenate([q_ghtd[:, 0], zero], axis=-1),
                              jnp.concatenate([zero, q_ghtd[:, 1]], axis=-1)], axis=1).reshape(sb, -1, LANES)
    o_rows = _slc_sample(page_table, nsa_t, q_rows, sel_s, s3(ms["ks"]), s3(ms["vs"]), nbp=nbp_s, n_slc=n_slc_s)
    o_rows = o_rows.reshape(sb, NSA_GROUPS, NSA_HPG, n_new, LANES)
    os_s = jnp.stack([o_rows[:, 0, :, :, :HEAD_DIM], o_rows[:, 1, :, :, HEAD_DIM:]], axis=3)
    os_s = os_s.transpose(0, 2, 1, 3, 4).reshape(sb, n_new, NSA_WIDTH)
    t_win = _round_up(keep + n_new, LANES)
    kw_all, vw_all = _win_rows(_rows_minor(state_win_kv[layer]), s3(ms["kw"]), s3(ms["vw"]), t_win)
    ow_s = _attention("win", _augment(_head_major(nq_s, NSA_HEADS), []),
                      _augment(_head_major(kw_all, NSA_GROUPS), []), _augment(_head_major(vw_all, NSA_GROUPS), [1.0]),
                      tq=n_new, tk=t_win, q_off=past, k_off=past - keep, name="win_sample")
    flat_s = lambda a: a.reshape(ns_rows, a.shape[-1])
    xs2 = _merge_ln(xs, fox_o_s, flat_s(oc_s), flat_s(os_s), flat_s(ow_s), ms["ng"], ms["gf"], ms["gn"],
                    w_up_fox[layer], w_up_nsa[layer], w_out[layer], ln2_g[layer], ln2_b[layer], alpha, "merge_sample")
    ys = _ffn_ln(xs2, ffn2_w_up[layer], ffn2_w_down[layer], ln3_g[layer], ln3_b[layer], alpha, "ffn2_sample")

    fox_kv_p = mp["fkv"].reshape(1, bsz, seq, 2, FOX_HEADS, HEAD_DIM)
    fox_kv_s = ms["fkv"].reshape(1, sb, n_new, 2, FOX_HEADS, HEAD_DIM)
    logf_p = mp["lf"].reshape(1, bsz, seq, FOX_HEADS)
    logf_s = ms["lf"].reshape(1, sb, n_new, FOX_HEADS)
    nsa_kv_p = mp["nkv"].reshape(1, bsz, seq, 4, NSA_GROUPS, HEAD_DIM)
    nsa_kv_s = ms["nkv"].reshape(1, sb, n_new, 4, NSA_GROUPS, HEAD_DIM)
    win_rows_p = r3(mp["win"]).reshape(bsz, seq, 2, NSA_GROUPS, HEAD_DIM)
    if seq >= keep:
        win_p = win_rows_p[:, seq - keep:]
    else:
        win_p = jnp.pad(win_rows_p, ((0, 0), (keep - seq, 0), (0, 0), (0, 0), (0, 0)))
    new_win = ms["win"].reshape(sb, n_new, 2, NSA_GROUPS, HEAD_DIM).astype(state_win_kv.dtype)
    win_s = jnp.concatenate([state_win_kv[layer], new_win], axis=1)[:, -keep:]
    return (yp.reshape(bsz, seq, d), ys.reshape(sb, n_new, d), fox_kv_p, fox_kv_s, logf_p, logf_s,
            nsa_kv_p, nsa_kv_s, win_p[None], win_s[None])
```

```python
import functools

import numpy as np
import jax
import jax.numpy as jnp
from jax import lax
from jax.experimental import pallas as pl
from jax.experimental.pallas import tpu as pltpu

F32 = jnp.float32
BF16 = jnp.bfloat16

HEAD_DIM = 64
FOX_HEADS = 8
NSA_HEADS = 8
NSA_GROUPS = 2
NSA_HPG = NSA_HEADS // NSA_GROUPS
FOX_WIDTH = FOX_HEADS * HEAD_DIM
NSA_WIDTH = NSA_HEADS * HEAD_DIM
NSA_KV_WIDTH = NSA_GROUPS * HEAD_DIM
N_NSA_BRANCHES = 3
CMP_LEN = 32
CMP_STRIDE = 16
CMP_HIDDEN = 128
SLC_LEN = 64
N_SELECT = 16
WINDOW = 512
PAGE_SIZE = 128
ROPE_THETA = 10000.0
LN_EPS = 1e-5
FORCE_BONUS = 1e4
NEG_INF = -1e30
M_INIT = 0.1 * NEG_INF
LOG2E = 1.4426950408889634
QSCALE = HEAD_DIM ** -0.5 * LOG2E
SUM_LANE = HEAD_DIM

LANES = 128
N_PAIRS = 4
MIB = 1024 * 1024
HIGHEST = lax.Precision.HIGHEST
NT_DIMS = (((1,), (1,)), ((), ()))


def _params(semantics, vmem_mib):
    return pltpu.CompilerParams(dimension_semantics=semantics, vmem_limit_bytes=vmem_mib * MIB)


def _pick(n, candidates):
    for c in candidates:
        if n % c == 0:
            return c
    return n


def _round_up(n, m):
    return (n + m - 1) // m * m


def _layer_norm(y, g, b):
    mu = jnp.mean(y, axis=-1, keepdims=True)
    d = y - mu
    var = jnp.mean(d * d, axis=-1, keepdims=True)
    return d * lax.rsqrt(var + LN_EPS) * g + b


def _const_spec(shape):
    nd = len(shape)
    return pl.BlockSpec(shape, lambda *_: (0,) * nd, pipeline_mode=pl.Buffered(1))


def _ffn_ln_kernel(x_ref, wa_ref, wb_ref, wd_ref, g_ref, b_ref, o_ref, acc_ref, *, alpha, n_chunks):
    x = x_ref[...]
    xb = x.astype(BF16)
    acc_ref[...] = jnp.zeros_like(acc_ref)

    def body(c, carry):
        a = jnp.dot(xb, wa_ref[c], preferred_element_type=F32)
        b = jnp.dot(xb, wb_ref[c], preferred_element_type=F32)
        h = (a * jax.nn.sigmoid(a) * b).astype(BF16)
        acc_ref[...] += jnp.dot(h, wd_ref[c], preferred_element_type=F32)
        return carry

    lax.fori_loop(0, n_chunks, body, 0)
    o_ref[...] = _layer_norm(alpha * x + 0.5 * acc_ref[...], g_ref[...], b_ref[...])


def _ffn_ln(x, w_up, w_down, g, b, alpha, name):
    n, d = x.shape
    f = w_down.shape[0]
    fc = _pick(f, (256, 128))
    nc = f // fc
    wa = w_up[:, :f].astype(BF16).reshape(d, nc, fc).transpose(1, 0, 2)
    wb = w_up[:, f:].astype(BF16).reshape(d, nc, fc).transpose(1, 0, 2)
    wd = w_down.astype(BF16).reshape(nc, fc, d)
    tm = _pick(n, (1024, 512, 256, 128, 8))
    return pl.pallas_call(
        functools.partial(_ffn_ln_kernel, alpha=alpha, n_chunks=nc),
        grid=(n // tm,),
        in_specs=[pl.BlockSpec((tm, d), lambda i: (i, 0)),
                  _const_spec((nc, d, fc)), _const_spec((nc, d, fc)), _const_spec((nc, fc, d)),
                  _const_spec((1, d)), _const_spec((1, d))],
        out_specs=pl.BlockSpec((tm, d), lambda i: (i, 0)),
        out_shape=jax.ShapeDtypeStruct((n, d), F32),
        scratch_shapes=[pltpu.VMEM((tm, d), F32)],
        compiler_params=_params(("parallel",), 56),
        name=name,
    )(x, wa, wb, wd, g.reshape(1, d), b.reshape(1, d))


def _rope_tables(pos):
    half = HEAD_DIM // 2
    inv_freq = ROPE_THETA ** (-jnp.arange(half, dtype=F32) / half)
    ang = pos.astype(F32)[:, None] * inv_freq[None, :]
    cos, sin = jnp.cos(ang), jnp.sin(ang)
    cos64 = jnp.concatenate([cos, cos], axis=-1)
    sin64 = jnp.concatenate([-sin, sin], axis=-1)
    return jnp.tile(cos64, (1, LANES // HEAD_DIM)), jnp.tile(sin64, (1, LANES // HEAD_DIM))


def _in_proj_kernel(x_ref, wfq, wfkv, wsm, wnq, wnkv, wwin, wgf, wgn, bf_ref, cos_ref, sin_ref,
                    fq_o, fkv_o, fk_o, fv_o, lf_o, nq_o, nkv_o, ks_o, vs_o, win_o, kw_o, vw_o,
                    ng_o, gf_o, gn_o):
    xb = x_ref[...].astype(BF16)
    cos = cos_ref[...]
    sin = sin_ref[...]
    lane = lax.broadcasted_iota(jnp.int32, (1, LANES), 1)
    low_half = (lane % HEAD_DIM) < (HEAD_DIM // 2)

    def rope(v):
        partner = jnp.where(low_half, pltpu.roll(v, LANES - HEAD_DIM // 2, 1), pltpu.roll(v, HEAD_DIM // 2, 1))
        return v * cos + partner * sin

    def proj(w_ref):
        return jnp.dot(xb, w_ref[...], preferred_element_type=F32)

    fq_o[...] = (proj(wfq) * QSCALE).astype(BF16)
    fkv = proj(wfkv)
    fkv_o[...] = fkv
    fk_o[...] = fkv[:, :FOX_WIDTH].astype(BF16)
    fv_o[...] = fkv[:, FOX_WIDTH:].astype(BF16)

    sm = proj(wsm)
    z = sm[:, :LANES] + bf_ref[...]
    logf = jnp.minimum(z, 0.0) - jnp.log1p(jnp.exp(-jnp.abs(z)))
    lf_o[...] = logf[:, :FOX_HEADS]
    ng_o[...] = sm[:, LANES:]

    nq = proj(wnq)
    for r in range(N_PAIRS):
        nq_o[:, r * LANES:(r + 1) * LANES] = (rope(nq[:, r * LANES:(r + 1) * LANES]) * QSCALE).astype(BF16)

    nkv = proj(wnkv)
    k_cmp = rope(nkv[:, 0:LANES])
    k_slc = rope(nkv[:, 2 * LANES:3 * LANES])
    v_slc = nkv[:, 3 * LANES:4 * LANES]
    nkv_o[:, 0:LANES] = k_cmp
    nkv_o[:, LANES:2 * LANES] = nkv[:, LANES:2 * LANES]
    nkv_o[:, 2 * LANES:3 * LANES] = k_slc
    nkv_o[:, 3 * LANES:4 * LANES] = v_slc
    ks_o[...] = k_slc.astype(BF16)
    vs_o[...] = v_slc.astype(BF16)

    win = proj(wwin)
    k_win = rope(win[:, :LANES])
    v_win = win[:, LANES:]
    win_o[:, :LANES] = k_win
    win_o[:, LANES:] = v_win
    kw_o[...] = k_win.astype(BF16)
    vw_o[...] = v_win.astype(BF16)

    gf_o[...] = proj(wgf)
    gn_o[...] = proj(wgn)


def _nsa_perm():
    perm = np.zeros(NSA_WIDTH, np.int32)
    for r in range(NSA_HPG):
        for g in range(NSA_GROUPS):
            for d in range(HEAD_DIM):
                perm[r * LANES + g * HEAD_DIM + d] = (g * NSA_HPG + r) * HEAD_DIM + d
    return perm


def _split_w_in(w_in, b_fgate):
    d = w_in.shape[0]
    sizes = (FOX_WIDTH, FOX_WIDTH, FOX_WIDTH, FOX_HEADS, NSA_WIDTH) + (NSA_KV_WIDTH,) * 6 + (
        N_NSA_BRANCHES * NSA_HEADS, d, d)
    offs = np.concatenate([[0], np.cumsum(sizes)])
    col = lambda i, j=None: w_in[:, offs[i]:offs[(i if j is None else j) + 1]]
    pad = lambda w: jnp.pad(w, ((0, 0), (0, LANES - w.shape[1])))
    ws = dict(
        wfq=col(0), wfkv=col(1, 2),
        wsm=jnp.concatenate([pad(col(3)), pad(col(11))], axis=1),
        wnq=col(4)[:, _nsa_perm()], wnkv=col(5, 8), wwin=col(9, 10), wgf=col(12), wgn=col(13))
    ws = {k: v.astype(BF16) for k, v in ws.items()}
    bf = jnp.pad(b_fgate.astype(F32), (0, LANES - FOX_HEADS)).reshape(1, LANES)
    return ws, bf


def _in_proj(x, ws, bf, cos_tab, sin_tab, name):
    n, d = x.shape
    n_tab = cos_tab.shape[0]
    tm = _pick(n_tab, (256, 128, 8))
    tab_tiles = n_tab // tm
    row = lambda w: pl.BlockSpec((tm, w), lambda i: (i, 0))
    tab = pl.BlockSpec((tm, LANES), lambda i: (i % tab_tiles, 0))
    names = ("wfq", "wfkv", "wsm", "wnq", "wnkv", "wwin", "wgf", "wgn")
    outs = [("fq", FOX_WIDTH, BF16), ("fkv", 2 * FOX_WIDTH, F32), ("fk", FOX_WIDTH, BF16), ("fv", FOX_WIDTH, BF16),
            ("lf", FOX_HEADS, F32), ("nq", NSA_WIDTH, BF16), ("nkv", 4 * NSA_KV_WIDTH, F32),
            ("ks", NSA_KV_WIDTH, BF16), ("vs", NSA_KV_WIDTH, BF16), ("win", 2 * NSA_KV_WIDTH, F32),
            ("kw", NSA_KV_WIDTH, BF16), ("vw", NSA_KV_WIDTH, BF16), ("ng", LANES, F32), ("gf", d, F32), ("gn", d, F32)]
    res = pl.pallas_call(
        _in_proj_kernel,
        grid=(n // tm,),
        in_specs=[row(d)] + [_const_spec(ws[k].shape) for k in names] + [_const_spec((1, LANES)), tab, tab],
        out_specs=[row(w) for _, w, _ in outs],
        out_shape=[jax.ShapeDtypeStruct((n, w), dt) for _, w, dt in outs],
        compiler_params=_params(("parallel",), 56),
        name=name,
    )(x, *[ws[k] for k in names], bf, cos_tab, sin_tab)
    return {k: v for (k, _, _), v in zip(outs, res)}


def _cumsum_kernel(x_ref, tri_ref, low_ref, o_ref):
    x = x_ref[0]
    within = lax.dot_general(x, tri_ref[...], (((1,), (0,)), ((), ())), precision=HIGHEST, preferred_element_type=F32)
    tot = jnp.broadcast_to(within[:, LANES - 1:LANES], within.shape)
    before = lax.dot_general(low_ref[...], tot, (((1,), (0,)), ((), ())), precision=HIGHEST, preferred_element_type=F32)
    o_ref[0] = within + before


def _cumsum_time(logf_t, name):
    b, h, t = logf_t.shape
    r = t // LANES
    rows = h * r
    idx = np.arange(LANES)
    tri = jnp.asarray((idx[:, None] <= idx[None, :]).astype(np.float32))
    ridx = np.arange(rows)
    low = jnp.asarray(((ridx[None, :] < ridx[:, None]) & (ridx[None, :] // r == ridx[:, None] // r)).astype(np.float32))
    out = pl.pallas_call(
        _cumsum_kernel,
        grid=(b,),
        in_specs=[pl.BlockSpec((1, rows, LANES), lambda i: (i, 0, 0)), _const_spec((LANES, LANES)),
                  _const_spec((rows, rows))],
        out_specs=pl.BlockSpec((1, rows, LANES), lambda i: (i, 0, 0)),
        out_shape=jax.ShapeDtypeStruct((b, rows, LANES), F32),
        compiler_params=_params(("parallel",), 32),
        name=name,
    )(logf_t.reshape(b, rows, LANES), tri, low)
    return out.reshape(b, h, t)


FLAG_FIRST, FLAG_LAST, FLAG_EDGE = 1, 2, 4


def _tile_schedule(mode, nq, nkt, tq, tk, q_off, k_off):
    qi_tab, kt_tab, flag_tab = [], [], []
    for qi in range(nq):
        q_lo, q_hi = q_off + qi * tq, q_off + (qi + 1) * tq - 1
        tiles = []
        for kt in range(nkt):
            k_lo, k_hi = k_off + kt * tk, k_off + (kt + 1) * tk - 1
            if mode == "win":
                visible = (q_hi - k_lo >= 0) and (q_lo - k_hi < WINDOW)
                interior = (q_lo - k_hi >= 0) and (q_hi - k_lo < WINDOW)
            else:
                visible = k_lo <= q_hi
                interior = k_hi <= q_lo
            if visible:
                tiles.append((kt, 0 if interior else FLAG_EDGE))
        assert tiles, "every query tile sees at least one key tile"
        for n, (kt, flag) in enumerate(tiles):
            qi_tab.append(qi)
            kt_tab.append(kt)
            flag_tab.append(flag | (FLAG_FIRST if n == 0 else 0) | (FLAG_LAST if n == len(tiles) - 1 else 0))
    as_i32 = lambda v: jnp.asarray(np.asarray(v, np.int32))
    return as_i32(qi_tab), as_i32(kt_tab), as_i32(flag_tab)


def _attn_kernel(qi_tab, kt_tab, flag_tab, q_ref, k_ref, v_ref, o_ref, m_ref, acc_ref, *,
                 mode, tq, tk, nh, n_kv, q_off, k_off):
    step = pl.program_id(1)
    qi = qi_tab[step]
    kt = kt_tab[step]
    flags = flag_tab[step]
    lane = lax.broadcasted_iota(jnp.int32, (1, LANES), 1)
    n_chunks = tk // LANES

    @pl.when((flags & FLAG_FIRST) != 0)
    def _():
        m_ref[...] = jnp.full_like(m_ref, M_INIT)
        acc_ref[...] = jnp.zeros_like(acc_ref)

    def tile(edge):
        if edge:
            q_pos = q_off + qi * tq + lax.broadcasted_iota(jnp.int32, (tq, 1), 0)
        scores = lambda i: lax.dot_general(q_ref[0, i], k_ref[0, i % n_kv], NT_DIMS, preferred_element_type=F32)

        def values(i, alpha, pb):
            acc_ref[i] = alpha * acc_ref[i] + jnp.dot(pb, v_ref[0, i % n_kv], preferred_element_type=F32)

        s_next = scores(0)
        pending = None
        for i in range(nh):
            s = s_next
            if i + 1 < nh:
                s_next = scores(i + 1)
            if pending is not None:
                values(*pending)
            chunks = []
            for c in range(n_chunks):
                t = s[:, c * LANES:(c + 1) * LANES]
                if edge:
                    k_pos = k_off + kt * tk + c * LANES + lane
                    if mode == "win":
                        rel = q_pos - k_pos
                        t = jnp.where((rel >= 0) & (rel < WINDOW), t, NEG_INF)
                    else:
                        t = jnp.where(k_pos <= q_pos, t, NEG_INF)
                chunks.append(t)
            mx = chunks[0]
            for t in chunks[1:]:
                mx = jnp.maximum(mx, t)
            m_prev = m_ref[i]
            m_new = jnp.maximum(m_prev, jnp.max(mx, axis=1, keepdims=True))
            alpha = jnp.exp2(m_prev - m_new)
            ps = [jnp.exp2(t - m_new).astype(BF16) for t in chunks]
            pb = jnp.concatenate(ps, axis=1) if n_chunks > 1 else ps[0]
            m_ref[i] = m_new
            pending = (i, alpha, pb)
        values(*pending)

    pl.when((flags & FLAG_EDGE) != 0)(lambda: tile(True))
    pl.when((flags & FLAG_EDGE) == 0)(lambda: tile(False))

    @pl.when((flags & FLAG_LAST) != 0)
    def _():
        for pr in range(nh // 2):
            outs = []
            for side in range(2):
                acc = acc_ref[2 * pr + side]
                outs.append(acc / jnp.maximum(acc[:, SUM_LANE:SUM_LANE + 1], 1e-30))
            o_ref[0, :, pr * LANES:(pr + 1) * LANES] = jnp.where(lane < HEAD_DIM, outs[0],
                                                                  pltpu.roll(outs[1], HEAD_DIM, 1))


def _attention(mode, q, k, v, *, tq, tk, q_off, k_off, name):
    b, nh, t_q, _ = q.shape
    n_kv, t_k = k.shape[1], k.shape[2]
    qi_tab, kt_tab, flag_tab = _tile_schedule(mode, t_q // tq, t_k // tk, tq, tk, q_off, k_off)
    n_steps = qi_tab.shape[0]
    q_spec = pl.BlockSpec((1, nh, tq, LANES), lambda bi, s, qt, kt, fl: (bi, 0, qt[s], 0))
    kv_spec = pl.BlockSpec((1, n_kv, tk, LANES), lambda bi, s, qt, kt, fl: (bi, 0, kt[s], 0))
    grid_spec = pltpu.PrefetchScalarGridSpec(
        num_scalar_prefetch=3, grid=(b, n_steps), in_specs=[q_spec, kv_spec, kv_spec],
        out_specs=pl.BlockSpec((1, tq, nh // 2 * LANES), lambda bi, s, qt, kt, fl: (bi, qt[s], 0)),
        scratch_shapes=[pltpu.VMEM((nh, tq, LANES), F32), pltpu.VMEM((nh, tq, LANES), F32)])
    return pl.pallas_call(
        functools.partial(_attn_kernel, mode=mode, tq=tq, tk=tk, nh=nh, n_kv=n_kv, q_off=q_off, k_off=k_off),
        grid_spec=grid_spec,
        out_shape=jax.ShapeDtypeStruct((b, t_q, nh // 2 * LANES), F32),
        compiler_params=_params(("parallel", "arbitrary"), 48),
        name=name,
    )(qi_tab, kt_tab, flag_tab, q, k, v)


def _limbs(x):
    hi = x.astype(BF16).astype(F32)
    mid = (x - hi).astype(BF16).astype(F32)
    lo = (x - hi - mid).astype(BF16).astype(F32)
    return hi, mid, lo


def _head_of_pair(pair, side):
    return pair if side == 0 else pltpu.roll(pair, HEAD_DIM, 1)


def _augment_fox_kernel(q_ref, k_ref, v_ref, c_ref, qa_ref, ka_ref, va_ref):
    lane = lax.broadcasted_iota(jnp.int32, (1, LANES), 1)
    data = lane < HEAD_DIM
    c = c_ref[0] * LOG2E
    for h in range(FOX_HEADS):
        p, side = divmod(h, 2)
        hi, mid, lo = _limbs(c[:, h:h + 1])
        q_extra = jnp.where(lane < HEAD_DIM + 3, 1.0,
                            jnp.where(lane == HEAD_DIM + 3, hi,
                                      jnp.where(lane == HEAD_DIM + 4, mid, jnp.where(lane == HEAD_DIM + 5, lo, 0.0))))
        k_extra = jnp.where(lane == HEAD_DIM, -hi,
                            jnp.where(lane == HEAD_DIM + 1, -mid,
                                      jnp.where(lane == HEAD_DIM + 2, -lo, jnp.where(lane < HEAD_DIM + 6, 1.0, 0.0))))
        v_extra = jnp.where(lane == SUM_LANE, 1.0, 0.0)
        cols = slice(p * LANES, (p + 1) * LANES)
        qa_ref[0, h] = jnp.where(data, _head_of_pair(q_ref[0, :, cols].astype(F32), side), q_extra).astype(BF16)
        ka_ref[0, h] = jnp.where(data, _head_of_pair(k_ref[0, :, cols].astype(F32), side), k_extra).astype(BF16)
        va_ref[0, h] = jnp.where(data, _head_of_pair(v_ref[0, :, cols].astype(F32), side), v_extra).astype(BF16)


def _augment_fox(q, k, v, c_col):
    b, t, w = q.shape
    tr = _pick(t, (512, 256, 128, 8))
    row = pl.BlockSpec((1, tr, w), lambda bi, i: (bi, i, 0))
    out = pl.BlockSpec((1, FOX_HEADS, tr, LANES), lambda bi, i: (bi, 0, i, 0))
    shape = jax.ShapeDtypeStruct((b, FOX_HEADS, t, LANES), BF16)
    return pl.pallas_call(
        _augment_fox_kernel,
        grid=(b, t // tr),
        in_specs=[row, row, row, pl.BlockSpec((1, tr, FOX_HEADS), lambda bi, i: (bi, i, 0))],
        out_specs=[out, out, out],
        out_shape=[shape, shape, shape],
        compiler_params=_params(("parallel", "parallel"), 32),
        name="augment_fox",
    )(q, k, v, c_col)


def _augment_nsa_kernel(nq_ref, sel_ref, ks_ref, vs_ref, kw_ref, vw_ref, qa_ref, ksa_ref, vsa_ref, kwa_ref, vwa_ref,
                        *, tr, nbp):
    lane = lax.broadcasted_iota(jnp.int32, (1, LANES), 1)
    data = lane < HEAD_DIM
    t_row = pl.program_id(1) * tr + lax.broadcasted_iota(jnp.int32, (tr, 1), 0)
    block_hot = jnp.where(t_row // SLC_LEN == lane - HEAD_DIM, 1.0, 0.0)
    ones_lane = jnp.where(lane == SUM_LANE, 1.0, 0.0)
    for g in range(NSA_GROUPS):
        bias = pltpu.roll(sel_ref[0, :, g * nbp:g * nbp + LANES].astype(F32), HEAD_DIM, 1)
        for r in range(NSA_HPG):
            qp = nq_ref[0, :, r * LANES:(r + 1) * LANES].astype(F32)
            qa_ref[0, NSA_GROUPS * r + g] = jnp.where(data, _head_of_pair(qp, g), bias).astype(BF16)
        ksa_ref[0, g] = jnp.where(data, _head_of_pair(ks_ref[0].astype(F32), g), block_hot).astype(BF16)
        vsa_ref[0, g] = jnp.where(data, _head_of_pair(vs_ref[0].astype(F32), g), ones_lane).astype(BF16)
        kwa_ref[0, g] = jnp.where(data, _head_of_pair(kw_ref[0].astype(F32), g), 0.0).astype(BF16)
        vwa_ref[0, g] = jnp.where(data, _head_of_pair(vw_ref[0].astype(F32), g), ones_lane).astype(BF16)


def _augment_nsa(nq, sel, ks, vs, kw, vw, nbp):
    b, t, w = nq.shape
    tr = _pick(t, (512, 256, 128, 8))
    row = lambda width: pl.BlockSpec((1, tr, width), lambda bi, i: (bi, i, 0))
    out = lambda heads: pl.BlockSpec((1, heads, tr, LANES), lambda bi, i: (bi, 0, i, 0))
    shape = lambda heads: jax.ShapeDtypeStruct((b, heads, t, LANES), BF16)
    return pl.pallas_call(
        functools.partial(_augment_nsa_kernel, tr=tr, nbp=nbp),
        grid=(b, t // tr),
        in_specs=[row(w), row(NSA_GROUPS * nbp)] + [row(LANES)] * 4,
        out_specs=[out(NSA_HEADS)] + [out(NSA_GROUPS)] * 4,
        out_shape=[shape(NSA_HEADS)] + [shape(NSA_GROUPS)] * 4,
        compiler_params=_params(("parallel", "parallel"), 32),
        name="augment_nsa",
    )(nq, sel, ks, vs, kw, vw)


def _head_major(a, n_heads):
    b, t, _ = a.shape
    return a.reshape(b, t, n_heads, HEAD_DIM).transpose(0, 2, 1, 3)


def _augment(head_rows, extras):
    b, h, t, _ = head_rows.shape
    cols = [jnp.broadcast_to(e, (b, h, t))[..., None].astype(BF16) for e in extras]
    pad = jnp.zeros((b, h, t, LANES - HEAD_DIM - len(cols)), BF16)
    return jnp.concatenate([head_rows] + cols + [pad], axis=-1)


def _compress_rows(row_of_half, pa_ref, pb_ref, wa_ref, wb_ref, w2_ref, ns):
    u = jnp.concatenate([row_of_half(r) for r in range(CMP_STRIDE)], axis=1)
    first = jnp.dot((u + pa_ref[...]).astype(BF16), wa_ref[...], preferred_element_type=F32)
    second = jnp.dot((u + pb_ref[...]).astype(BF16), wb_ref[...], preferred_element_type=F32)
    pre = first + pltpu.roll(second, ns - 1, 0)
    h = (pre * jax.nn.sigmoid(pre)).astype(BF16)
    return jnp.dot(h, w2_ref[...], preferred_element_type=F32).astype(BF16)


def _compress_kernel(rows_ref, pa_ref, pb_ref, wa_ref, wb_ref, w2_ref, o_ref, *, ns):
    o_ref[0] = _compress_rows(lambda r: rows_ref[0, pl.ds(r, ns, stride=CMP_STRIDE), :],
                              pa_ref, pb_ref, wa_ref, wb_ref, w2_ref, ns)


def _compress_weights(pos, w1, w2):
    ratio = CMP_LEN // CMP_STRIDE
    assert ratio == 2
    w1r = w1.reshape(CMP_LEN, HEAD_DIM, CMP_HIDDEN)
    zeros = jnp.zeros((CMP_STRIDE, HEAD_DIM, CMP_HIDDEN), w1.dtype)

    def half(rows):
        g0 = jnp.concatenate([jnp.stack([rows, zeros], axis=1).reshape(-1, CMP_HIDDEN),
                              jnp.stack([zeros, rows], axis=1).reshape(-1, CMP_HIDDEN)], axis=1)
        return g0.astype(BF16)

    def pos_tab(p):
        return jnp.stack([p, p], axis=1).reshape(1, -1).astype(F32)

    z2 = jnp.zeros_like(w2)
    w2d = jnp.concatenate([jnp.concatenate([w2, z2], axis=1), jnp.concatenate([z2, w2], axis=1)], axis=0)
    return (pos_tab(pos[:CMP_STRIDE]), pos_tab(pos[CMP_STRIDE:]), half(w1r[:CMP_STRIDE]), half(w1r[CMP_STRIDE:]),
            w2d.astype(BF16))


def _compress(rows, col_block, weights, name):
    b, t, _ = rows.shape
    ns = t // CMP_STRIDE
    width = CMP_STRIDE * LANES
    pa, pb, wa, wb, w2d = weights
    return pl.pallas_call(
        functools.partial(_compress_kernel, ns=ns),
        grid=(b,),
        in_specs=[pl.BlockSpec((1, t, LANES), lambda i: (i, 0, col_block)), _const_spec((1, width)),
                  _const_spec((1, width)), _const_spec(wa.shape), _const_spec(wb.shape), _const_spec(w2d.shape)],
        out_specs=pl.BlockSpec((1, ns, LANES), lambda i: (i, 0, 0)),
        out_shape=jax.ShapeDtypeStruct((b, ns, LANES), BF16),
        compiler_params=_params(("parallel",), 48),
        name=name,
    )(rows, pa, pb, wa, wb, w2d)


def _cmp_select_kernel(q_ref, kc_ref, vc_ref, cov_ref, oc_ref, sel_ref, imp_ref, vt_ref, *,
                       sps, tq, rows_p, nc, n_cmp, n_slc, nbp, q_off, n_sel):
    qi = pl.program_id(1)
    lane = lax.broadcasted_iota(jnp.int32, (1, LANES), 1)
    left = lane < HEAD_DIM
    q_pos = q_off + qi * tq + lax.broadcasted_iota(jnp.int32, (tq, 1), 0)
    n_idx = lax.broadcasted_iota(jnp.int32, (1, nc), 1)
    cmask = ((n_idx * CMP_STRIDE + CMP_LEN - 1) <= q_pos) & (n_idx < n_cmp)
    cov = cov_ref[...]
    if rows_p > sps * tq:
        imp_ref[...] = jnp.zeros_like(imp_ref)

    def attend(sq, carry):
        kc = kc_ref[sq]
        vc = vc_ref[sq]
        imp = [jnp.zeros((tq, nbp), F32) for _ in range(NSA_GROUPS)]
        heads = [(r, g) for r in range(N_PAIRS) for g in range(NSA_GROUPS)]

        def scores(r, g):
            qp = q_ref[sq, :, r * LANES:(r + 1) * LANES]
            qh = jnp.where(left if g == 0 else jnp.logical_not(left), qp, jnp.zeros_like(qp))
            return lax.dot_general(qh, kc, NT_DIMS, preferred_element_type=F32)

        halves = []
        s_next = scores(*heads[0])
        for n, (r, g) in enumerate(heads):
            s = s_next
            if n + 1 < len(heads):
                s_next = scores(*heads[n + 1])
            s = jnp.where(cmask, s, NEG_INF)
            m = jnp.maximum(jnp.max(s, axis=1, keepdims=True), M_INIT)
            e = jnp.exp2(s - m)
            inv = 1.0 / jnp.maximum(jnp.sum(e, axis=1, keepdims=True), 1e-30)
            pb = (e * inv).astype(BF16)
            halves.append(jnp.dot(pb, vc, preferred_element_type=F32))
            imp[g] = imp[g] + jnp.dot(pb, cov, preferred_element_type=F32)
            if g == NSA_GROUPS - 1:
                oc_ref[sq, :, r * LANES:(r + 1) * LANES] = jnp.where(left, halves[-2], halves[-1])
        first_row = pl.multiple_of(sq * tq, tq)
        for g in range(NSA_GROUPS):
            imp_ref[g, pl.ds(first_row, tq), :] = imp[g]
        return carry

    if sps == 1:
        attend(0, 0)
    else:
        lax.fori_loop(0, sps, attend, 0)

    pos = q_off + qi * tq + lax.broadcasted_iota(jnp.int32, (rows_p, 1), 0) % tq
    blk = lax.broadcasted_iota(jnp.int32, (1, nbp), 1)
    cur = pos // SLC_LEN
    forced = (blk == 0) | (blk == cur) | (blk == cur - 1)
    valid = (blk * SLC_LEN <= pos) & (blk < n_slc)
    nbr = _round_up(n_slc, 8)
    row = lax.broadcasted_iota(jnp.int32, (nbr, rows_p), 0)
    n_live = jnp.minimum(n_slc, (q_off + (qi + 1) * tq - 1) // SLC_LEN + 1)
    for g in range(NSA_GROUPS):
        val = jnp.where(valid, jnp.where(forced, imp_ref[g] + FORCE_BONUS, imp_ref[g]), NEG_INF)
        vt_ref[...] = val.T

        def body(i, cnt):
            vi = vt_ref[pl.ds(i, 1), :]
            vt = vt_ref[0:nbr, :]
            ahead = (vi > vt) | ((vi == vt) & (i < row))
            return cnt + ahead.astype(F32)

        cnt = lax.fori_loop(0, n_live, body, jnp.zeros((nbr, rows_p), F32))
        bias = jnp.where(cnt < n_sel, 0.0, NEG_INF)
        if nbp > nbr:
            bias = jnp.concatenate([bias, jnp.full((nbp - nbr, rows_p), NEG_INF, F32)], axis=0)
        bias_t = bias.T
        for sq in range(sps):
            sel_ref[sq, :, g * nbp:(g + 1) * nbp] = bias_t[sq * tq:(sq + 1) * tq].astype(BF16)


def _coverage(n_cmp, nc, n_slc, nbp):
    c0 = np.arange(nc) * CMP_STRIDE
    s0 = np.arange(nbp) * SLC_LEN
    lo = np.maximum(c0[:, None], s0[None, :])
    hi = np.minimum(c0[:, None] + CMP_LEN, s0[None, :] + SLC_LEN)
    cov = np.maximum(hi - lo, 0).astype(np.float32) / CMP_LEN
    cov[n_cmp:, :] = 0.0
    cov[:, n_slc:] = 0.0
    return jnp.asarray(cov, dtype=BF16)


def _cmp_select(q, kc, vc, *, tq, sps, n_cmp, n_slc, nbp, q_off, name):
    b, t_q, _ = q.shape
    nc = kc.shape[1]
    rows_p = _round_up(sps * tq, LANES)
    cov = _coverage(n_cmp, nc, n_slc, nbp)
    kern = functools.partial(_cmp_select_kernel, sps=sps, tq=tq, rows_p=rows_p, nc=nc, n_cmp=n_cmp, n_slc=n_slc,
                             nbp=nbp, q_off=q_off, n_sel=min(N_SELECT, n_slc))
    return pl.pallas_call(
        kern,
        grid=(b // sps, t_q // tq),
        in_specs=[pl.BlockSpec((sps, tq, N_PAIRS * LANES), lambda bi, qi: (bi, qi, 0)),
                  pl.BlockSpec((sps, nc, LANES), lambda bi, qi: (bi, 0, 0)),
                  pl.BlockSpec((sps, nc, LANES), lambda bi, qi: (bi, 0, 0)),
                  _const_spec((nc, nbp))],
        out_specs=[pl.BlockSpec((sps, tq, N_PAIRS * LANES), lambda bi, qi: (bi, qi, 0)),
                   pl.BlockSpec((sps, tq, NSA_GROUPS * nbp), lambda bi, qi: (bi, qi, 0))],
        out_shape=[jax.ShapeDtypeStruct((b, t_q, N_PAIRS * LANES), F32),
                   jax.ShapeDtypeStruct((b, t_q, NSA_GROUPS * nbp), BF16)],
        scratch_shapes=[pltpu.VMEM((NSA_GROUPS, rows_p, nbp), F32), pltpu.VMEM((nbp, rows_p), F32)],
        compiler_params=_params(("parallel", "parallel"), 40),
        name=name,
    )(q, kc, vc, cov)


def _merge_kernel(x_ref, fo_ref, oc_ref, os_ref, ow_ref, ng_ref, gf_ref, gn_ref, wuf_ref, wun_ref, wout_ref,
                  eg_ref, g_ref, b_ref, o_ref, *, alpha):
    gates = jax.nn.sigmoid(ng_ref[...])
    gx = lax.dot_general(gates, eg_ref[...], (((1,), (0,)), ((), ())), precision=HIGHEST, preferred_element_type=F32)
    w = NSA_WIDTH
    nsa_o = gx[:, :w] * oc_ref[...] + gx[:, w:2 * w] * os_ref[...] + gx[:, 2 * w:] * ow_ref[...]
    up_f = jnp.dot(fo_ref[...].astype(BF16), wuf_ref[...], preferred_element_type=F32)
    up_n = jnp.dot(nsa_o.astype(BF16), wun_ref[...], preferred_element_type=F32)
    mixed = jax.nn.sigmoid(gf_ref[...]) * up_f + jax.nn.sigmoid(gn_ref[...]) * up_n
    mix = jnp.dot(mixed.astype(BF16), wout_ref[...], preferred_element_type=F32)
    o_ref[...] = _layer_norm(alpha * x_ref[...] + mix, g_ref[...], b_ref[...])


def _gate_expand():
    perm = _nsa_perm()
    e = np.zeros((LANES, N_NSA_BRANCHES * NSA_WIDTH), np.float32)
    for br in range(N_NSA_BRANCHES):
        for pos in range(NSA_WIDTH):
            head = perm[pos] // HEAD_DIM
            e[br * NSA_HEADS + head, br * NSA_WIDTH + pos] = 1.0
    return jnp.asarray(e)


def _merge_ln(x, fox_o, o_c, o_s, o_w, ng, gf, gn, w_up_fox, w_up_nsa, w_out, g, b, alpha, name):
    n, d = x.shape
    tm = _pick(n, (256, 128, 8))
    row = lambda w: pl.BlockSpec((tm, w), lambda i: (i, 0))
    wuf = w_up_fox.astype(BF16)
    wun = w_up_nsa[_nsa_perm(), :].astype(BF16)
    wout = w_out.astype(BF16)
    eg = _gate_expand()
    return pl.pallas_call(
        functools.partial(_merge_kernel, alpha=alpha),
        grid=(n // tm,),
        in_specs=[row(d), row(FOX_WIDTH), row(NSA_WIDTH), row(NSA_WIDTH), row(NSA_WIDTH), row(LANES), row(d), row(d),
                  _const_spec(wuf.shape), _const_spec(wun.shape), _const_spec(wout.shape), _const_spec(eg.shape),
                  _const_spec((1, d)), _const_spec((1, d))],
        out_specs=row(d),
        out_shape=jax.ShapeDtypeStruct((n, d), F32),
        compiler_params=_params(("parallel",), 48),
        name=name,
    )(x, fox_o, o_c, o_s, o_w, ng, gf, gn, wuf, wun, wout, eg, g.reshape(1, d), b.reshape(1, d))


def _page_specs(block, group, n_pages, second=0):
    nd = len(block)

    def spec(slot):
        return pl.BlockSpec(block, lambda si, j, pt: (pt[si * n_pages + jnp.minimum(j * group + slot, n_pages - 1)],
                                                      second) + (0,) * (nd - 2))

    return [spec(slot) for slot in range(group)]


def _lf_pages_kernel(pt_ref, *refs, group):
    o_ref = refs[group]
    for i in range(group):
        o_ref[0, :, i * PAGE_SIZE:(i + 1) * PAGE_SIZE] = refs[i][0]


def _lf_pages(page_table, cache_lf_t):
    s, n_pages = page_table.shape
    group = _pick(n_pages, (64, 32, 16, 8, 4, 2, 1))
    grid_spec = pltpu.PrefetchScalarGridSpec(
        num_scalar_prefetch=1, grid=(s, n_pages // group),
        in_specs=_page_specs((1, FOX_HEADS, PAGE_SIZE), group, n_pages),
        out_specs=pl.BlockSpec((1, FOX_HEADS, group * PAGE_SIZE), lambda si, j, pt: (si, 0, j)))
    return pl.pallas_call(
        functools.partial(_lf_pages_kernel, group=group),
        grid_spec=grid_spec,
        out_shape=jax.ShapeDtypeStruct((s, FOX_HEADS, n_pages * PAGE_SIZE), F32),
        compiler_params=_params(("parallel", "arbitrary"), 32),
        name="lf_pages",
    )(page_table.reshape(-1), *([cache_lf_t] * group))


def _fox_sample_kernel(pt_ref, q_ref, cc_ref, cr_ref, crn_ref, kn_ref, vn_ref, *refs, group, n_steps, n_new):
    pages = refs[:group]
    o_ref, m_ref, l_ref, acc_ref = refs[group:]
    j = pl.program_id(1)

    @pl.when(j == 0)
    def _():
        m_ref[...] = jnp.full_like(m_ref, M_INIT)
        l_ref[...] = jnp.zeros_like(l_ref)
        acc_ref[...] = jnp.zeros_like(acc_ref)

    def update(s, value_dots):
        m_prev = m_ref[...]
        m_new = jnp.maximum(m_prev, jnp.max(s, axis=1, keepdims=True))
        p = jnp.exp2(s - m_new)
        alpha = jnp.exp2(m_prev - m_new)
        l_ref[...] = alpha * l_ref[...] + jnp.sum(p, axis=1, keepdims=True)
        acc_ref[...] = alpha * acc_ref[...] + jnp.concatenate(
            [value_dots(h, p[h * n_new:(h + 1) * n_new].astype(BF16)) for h in range(FOX_HEADS)], axis=0)
        m_ref[...] = m_new

    def tiles(kv, h):
        return jnp.concatenate([pages[i][0, kv, h].astype(BF16) for i in range(group)], axis=1)

    scores = [jnp.dot(q_ref[0, h], tiles(0, h), preferred_element_type=F32) + (cc_ref[0, h] - cr_ref[0, h:h + 1, :])
              for h in range(FOX_HEADS)]
    update(jnp.concatenate(scores, axis=0),
           lambda h, pb: lax.dot_general(pb, tiles(1, h), NT_DIMS, preferred_element_type=F32))

    @pl.when(j == n_steps - 1)
    def _():
        t_q = lax.broadcasted_iota(jnp.int32, (n_new, n_new), 0)
        t_k = lax.broadcasted_iota(jnp.int32, (n_new, n_new), 1)
        new_scores = []
        for h in range(FOX_HEADS):
            s = lax.dot_general(q_ref[0, h], kn_ref[0, h], NT_DIMS, preferred_element_type=F32)
            new_scores.append(jnp.where(t_k <= t_q, s + (cc_ref[0, h] - crn_ref[0, h:h + 1, :]), NEG_INF))
        update(jnp.concatenate(new_scores, axis=0),
               lambda h, pb: jnp.dot(pb, vn_ref[0, h], preferred_element_type=F32))
        o_ref[0] = acc_ref[...] / jnp.maximum(l_ref[...], 1e-30)


def _fox_sample(page_table, cache_kv_t, q_h, k_new_h, v_new_h, c_q, c_row, c_new):
    s, n_pages = page_table.shape
    n_new = q_h.shape[2]
    group = _pick(n_pages, (16, 8, 4, 2, 1))
    n_steps = n_pages // group
    rows = FOX_HEADS * n_new
    seq4 = lambda shape: pl.BlockSpec(shape, lambda si, j, pt: (si, 0, 0, 0))
    grid_spec = pltpu.PrefetchScalarGridSpec(
        num_scalar_prefetch=1, grid=(s, n_steps),
        in_specs=[seq4((1, FOX_HEADS, n_new, HEAD_DIM)), seq4((1, FOX_HEADS, n_new, 1)),
                  pl.BlockSpec((1, FOX_HEADS, group * PAGE_SIZE), lambda si, j, pt: (si, 0, j)),
                  pl.BlockSpec((1, FOX_HEADS, n_new), lambda si, j, pt: (si, 0, 0)),
                  seq4((1, FOX_HEADS, n_new, HEAD_DIM)), seq4((1, FOX_HEADS, n_new, HEAD_DIM))]
        + _page_specs((1, 2, FOX_HEADS, HEAD_DIM, PAGE_SIZE), group, n_pages),
        out_specs=pl.BlockSpec((1, rows, HEAD_DIM), lambda si, j, pt: (si, 0, 0)),
        scratch_shapes=[pltpu.VMEM((rows, 1), F32), pltpu.VMEM((rows, 1), F32), pltpu.VMEM((rows, HEAD_DIM), F32)])
    out = pl.pallas_call(
        functools.partial(_fox_sample_kernel, group=group, n_steps=n_steps, n_new=n_new),
        grid_spec=grid_spec,
        out_shape=jax.ShapeDtypeStruct((s, rows, HEAD_DIM), F32),
        compiler_params=_params(("parallel", "arbitrary"), 48),
        name="fox_sample",
    )(page_table.reshape(-1), q_h, c_q, c_row, c_new, k_new_h, v_new_h, *([cache_kv_t] * group))
    return out.reshape(s, FOX_HEADS, n_new, HEAD_DIM)


def _to_rows(tile):
    g, d, n = tile.shape
    return tile.reshape(g * d, n).T


def _compress_pages_kernel(pt_ref, pak, pbk, wak, wbk, w2k, pav, pbv, wav, wbv, w2v, *refs, group, n_steps, ns):
    pages = refs[:group]
    kc_o, vc_o, rk_ref, rv_ref = refs[group:]
    j = pl.program_id(1)
    for i in range(group):
        first = pl.multiple_of((j * group + i) * PAGE_SIZE, PAGE_SIZE)
        rk_ref[pl.ds(first, PAGE_SIZE), :] = _to_rows(pages[i][0, 0])
        rv_ref[pl.ds(first, PAGE_SIZE), :] = _to_rows(pages[i][0, 1])

    @pl.when(j == n_steps - 1)
    def _():
        half_rows = lambda ref: (lambda r: ref[pl.ds(r, ns, stride=CMP_STRIDE), :])
        kc_o[0] = _compress_rows(half_rows(rk_ref), pak, pbk, wak, wbk, w2k, ns)
        vc_o[0] = _compress_rows(half_rows(rv_ref), pav, pbv, wav, wbv, w2v, ns)


def _compress_pages(page_table, cache_kv_t, cmp_k, cmp_v):
    s, n_pages = page_table.shape
    group = _pick(n_pages, (32, 16, 8, 4, 2, 1))
    n_steps = n_pages // group
    past = n_pages * PAGE_SIZE
    ns = past // CMP_STRIDE
    consts = list(cmp_k) + list(cmp_v)
    out = pl.BlockSpec((1, ns, LANES), lambda si, j, pt: (si, 0, 0))
    grid_spec = pltpu.PrefetchScalarGridSpec(
        num_scalar_prefetch=1, grid=(s, n_steps),
        in_specs=[_const_spec(c.shape) for c in consts]
        + _page_specs((1, 2, NSA_GROUPS, HEAD_DIM, PAGE_SIZE), group, n_pages, second=0),
        out_specs=[out, out],
        scratch_shapes=[pltpu.VMEM((past, LANES), F32), pltpu.VMEM((past, LANES), F32)])
    return pl.pallas_call(
        functools.partial(_compress_pages_kernel, group=group, n_steps=n_steps, ns=ns),
        grid_spec=grid_spec,
        out_shape=[jax.ShapeDtypeStruct((s, ns, LANES), BF16), jax.ShapeDtypeStruct((s, ns, LANES), BF16)],
        compiler_params=_params(("parallel", "arbitrary"), 48),
        name="compress_pages",
    )(page_table.reshape(-1), *consts, *([cache_kv_t] * group))


def _slc_sample_kernel(pt_ref, q_ref, sel_ref, kn_ref, vn_ref, *refs, group, n_steps, n_new, nbp, n_slc):
    pages = refs[:group]
    o_ref, m_ref, l_ref, acc_ref = refs[group:]
    j = pl.program_id(1)
    n_keys = group * PAGE_SIZE
    rows = NSA_GROUPS * NSA_HPG * n_new

    @pl.when(j == 0)
    def _():
        m_ref[...] = jnp.full_like(m_ref, M_INIT)
        l_ref[...] = jnp.zeros_like(l_ref)
        acc_ref[...] = jnp.zeros_like(acc_ref)

    def group_rows(per_query):
        return jnp.concatenate([per_query[g] for g in range(NSA_GROUPS) for _ in range(NSA_HPG)], axis=0)

    def update(s, values):
        m_prev = m_ref[...]
        m_new = jnp.maximum(m_prev, jnp.max(s, axis=1, keepdims=True))
        p = jnp.exp2(s - m_new)
        alpha = jnp.exp2(m_prev - m_new)
        l_ref[...] = alpha * l_ref[...] + jnp.sum(p, axis=1, keepdims=True)
        acc_ref[...] = alpha * acc_ref[...] + values(p.astype(BF16))
        m_ref[...] = m_new

    def tiles(c):
        return jnp.concatenate([pages[i][0, c].reshape(NSA_KV_WIDTH, PAGE_SIZE).astype(BF16) for i in range(group)],
                               axis=1)

    q = q_ref[0]
    blk_row = lax.broadcasted_iota(jnp.int32, (nbp, n_keys), 0)
    blk_key = (j * n_keys + lax.broadcasted_iota(jnp.int32, (nbp, n_keys), 1)) // SLC_LEN
    expand = (blk_row == blk_key).astype(BF16)
    bias = [jnp.dot(sel_ref[0, :, g * nbp:(g + 1) * nbp], expand, preferred_element_type=F32)
            for g in range(NSA_GROUPS)]
    update(jnp.dot(q, tiles(0), preferred_element_type=F32) + group_rows(bias),
           lambda pb: lax.dot_general(pb, tiles(1), NT_DIMS, preferred_element_type=F32))

    @pl.when(j == n_steps - 1)
    def _():
        t_q = lax.broadcasted_iota(jnp.int32, (rows, n_new), 0) % n_new
        t_k = lax.broadcasted_iota(jnp.int32, (rows, n_new), 1)
        last_blk = [sel_ref[0, :, g * nbp + n_slc - 1:g * nbp + n_slc].astype(F32) for g in range(NSA_GROUPS)]
        s = lax.dot_general(q, kn_ref[0], NT_DIMS, preferred_element_type=F32) + group_rows(last_blk)
        update(jnp.where(t_k <= t_q, s, NEG_INF), lambda pb: jnp.dot(pb, vn_ref[0], preferred_element_type=F32))
        o_ref[0] = acc_ref[...] / jnp.maximum(l_ref[...], 1e-30)


def _slc_sample(page_table, cache_kv_t, q_rows, sel, ks_new, vs_new, *, nbp, n_slc):
    s, n_pages = page_table.shape
    n_new = ks_new.shape[1]
    rows = q_rows.shape[1]
    group = _pick(n_pages, (32, 16, 8, 4, 2, 1))
    n_steps = n_pages // group
    seq = lambda shape: pl.BlockSpec(shape, lambda si, j, pt: (si, 0, 0))
    grid_spec = pltpu.PrefetchScalarGridSpec(
        num_scalar_prefetch=1, grid=(s, n_steps),
        in_specs=[seq((1, rows, LANES)), seq((1, n_new, NSA_GROUPS * nbp)), seq((1, n_new, LANES)),
                  seq((1, n_new, LANES))]
        + _page_specs((1, 2, NSA_GROUPS, HEAD_DIM, PAGE_SIZE), group, n_pages, second=1),
        out_specs=seq((1, rows, LANES)),
        scratch_shapes=[pltpu.VMEM((rows, 1), F32), pltpu.VMEM((rows, 1), F32), pltpu.VMEM((rows, LANES), F32)])
    return pl.pallas_call(
        functools.partial(_slc_sample_kernel, group=group, n_steps=n_steps, n_new=n_new, nbp=nbp, n_slc=n_slc),
        grid_spec=grid_spec,
        out_shape=jax.ShapeDtypeStruct((s, rows, LANES), F32),
        compiler_params=_params(("parallel", "arbitrary"), 40),
        name="slc_sample",
    )(page_table.reshape(-1), q_rows, sel, ks_new, vs_new, *([cache_kv_t] * group))


def _win_rows_kernel(st_ref, kwnew_ref, vwnew_ref, kw_o, vw_o, *, keep, n_new):
    for c, new_ref, o_ref in ((0, kwnew_ref, kw_o), (1, vwnew_ref, vw_o)):
        o_ref[0] = jnp.zeros(o_ref.shape[1:], BF16)
        o_ref[0, 0:keep, :] = _to_rows(st_ref[0, c]).astype(BF16)
        o_ref[0, keep:keep + n_new, :] = new_ref[0]


def _win_rows(state_t, kw_new, vw_new, t_all):
    s, keep = state_t.shape[0], state_t.shape[-1]
    n_new = kw_new.shape[1]
    seq = lambda si: (si, 0, 0)
    return pl.pallas_call(
        functools.partial(_win_rows_kernel, keep=keep, n_new=n_new),
        grid=(s,),
        in_specs=[pl.BlockSpec((1, 2, NSA_GROUPS, HEAD_DIM, keep), lambda si: (si, 0, 0, 0, 0)),
                  pl.BlockSpec((1, n_new, LANES), seq), pl.BlockSpec((1, n_new, LANES), seq)],
        out_specs=[pl.BlockSpec((1, t_all, LANES), seq), pl.BlockSpec((1, t_all, LANES), seq)],
        out_shape=[jax.ShapeDtypeStruct((s, t_all, LANES), BF16), jax.ShapeDtypeStruct((s, t_all, LANES), BF16)],
        compiler_params=_params(("parallel",), 32),
        name="win_rows",
    )(state_t, kw_new, vw_new)


def _rows_minor(a):
    return jnp.moveaxis(a, 1, -1)


def kernel(x_prompt, x_sample, cache_fox_kv, cache_fox_logf, cache_nsa_kv, state_win_kv, page_table, ln1_g, ln1_b, ffn1_w_up, ffn1_w_down, w_in, b_fgate, cmp_pos_k, cmp_wk1, cmp_wk2, cmp_pos_v, cmp_wv1, cmp_wv2, w_up_fox, w_up_nsa, w_out, ln2_g, ln2_b, ffn2_w_up, ffn2_w_down, ln3_g, ln3_b):
    depth = ln1_g.shape[0]
    assert depth == 1, "single-layer step"
    alpha = (2.0 * depth) ** 0.25
    bsz, seq, d = x_prompt.shape
    sb, n_new, _ = x_sample.shape
    n_pages = page_table.shape[1]
    past = n_pages * PAGE_SIZE
    keep = state_win_kv.shape[2]
    assert seq % 256 == 0 and past % SLC_LEN == 0 and n_new < CMP_STRIDE and keep == WINDOW
    layer = 0

    ws, bf = _split_w_in(w_in[layer], b_fgate[layer])
    cmp_k = _compress_weights(cmp_pos_k[layer], cmp_wk1[layer], cmp_wk2[layer])
    cmp_v = _compress_weights(cmp_pos_v[layer], cmp_wv1[layer], cmp_wv2[layer])

    n = bsz * seq
    xp = _ffn_ln(x_prompt.reshape(n, d), ffn1_w_up[layer], ffn1_w_down[layer], ln1_g[layer], ln1_b[layer], alpha,
                 "ffn1_prompt")
    cos_p, sin_p = _rope_tables(jnp.arange(seq, dtype=jnp.int32))
    mp = _in_proj(xp, ws, bf, cos_p, sin_p, "in_proj_prompt")
    r3 = lambda a: a.reshape(bsz, seq, a.shape[-1])

    c_p = _cumsum_time(r3(mp["lf"]).transpose(0, 2, 1), "cumsum_prompt")
    tq = _pick(seq, (512, 256))
    tk = _pick(seq, (512, 256))
    q_fox, k_fox, v_fox = _augment_fox(r3(mp["fq"]), r3(mp["fk"]), r3(mp["fv"]), c_p.transpose(0, 2, 1))
    fox_o = _attention("fox", q_fox, k_fox, v_fox, tq=tq, tk=tk, q_off=0, k_off=0, name="fox_prompt")

    nkv3 = r3(mp["nkv"])
    kc_p = _compress(nkv3, 0, cmp_k, "compress_k_prompt")
    vc_p = _compress(nkv3, 1, cmp_v, "compress_v_prompt")
    n_cmp_p = seq // CMP_STRIDE - 1
    n_slc_p = -(-seq // SLC_LEN)
    nbp_p = _round_up(n_slc_p, LANES)
    nq3 = r3(mp["nq"])
    oc_p, sel_p = _cmp_select(nq3, kc_p, vc_p, tq=256, sps=1, n_cmp=n_cmp_p, n_slc=n_slc_p, nbp=nbp_p, q_off=0,
                              name="cmp_select_prompt")
    assert n_slc_p <= LANES - HEAD_DIM
    q_nsa, k_slc, v_slc, k_win, v_win = _augment_nsa(nq3, sel_p, r3(mp["ks"]), r3(mp["vs"]), r3(mp["kw"]),
                                                      r3(mp["vw"]), nbp_p)
    os_p = _attention("slc", q_nsa, k_slc, v_slc, tq=tq, tk=tk, q_off=0, k_off=0, name="slc_prompt")
    ow_p = _attention("win", q_nsa, k_win, v_win, tq=256, tk=256, q_off=0, k_off=0, name="win_prompt")
    flat = lambda a: a.reshape(n, a.shape[-1])
    xp2 = _merge_ln(xp, flat(fox_o), flat(oc_p), flat(os_p), flat(ow_p), mp["ng"], mp["gf"], mp["gn"],
                    w_up_fox[layer], w_up_nsa[layer], w_out[layer], ln2_g[layer], ln2_b[layer], alpha, "merge_prompt")
    yp = _ffn_ln(xp2, ffn2_w_up[layer], ffn2_w_down[layer], ln3_g[layer], ln3_b[layer], alpha, "ffn2_prompt")

    ns_rows = sb * n_new
    xs = _ffn_ln(x_sample.reshape(ns_rows, d), ffn1_w_up[layer], ffn1_w_down[layer], ln1_g[layer], ln1_b[layer],
                 alpha, "ffn1_sample")
    cos_s, sin_s = _rope_tables(past + jnp.arange(n_new, dtype=jnp.int32))
    tile_rows = _pick(ns_rows, (256, 128, 8))
    reps = tile_rows // n_new
    ms = _in_proj(xs, ws, bf, jnp.tile(cos_s, (reps, 1)), jnp.tile(sin_s, (reps, 1)), "in_proj_sample")
    s3 = lambda a: a.reshape(sb, n_new, a.shape[-1])
    by_head = lambda a: s3(a).reshape(sb, n_new, FOX_HEADS, HEAD_DIM).transpose(0, 2, 1, 3)

    lf_past = _lf_pages(page_table, _rows_minor(cache_fox_logf[layer]))
    lf_new = s3(ms["lf"]).transpose(0, 2, 1)
    t_cs = _round_up(past + n_new, 8 * LANES)
    c_s = _cumsum_time(jnp.pad(jnp.concatenate([lf_past, lf_new], axis=2), ((0, 0), (0, 0), (0, t_cs - past - n_new))),
                       "cumsum_sample") * LOG2E
    c_new = c_s[:, :, past:past + n_new]
    fox_o_s = _fox_sample(page_table, _rows_minor(cache_fox_kv[layer]), by_head(ms["fq"]), by_head(ms["fk"]),
                          by_head(ms["fv"]), c_new[..., None], c_s, c_new)
    fox_o_s = fox_o_s.transpose(0, 2, 1, 3).reshape(ns_rows, FOX_WIDTH)

    nsa_t = _rows_minor(cache_nsa_kv[layer])
    kc_s, vc_s = _compress_pages(page_table, nsa_t, cmp_k, cmp_v)
    n_cmp_s = (past + n_new) // CMP_STRIDE - 1
    n_slc_s = -(-(past + n_new) // SLC_LEN)
    nbp_s = _round_up(n_slc_s, LANES)
    nq_s = s3(ms["nq"])
    sps = _pick(sb, tuple(c for c in (16, 8, 4, 2, 1) if c * n_new <= LANES))
    oc_s, sel_s = _cmp_select(nq_s, kc_s, vc_s, tq=n_new, sps=sps, n_cmp=n_cmp_s, n_slc=n_slc_s, nbp=nbp_s,
                              q_off=past, name="cmp_select_sample")
    q_ghtd = nq_s.reshape(sb, n_new, NSA_HPG, NSA_GROUPS, HEAD_DIM).transpose(0, 3, 2, 1, 4)
    zero = jnp.zeros_like(q_ghtd[:, 0])
    q_rows = jnp.concatenate([jnp.concatenate([q_ghtd[:, 0], zero], axis=-1),
                              jnp.concatenate([zero, q_ghtd[:, 1]], axis=-1)], axis=1).reshape(sb, -1, LANES)
    o_rows = _slc_sample(page_table, nsa_t, q_rows, sel_s, s3(ms["ks"]), s3(ms["vs"]), nbp=nbp_s, n_slc=n_slc_s)
    o_rows = o_rows.reshape(sb, NSA_GROUPS, NSA_HPG, n_new, LANES)
    os_s = jnp.stack([o_rows[:, 0, :, :, :HEAD_DIM], o_rows[:, 1, :, :, HEAD_DIM:]], axis=3)
    os_s = os_s.transpose(0, 2, 1, 3, 4).reshape(sb, n_new, NSA_WIDTH)
    t_win = _round_up(keep + n_new, LANES)
    kw_all, vw_all = _win_rows(_rows_minor(state_win_kv[layer]), s3(ms["kw"]), s3(ms["vw"]), t_win)
    ow_s = _attention("win", _augment(_head_major(nq_s, NSA_HEADS), []),
                      _augment(_head_major(kw_all, NSA_GROUPS), []), _augment(_head_major(vw_all, NSA_GROUPS), [1.0]),
                      tq=n_new, tk=t_win, q_off=past, k_off=past - keep, name="win_sample")
    flat_s = lambda a: a.reshape(ns_rows, a.shape[-1])
    xs2 = _merge_ln(xs, fox_o_s, flat_s(oc_s), flat_s(os_s), flat_s(ow_s), ms["ng"], ms["gf"], ms["gn"],
                    w_up_fox[layer], w_up_nsa[layer], w_out[layer], ln2_g[layer], ln2_b[layer], alpha, "merge_sample")
    ys = _ffn_ln(xs2, ffn2_w_up[layer], ffn2_w_down[layer], ln3_g[layer], ln3_b[layer], alpha, "ffn2_sample")

    fox_kv_p = mp["fkv"].reshape(1, bsz, seq, 2, FOX_HEADS, HEAD_DIM)
    fox_kv_s = ms["fkv"].reshape(1, sb, n_new, 2, FOX_HEADS, HEAD_DIM)
    logf_p = mp["lf"].reshape(1, bsz, seq, FOX_HEADS)
    logf_s = ms["lf"].reshape(1, sb, n_new, FOX_HEADS)
    nsa_kv_p = mp["nkv"].reshape(1, bsz, seq, 4, NSA_GROUPS, HEAD_DIM)
    nsa_kv_s = ms["nkv"].reshape(1, sb, n_new, 4, NSA_GROUPS, HEAD_DIM)
    win_rows_p = r3(mp["win"]).reshape(bsz, seq, 2, NSA_GROUPS, HEAD_DIM)
    if seq >= keep:
        win_p = win_rows_p[:, seq - keep:]
    else:
        win_p = jnp.pad(win_rows_p, ((0, 0), (keep - seq, 0), (0, 0), (0, 0), (0, 0)))
    new_win = ms["win"].reshape(sb, n_new, 2, NSA_GROUPS, HEAD_DIM).astype(state_win_kv.dtype)
    win_s = jnp.concatenate([state_win_kv[layer], new_win], axis=1)[:, -keep:]
    return (yp.reshape(bsz, seq, d), ys.reshape(sb, n_new, d), fox_kv_p, fox_kv_s, logf_p, logf_s,
            nsa_kv_p, nsa_kv_s, win_p[None], win_s[None])
```

```python
import functools

import numpy as np
import jax
import jax.numpy as jnp
from jax import lax
from jax.experimental import pallas as pl
from jax.experimental.pallas import tpu as pltpu

F32 = jnp.float32
BF16 = jnp.bfloat16

HEAD_DIM = 64
FOX_HEADS = 8
NSA_HEADS = 8
NSA_GROUPS = 2
NSA_HPG = NSA_HEADS // NSA_GROUPS
FOX_WIDTH = FOX_HEADS * HEAD_DIM
NSA_WIDTH = NSA_HEADS * HEAD_DIM
NSA_KV_WIDTH = NSA_GROUPS * HEAD_DIM
N_NSA_BRANCHES = 3
CMP_LEN = 32
CMP_STRIDE = 16
CMP_HIDDEN = 128
SLC_LEN = 64
N_SELECT = 16
WINDOW = 512
PAGE_SIZE = 128
ROPE_THETA = 10000.0
LN_EPS = 1e-5
FORCE_BONUS = 1e4
NEG_INF = -1e30
M_INIT = 0.1 * NEG_INF
LOG2E = 1.4426950408889634
QSCALE = HEAD_DIM ** -0.5 * LOG2E
SUM_LANE = HEAD_DIM

LANES = 128
N_PAIRS = 4
MIB = 1024 * 1024
HIGHEST = lax.Precision.HIGHEST
NT_DIMS = (((1,), (1,)), ((), ()))


def _params(semantics, vmem_mib):
    return pltpu.CompilerParams(dimension_semantics=semantics, vmem_limit_bytes=vmem_mib * MIB)


def _pick(n, candidates):
    for c in candidates:
        if n % c == 0:
            return c
    return n


def _round_up(n, m):
    return (n + m - 1) // m * m


def _layer_norm(y, g, b):
    mu = jnp.mean(y, axis=-1, keepdims=True)
    d = y - mu
    var = jnp.mean(d * d, axis=-1, keepdims=True)
    return d * lax.rsqrt(var + LN_EPS) * g + b


def _const_spec(shape):
    nd = len(shape)
    return pl.BlockSpec(shape, lambda *_: (0,) * nd, pipeline_mode=pl.Buffered(1))


def _ffn_ln_kernel(x_ref, wa_ref, wb_ref, wd_ref, g_ref, b_ref, o_ref, acc_ref, *, alpha, n_chunks):
    x = x_ref[...]
    xb = x.astype(BF16)
    acc_ref[...] = jnp.zeros_like(acc_ref)

    def body(c, carry):
        a = jnp.dot(xb, wa_ref[c], preferred_element_type=F32)
        b = jnp.dot(xb, wb_ref[c], preferred_element_type=F32)
        h = (a * jax.nn.sigmoid(a) * b).astype(BF16)
        acc_ref[...] += jnp.dot(h, wd_ref[c], preferred_element_type=F32)
        return carry

    lax.fori_loop(0, n_chunks, body, 0)
    o_ref[...] = _layer_norm(alpha * x + 0.5 * acc_ref[...], g_ref[...], b_ref[...])


def _ffn_ln(x, w_up, w_down, g, b, alpha, name):
    n, d = x.shape
    f = w_down.shape[0]
    fc = _pick(f, (256, 128))
    nc = f // fc
    wa = w_up[:, :f].astype(BF16).reshape(d, nc, fc).transpose(1, 0, 2)
    wb = w_up[:, f:].astype(BF16).reshape(d, nc, fc).transpose(1, 0, 2)
    wd = w_down.astype(BF16).reshape(nc, fc, d)
    tm = _pick(n, (1024, 512, 256, 128, 8))
    return pl.pallas_call(
        functools.partial(_ffn_ln_kernel, alpha=alpha, n_chunks=nc),
        grid=(n // tm,),
        in_specs=[pl.BlockSpec((tm, d), lambda i: (i, 0)),
                  _const_spec((nc, d, fc)), _const_spec((nc, d, fc)), _const_spec((nc, fc, d)),
                  _const_spec((1, d)), _const_spec((1, d))],
        out_specs=pl.BlockSpec((tm, d), lambda i: (i, 0)),
        out_shape=jax.ShapeDtypeStruct((n, d), F32),
        scratch_shapes=[pltpu.VMEM((tm, d), F32)],
        compiler_params=_params(("parallel",), 56),
        name=name,
    )(x, wa, wb, wd, g.reshape(1, d), b.reshape(1, d))


def _rope_tables(pos):
    half = HEAD_DIM // 2
    inv_freq = ROPE_THETA ** (-jnp.arange(half, dtype=F32) / half)
    ang = pos.astype(F32)[:, None] * inv_freq[None, :]
    cos, sin = jnp.cos(ang), jnp.sin(ang)
    cos64 = jnp.concatenate([cos, cos], axis=-1)
    sin64 = jnp.concatenate([-sin, sin], axis=-1)
    return jnp.tile(cos64, (1, LANES // HEAD_DIM)), jnp.tile(sin64, (1, LANES // HEAD_DIM))


def _in_proj_kernel(x_ref, wfq, wfkv, wsm, wnq, wnkv, wwin, wgf, wgn, bf_ref, cos_ref, sin_ref,
                    fq_o, fkv_o, fk_o, fv_o, lf_o, nq_o, nkv_o, ks_o, vs_o, win_o, kw_o, vw_o,
                    ng_o, gf_o, gn_o):
    xb = x_ref[...].astype(BF16)
    cos = cos_ref[...]
    sin = sin_ref[...]
    lane = lax.broadcasted_iota(jnp.int32, (1, LANES), 1)
    low_half = (lane % HEAD_DIM) < (HEAD_DIM // 2)

    def rope(v):
        partner = jnp.where(low_half, pltpu.roll(v, LANES - HEAD_DIM // 2, 1), pltpu.roll(v, HEAD_DIM // 2, 1))
        return v * cos + partner * sin

    def proj(w_ref):
        return jnp.dot(xb, w_ref[...], preferred_element_type=F32)

    fq_o[...] = (proj(wfq) * QSCALE).astype(BF16)
    fkv = proj(wfkv)
    fkv_o[...] = fkv
    fk_o[...] = fkv[:, :FOX_WIDTH].astype(BF16)
    fv_o[...] = fkv[:, FOX_WIDTH:].astype(BF16)

    sm = proj(wsm)
    z = sm[:, :LANES] + bf_ref[...]
    logf = jnp.minimum(z, 0.0) - jnp.log1p(jnp.exp(-jnp.abs(z)))
    lf_o[...] = logf[:, :FOX_HEADS]
    ng_o[...] = sm[:, LANES:]

    nq = proj(wnq)
    for r in range(N_PAIRS):
        nq_o[:, r * LANES:(r + 1) * LANES] = (rope(nq[:, r * LANES:(r + 1) * LANES]) * QSCALE).astype(BF16)

    nkv = proj(wnkv)
    k_cmp = rope(nkv[:, 0:LANES])
    k_slc = rope(nkv[:, 2 * LANES:3 * LANES])
    v_slc = nkv[:, 3 * LANES:4 * LANES]
    nkv_o[:, 0:LANES] = k_cmp
    nkv_o[:, LANES:2 * LANES] = nkv[:, LANES:2 * LANES]
    nkv_o[:, 2 * LANES:3 * LANES] = k_slc
    nkv_o[:, 3 * LANES:4 * LANES] = v_slc
    ks_o[...] = k_slc.astype(BF16)
    vs_o[...] = v_slc.astype(BF16)

    win = proj(wwin)
    k_win = rope(win[:, :LANES])
    v_win = win[:, LANES:]
    win_o[:, :LANES] = k_win
    win_o[:, LANES:] = v_win
    kw_o[...] = k_win.astype(BF16)
    vw_o[...] = v_win.astype(BF16)

    gf_o[...] = proj(wgf)
    gn_o[...] = proj(wgn)


def _nsa_perm():
    perm = np.zeros(NSA_WIDTH, np.int32)
    for r in range(NSA_HPG):
        for g in range(NSA_GROUPS):
            for d in range(HEAD_DIM):
                perm[r * LANES + g * HEAD_DIM + d] = (g * NSA_HPG + r) * HEAD_DIM + d
    return perm


def _split_w_in(w_in, b_fgate):
    d = w_in.shape[0]
    sizes = (FOX_WIDTH, FOX_WIDTH, FOX_WIDTH, FOX_HEADS, NSA_WIDTH) + (NSA_KV_WIDTH,) * 6 + (
        N_NSA_BRANCHES * NSA_HEADS, d, d)
    offs = np.concatenate([[0], np.cumsum(sizes)])
    col = lambda i, j=None: w_in[:, offs[i]:offs[(i if j is None else j) + 1]]
    pad = lambda w: jnp.pad(w, ((0, 0), (0, LANES - w.shape[1])))
    ws = dict(
        wfq=col(0), wfkv=col(1, 2),
        wsm=jnp.concatenate([pad(col(3)), pad(col(11))], axis=1),
        wnq=col(4)[:, _nsa_perm()], wnkv=col(5, 8), wwin=col(9, 10), wgf=col(12), wgn=col(13))
    ws = {k: v.astype(BF16) for k, v in ws.items()}
    bf = jnp.pad(b_fgate.astype(F32), (0, LANES - FOX_HEADS)).reshape(1, LANES)
    return ws, bf


def _in_proj(x, ws, bf, cos_tab, sin_tab, name):
    n, d = x.shape
    n_tab = cos_tab.shape[0]
    tm = _pick(n_tab, (256, 128, 8))
    tab_tiles = n_tab // tm
    row = lambda w: pl.BlockSpec((tm, w), lambda i: (i, 0))
    tab = pl.BlockSpec((tm, LANES), lambda i: (i % tab_tiles, 0))
    names = ("wfq", "wfkv", "wsm", "wnq", "wnkv", "wwin", "wgf", "wgn")
    outs = [("fq", FOX_WIDTH, BF16), ("fkv", 2 * FOX_WIDTH, F32), ("fk", FOX_WIDTH, BF16), ("fv", FOX_WIDTH, BF16),
            ("lf", FOX_HEADS, F32), ("nq", NSA_WIDTH, BF16), ("nkv", 4 * NSA_KV_WIDTH, F32),
            ("ks", NSA_KV_WIDTH, BF16), ("vs", NSA_KV_WIDTH, BF16), ("win", 2 * NSA_KV_WIDTH, F32),
            ("kw", NSA_KV_WIDTH, BF16), ("vw", NSA_KV_WIDTH, BF16), ("ng", LANES, F32), ("gf", d, F32), ("gn", d, F32)]
    res = pl.pallas_call(
        _in_proj_kernel,
        grid=(n // tm,),
        in_specs=[row(d)] + [_const_spec(ws[k].shape) for k in names] + [_const_spec((1, LANES)), tab, tab],
        out_specs=[row(w) for _, w, _ in outs],
        out_shape=[jax.ShapeDtypeStruct((n, w), dt) for _, w, dt in outs],
        compiler_params=_params(("parallel",), 56),
        name=name,
    )(x, *[ws[k] for k in names], bf, cos_tab, sin_tab)
    return {k: v for (k, _, _), v in zip(outs, res)}


def _cumsum_kernel(x_ref, tri_ref, low_ref, o_ref, *, n_heads, r):
    x = x_ref[0]
    within = lax.dot_general(x, tri_ref[...], (((1,), (0,)), ((), ())), precision=HIGHEST, preferred_element_type=F32)
    tot = jnp.broadcast_to(within[:, LANES - 1:LANES], within.shape)
    before = jnp.concatenate(
        [lax.dot_general(low_ref[...], tot[h * r:(h + 1) * r], (((1,), (0,)), ((), ())), precision=HIGHEST,
                         preferred_element_type=F32) for h in range(n_heads)], axis=0)
    o_ref[0] = within + before


def _cumsum_time(logf_t, name):
    b, h, t = logf_t.shape
    r = t // LANES
    rows = h * r
    idx = np.arange(LANES)
    tri = jnp.asarray((idx[:, None] <= idx[None, :]).astype(np.float32))
    assert r % 8 == 0
    ridx = np.arange(r)
    low = jnp.asarray((ridx[None, :] < ridx[:, None]).astype(np.float32))
    out = pl.pallas_call(
        functools.partial(_cumsum_kernel, n_heads=h, r=r),
        grid=(b,),
        in_specs=[pl.BlockSpec((1, rows, LANES), lambda i: (i, 0, 0)), _const_spec((LANES, LANES)),
                  _const_spec((r, r))],
        out_specs=pl.BlockSpec((1, rows, LANES), lambda i: (i, 0, 0)),
        out_shape=jax.ShapeDtypeStruct((b, rows, LANES), F32),
        compiler_params=_params(("parallel",), 32),
        name=name,
    )(logf_t.reshape(b, rows, LANES), tri, low)
    return out.reshape(b, h, t)


FLAG_FIRST, FLAG_LAST, FLAG_EDGE = 1, 2, 4


def _tile_schedule(mode, nq, nkt, tq, tk, q_off, k_off):
    qi_tab, kt_tab, flag_tab = [], [], []
    for qi in range(nq):
        q_lo, q_hi = q_off + qi * tq, q_off + (qi + 1) * tq - 1
        tiles = []
        for kt in range(nkt):
            k_lo, k_hi = k_off + kt * tk, k_off + (kt + 1) * tk - 1
            if mode == "win":
                visible = (q_hi - k_lo >= 0) and (q_lo - k_hi < WINDOW)
                interior = (q_lo - k_hi >= 0) and (q_hi - k_lo < WINDOW)
            else:
                visible = k_lo <= q_hi
                interior = k_hi <= q_lo
            if visible:
                tiles.append((kt, 0 if interior else FLAG_EDGE))
        assert tiles, "every query tile sees at least one key tile"
        for n, (kt, flag) in enumerate(tiles):
            qi_tab.append(qi)
            kt_tab.append(kt)
            flag_tab.append(flag | (FLAG_FIRST if n == 0 else 0) | (FLAG_LAST if n == len(tiles) - 1 else 0))
    as_i32 = lambda v: jnp.asarray(np.asarray(v, np.int32))
    return as_i32(qi_tab), as_i32(kt_tab), as_i32(flag_tab)


def _attn_kernel(qi_tab, kt_tab, flag_tab, q_ref, k_ref, v_ref, o_ref, m_ref, acc_ref, *,
                 mode, tq, tk, nh, n_kv, q_off, k_off):
    step = pl.program_id(1)
    qi = qi_tab[step]
    kt = kt_tab[step]
    flags = flag_tab[step]
    lane = lax.broadcasted_iota(jnp.int32, (1, LANES), 1)
    n_chunks = tk // LANES

    @pl.when((flags & FLAG_FIRST) != 0)
    def _():
        m_ref[...] = jnp.full_like(m_ref, M_INIT)
        acc_ref[...] = jnp.zeros_like(acc_ref)

    def tile(edge):
        if edge:
            q_pos = q_off + qi * tq + lax.broadcasted_iota(jnp.int32, (tq, 1), 0)
        scores = lambda i: lax.dot_general(q_ref[0, i], k_ref[0, i % n_kv], NT_DIMS, preferred_element_type=F32)

        def values(i, alpha, pb):
            acc_ref[i] = alpha * acc_ref[i] + jnp.dot(pb, v_ref[0, i % n_kv], preferred_element_type=F32)

        s_next = scores(0)
        pending = None
        for i in range(nh):
            s = s_next
            if i + 1 < nh:
                s_next = scores(i + 1)
            if pending is not None:
                values(*pending)
            chunks = []
            for c in range(n_chunks):
                t = s[:, c * LANES:(c + 1) * LANES]
                if edge:
                    k_pos = k_off + kt * tk + c * LANES + lane
                    if mode == "win":
                        rel = q_pos - k_pos
                        t = jnp.where((rel >= 0) & (rel < WINDOW), t, NEG_INF)
                    else:
                        t = jnp.where(k_pos <= q_pos, t, NEG_INF)
                chunks.append(t)
            mx = chunks[0]
            for t in chunks[1:]:
                mx = jnp.maximum(mx, t)
            m_prev = m_ref[i]
            m_new = jnp.maximum(m_prev, jnp.max(mx, axis=1, keepdims=True))
            alpha = jnp.exp2(m_prev - m_new)
            ps = [jnp.exp2(t - m_new).astype(BF16) for t in chunks]
            pb = jnp.concatenate(ps, axis=1) if n_chunks > 1 else ps[0]
            m_ref[i] = m_new
            pending = (i, alpha, pb)
        values(*pending)

    pl.when((flags & FLAG_EDGE) != 0)(lambda: tile(True))
    pl.when((flags & FLAG_EDGE) == 0)(lambda: tile(False))

    @pl.when((flags & FLAG_LAST) != 0)
    def _():
        for pr in range(nh // 2):
            outs = []
            for side in range(2):
                acc = acc_ref[2 * pr + side]
                outs.append(acc / jnp.maximum(acc[:, SUM_LANE:SUM_LANE + 1], 1e-30))
            o_ref[0, :, pr * LANES:(pr + 1) * LANES] = jnp.where(lane < HEAD_DIM, outs[0],
                                                                  pltpu.roll(outs[1], HEAD_DIM, 1))


def _attention(mode, q, k, v, *, tq, tk, q_off, k_off, name):
    b, nh, t_q, _ = q.shape
    n_kv, t_k = k.shape[1], k.shape[2]
    qi_tab, kt_tab, flag_tab = _tile_schedule(mode, t_q // tq, t_k // tk, tq, tk, q_off, k_off)
    n_steps = qi_tab.shape[0]
    q_spec = pl.BlockSpec((1, nh, tq, LANES), lambda bi, s, qt, kt, fl: (bi, 0, qt[s], 0))
    kv_spec = pl.BlockSpec((1, n_kv, tk, LANES), lambda bi, s, qt, kt, fl: (bi, 0, kt[s], 0))
    grid_spec = pltpu.PrefetchScalarGridSpec(
        num_scalar_prefetch=3, grid=(b, n_steps), in_specs=[q_spec, kv_spec, kv_spec],
        out_specs=pl.BlockSpec((1, tq, nh // 2 * LANES), lambda bi, s, qt, kt, fl: (bi, qt[s], 0)),
        scratch_shapes=[pltpu.VMEM((nh, tq, LANES), F32), pltpu.VMEM((nh, tq, LANES), F32)])
    return pl.pallas_call(
        functools.partial(_attn_kernel, mode=mode, tq=tq, tk=tk, nh=nh, n_kv=n_kv, q_off=q_off, k_off=k_off),
        grid_spec=grid_spec,
        out_shape=jax.ShapeDtypeStruct((b, t_q, nh // 2 * LANES), F32),
        compiler_params=_params(("parallel", "arbitrary"), 48),
        name=name,
    )(qi_tab, kt_tab, flag_tab, q, k, v)


def _limbs(x):
    hi = x.astype(BF16).astype(F32)
    mid = (x - hi).astype(BF16).astype(F32)
    lo = (x - hi - mid).astype(BF16).astype(F32)
    return hi, mid, lo


def _head_of_pair(pair, side):
    return pair if side == 0 else pltpu.roll(pair, HEAD_DIM, 1)


def _augment_fox_kernel(q_ref, k_ref, v_ref, c_ref, qa_ref, ka_ref, va_ref):
    lane = lax.broadcasted_iota(jnp.int32, (1, LANES), 1)
    data = lane < HEAD_DIM
    c = c_ref[0] * LOG2E
    for h in range(FOX_HEADS):
        p, side = divmod(h, 2)
        hi, mid, lo = _limbs(c[:, h:h + 1])
        q_extra = jnp.where(lane < HEAD_DIM + 3, 1.0,
                            jnp.where(lane == HEAD_DIM + 3, hi,
                                      jnp.where(lane == HEAD_DIM + 4, mid, jnp.where(lane == HEAD_DIM + 5, lo, 0.0))))
        k_extra = jnp.where(lane == HEAD_DIM, -hi,
                            jnp.where(lane == HEAD_DIM + 1, -mid,
                                      jnp.where(lane == HEAD_DIM + 2, -lo, jnp.where(lane < HEAD_DIM + 6, 1.0, 0.0))))
        v_extra = jnp.where(lane == SUM_LANE, 1.0, 0.0)
        cols = slice(p * LANES, (p + 1) * LANES)
        qa_ref[0, h] = jnp.where(data, _head_of_pair(q_ref[0, :, cols].astype(F32), side), q_extra).astype(BF16)
        ka_ref[0, h] = jnp.where(data, _head_of_pair(k_ref[0, :, cols].astype(F32), side), k_extra).astype(BF16)
        va_ref[0, h] = jnp.where(data, _head_of_pair(v_ref[0, :, cols].astype(F32), side), v_extra).astype(BF16)


def _augment_fox(q, k, v, c_col):
    b, t, w = q.shape
    tr = _pick(t, (512, 256, 128, 8))
    row = pl.BlockSpec((1, tr, w), lambda bi, i: (bi, i, 0))
    out = pl.BlockSpec((1, FOX_HEADS, tr, LANES), lambda bi, i: (bi, 0, i, 0))
    shape = jax.ShapeDtypeStruct((b, FOX_HEADS, t, LANES), BF16)
    return pl.pallas_call(
        _augment_fox_kernel,
        grid=(b, t // tr),
        in_specs=[row, row, row, pl.BlockSpec((1, tr, FOX_HEADS), lambda bi, i: (bi, i, 0))],
        out_specs=[out, out, out],
        out_shape=[shape, shape, shape],
        compiler_params=_params(("parallel", "parallel"), 32),
        name="augment_fox",
    )(q, k, v, c_col)


def _augment_nsa_kernel(nq_ref, sel_ref, ks_ref, vs_ref, kw_ref, vw_ref, qa_ref, ksa_ref, vsa_ref, kwa_ref, vwa_ref,
                        *, tr, nbp):
    lane = lax.broadcasted_iota(jnp.int32, (1, LANES), 1)
    data = lane < HEAD_DIM
    t_row = pl.program_id(1) * tr + lax.broadcasted_iota(jnp.int32, (tr, 1), 0)
    block_hot = jnp.where(t_row // SLC_LEN == lane - HEAD_DIM, 1.0, 0.0)
    ones_lane = jnp.where(lane == SUM_LANE, 1.0, 0.0)
    for g in range(NSA_GROUPS):
        bias = pltpu.roll(sel_ref[0, :, g * nbp:g * nbp + LANES].astype(F32), HEAD_DIM, 1)
        for r in range(NSA_HPG):
            qp = nq_ref[0, :, r * LANES:(r + 1) * LANES].astype(F32)
            qa_ref[0, NSA_GROUPS * r + g] = jnp.where(data, _head_of_pair(qp, g), bias).astype(BF16)
        ksa_ref[0, g] = jnp.where(data, _head_of_pair(ks_ref[0].astype(F32), g), block_hot).astype(BF16)
        vsa_ref[0, g] = jnp.where(data, _head_of_pair(vs_ref[0].astype(F32), g), ones_lane).astype(BF16)
        kwa_ref[0, g] = jnp.where(data, _head_of_pair(kw_ref[0].astype(F32), g), 0.0).astype(BF16)
        vwa_ref[0, g] = jnp.where(data, _head_of_pair(vw_ref[0].astype(F32), g), ones_lane).astype(BF16)


def _augment_nsa(nq, sel, ks, vs, kw, vw, nbp):
    b, t, w = nq.shape
    tr = _pick(t, (512, 256, 128, 8))
    row = lambda width: pl.BlockSpec((1, tr, width), lambda bi, i: (bi, i, 0))
    out = lambda heads: pl.BlockSpec((1, heads, tr, LANES), lambda bi, i: (bi, 0, i, 0))
    shape = lambda heads: jax.ShapeDtypeStruct((b, heads, t, LANES), BF16)
    return pl.pallas_call(
        functools.partial(_augment_nsa_kernel, tr=tr, nbp=nbp),
        grid=(b, t // tr),
        in_specs=[row(w), row(NSA_GROUPS * nbp)] + [row(LANES)] * 4,
        out_specs=[out(NSA_HEADS)] + [out(NSA_GROUPS)] * 4,
        out_shape=[shape(NSA_HEADS)] + [shape(NSA_GROUPS)] * 4,
        compiler_params=_params(("parallel", "parallel"), 32),
        name="augment_nsa",
    )(nq, sel, ks, vs, kw, vw)


def _head_major(a, n_heads):
    b, t, _ = a.shape
    return a.reshape(b, t, n_heads, HEAD_DIM).transpose(0, 2, 1, 3)


def _augment(head_rows, extras):
    b, h, t, _ = head_rows.shape
    cols = [jnp.broadcast_to(e, (b, h, t))[..., None].astype(BF16) for e in extras]
    pad = jnp.zeros((b, h, t, LANES - HEAD_DIM - len(cols)), BF16)
    return jnp.concatenate([head_rows] + cols + [pad], axis=-1)


def _compress_rows(row_of_half, pa_ref, pb_ref, wa_ref, wb_ref, w2_ref, ns):
    u = jnp.concatenate([row_of_half(r) for r in range(CMP_STRIDE)], axis=1)
    first = jnp.dot((u + pa_ref[...]).astype(BF16), wa_ref[...], preferred_element_type=F32)
    second = jnp.dot((u + pb_ref[...]).astype(BF16), wb_ref[...], preferred_element_type=F32)
    pre = first + pltpu.roll(second, ns - 1, 0)
    h = (pre * jax.nn.sigmoid(pre)).astype(BF16)
    return jnp.dot(h, w2_ref[...], preferred_element_type=F32).astype(BF16)


def _compress_kernel(rows_ref, pa_ref, pb_ref, wa_ref, wb_ref, w2_ref, o_ref, *, ns):
    o_ref[0] = _compress_rows(lambda r: rows_ref[0, pl.ds(r, ns, stride=CMP_STRIDE), :],
                              pa_ref, pb_ref, wa_ref, wb_ref, w2_ref, ns)


def _compress_weights(pos, w1, w2):
    ratio = CMP_LEN // CMP_STRIDE
    assert ratio == 2
    w1r = w1.reshape(CMP_LEN, HEAD_DIM, CMP_HIDDEN)
    zeros = jnp.zeros((CMP_STRIDE, HEAD_DIM, CMP_HIDDEN), w1.dtype)

    def half(rows):
        g0 = jnp.concatenate([jnp.stack([rows, zeros], axis=1).reshape(-1, CMP_HIDDEN),
                              jnp.stack([zeros, rows], axis=1).reshape(-1, CMP_HIDDEN)], axis=1)
        return g0.astype(BF16)

    def pos_tab(p):
        return jnp.stack([p, p], axis=1).reshape(1, -1).astype(F32)

    z2 = jnp.zeros_like(w2)
    w2d = jnp.concatenate([jnp.concatenate([w2, z2], axis=1), jnp.concatenate([z2, w2], axis=1)], axis=0)
    return (pos_tab(pos[:CMP_STRIDE]), pos_tab(pos[CMP_STRIDE:]), half(w1r[:CMP_STRIDE]), half(w1r[CMP_STRIDE:]),
            w2d.astype(BF16))


def _compress(rows, col_block, weights, name):
    b, t, _ = rows.shape
    ns = t // CMP_STRIDE
    width = CMP_STRIDE * LANES
    pa, pb, wa, wb, w2d = weights
    return pl.pallas_call(
        functools.partial(_compress_kernel, ns=ns),
        grid=(b,),
        in_specs=[pl.BlockSpec((1, t, LANES), lambda i: (i, 0, col_block)), _const_spec((1, width)),
                  _const_spec((1, width)), _const_spec(wa.shape), _const_spec(wb.shape), _const_spec(w2d.shape)],
        out_specs=pl.BlockSpec((1, ns, LANES), lambda i: (i, 0, 0)),
        out_shape=jax.ShapeDtypeStruct((b, ns, LANES), BF16),
        compiler_params=_params(("parallel",), 48),
        name=name,
    )(rows, pa, pb, wa, wb, w2d)


def _cmp_select_kernel(q_ref, kc_ref, vc_ref, cov_ref, oc_ref, sel_ref, imp_ref, vt_ref, *,
                       sps, tq, rows_p, nc, n_cmp, n_slc, nbp, q_off, n_sel):
    qi = pl.program_id(1)
    lane = lax.broadcasted_iota(jnp.int32, (1, LANES), 1)
    left = lane < HEAD_DIM
    q_pos = q_off + qi * tq + lax.broadcasted_iota(jnp.int32, (tq, 1), 0)
    n_idx = lax.broadcasted_iota(jnp.int32, (1, nc), 1)
    cmask = ((n_idx * CMP_STRIDE + CMP_LEN - 1) <= q_pos) & (n_idx < n_cmp)
    cov = cov_ref[...]
    if rows_p > sps * tq:
        imp_ref[...] = jnp.zeros_like(imp_ref)

    def attend(sq, carry):
        kc = kc_ref[sq]
        vc = vc_ref[sq]
        imp = [jnp.zeros((tq, nbp), F32) for _ in range(NSA_GROUPS)]
        heads = [(r, g) for r in range(N_PAIRS) for g in range(NSA_GROUPS)]

        def scores(r, g):
            qp = q_ref[sq, :, r * LANES:(r + 1) * LANES]
            qh = jnp.where(left if g == 0 else jnp.logical_not(left), qp, jnp.zeros_like(qp))
            return lax.dot_general(qh, kc, NT_DIMS, preferred_element_type=F32)

        halves = []
        s_next = scores(*heads[0])
        for n, (r, g) in enumerate(heads):
            s = s_next
            if n + 1 < len(heads):
                s_next = scores(*heads[n + 1])
            s = jnp.where(cmask, s, NEG_INF)
            m = jnp.maximum(jnp.max(s, axis=1, keepdims=True), M_INIT)
            e = jnp.exp2(s - m)
            inv = 1.0 / jnp.maximum(jnp.sum(e, axis=1, keepdims=True), 1e-30)
            pb = (e * inv).astype(BF16)
            halves.append(jnp.dot(pb, vc, preferred_element_type=F32))
            imp[g] = imp[g] + jnp.dot(pb, cov, preferred_element_type=F32)
            if g == NSA_GROUPS - 1:
                oc_ref[sq, :, r * LANES:(r + 1) * LANES] = jnp.where(left, halves[-2], halves[-1])
        first_row = pl.multiple_of(sq * tq, tq)
        for g in range(NSA_GROUPS):
            imp_ref[g, pl.ds(first_row, tq), :] = imp[g]
        return carry

    if sps == 1:
        attend(0, 0)
    else:
        lax.fori_loop(0, sps, attend, 0)

    pos = q_off + qi * tq + lax.broadcasted_iota(jnp.int32, (rows_p, 1), 0) % tq
    blk = lax.broadcasted_iota(jnp.int32, (1, nbp), 1)
    cur = pos // SLC_LEN
    forced = (blk == 0) | (blk == cur) | (blk == cur - 1)
    valid = (blk * SLC_LEN <= pos) & (blk < n_slc)
    nbr = _round_up(n_slc, 8)
    row = lax.broadcasted_iota(jnp.int32, (nbr, rows_p), 0)
    n_live = jnp.minimum(n_slc, (q_off + (qi + 1) * tq - 1) // SLC_LEN + 1)
    for g in range(NSA_GROUPS):
        val = jnp.where(valid, jnp.where(forced, imp_ref[g] + FORCE_BONUS, imp_ref[g]), NEG_INF)
        vt_ref[...] = val.T

        def body(i, cnt):
            vi = vt_ref[pl.ds(i, 1), :]
            vt = vt_ref[0:nbr, :]
            ahead = (vi > vt) | ((vi == vt) & (i < row))
            return cnt + ahead.astype(F32)

        cnt = lax.fori_loop(0, n_live, body, jnp.zeros((nbr, rows_p), F32))
        bias = jnp.where(cnt < n_sel, 0.0, NEG_INF)
        if nbp > nbr:
            bias = jnp.concatenate([bias, jnp.full((nbp - nbr, rows_p), NEG_INF, F32)], axis=0)
        bias_t = bias.T
        for sq in range(sps):
            sel_ref[sq, :, g * nbp:(g + 1) * nbp] = bias_t[sq * tq:(sq + 1) * tq].astype(BF16)


def _coverage(n_cmp, nc, n_slc, nbp):
    c0 = np.arange(nc) * CMP_STRIDE
    s0 = np.arange(nbp) * SLC_LEN
    lo = np.maximum(c0[:, None], s0[None, :])
    hi = np.minimum(c0[:, None] + CMP_LEN, s0[None, :] + SLC_LEN)
    cov = np.maximum(hi - lo, 0).astype(np.float32) / CMP_LEN
    cov[n_cmp:, :] = 0.0
    cov[:, n_slc:] = 0.0
    return jnp.asarray(cov, dtype=BF16)


def _cmp_select(q, kc, vc, *, tq, sps, n_cmp, n_slc, nbp, q_off, name):
    b, t_q, _ = q.shape
    nc = kc.shape[1]
    rows_p = _round_up(sps * tq, LANES)
    cov = _coverage(n_cmp, nc, n_slc, nbp)
    kern = functools.partial(_cmp_select_kernel, sps=sps, tq=tq, rows_p=rows_p, nc=nc, n_cmp=n_cmp, n_slc=n_slc,
                             nbp=nbp, q_off=q_off, n_sel=min(N_SELECT, n_slc))
    return pl.pallas_call(
        kern,
        grid=(b // sps, t_q // tq),
        in_specs=[pl.BlockSpec((sps, tq, N_PAIRS * LANES), lambda bi, qi: (bi, qi, 0)),
                  pl.BlockSpec((sps, nc, LANES), lambda bi, qi: (bi, 0, 0)),
                  pl.BlockSpec((sps, nc, LANES), lambda bi, qi: (bi, 0, 0)),
                  _const_spec((nc, nbp))],
        out_specs=[pl.BlockSpec((sps, tq, N_PAIRS * LANES), lambda bi, qi: (bi, qi, 0)),
                   pl.BlockSpec((sps, tq, NSA_GROUPS * nbp), lambda bi, qi: (bi, qi, 0))],
        out_shape=[jax.ShapeDtypeStruct((b, t_q, N_PAIRS * LANES), F32),
                   jax.ShapeDtypeStruct((b, t_q, NSA_GROUPS * nbp), BF16)],
        scratch_shapes=[pltpu.VMEM((NSA_GROUPS, rows_p, nbp), F32), pltpu.VMEM((nbp, rows_p), F32)],
        compiler_params=_params(("parallel", "parallel"), 40),
        name=name,
    )(q, kc, vc, cov)


def _merge_kernel(x_ref, fo_ref, oc_ref, os_ref, ow_ref, ng_ref, gf_ref, gn_ref, wuf_ref, wun_ref, wout_ref,
                  eg_ref, g_ref, b_ref, o_ref, *, alpha):
    gates = jax.nn.sigmoid(ng_ref[...])
    gx = lax.dot_general(gates, eg_ref[...], (((1,), (0,)), ((), ())), precision=HIGHEST, preferred_element_type=F32)
    w = NSA_WIDTH
    nsa_o = gx[:, :w] * oc_ref[...] + gx[:, w:2 * w] * os_ref[...] + gx[:, 2 * w:] * ow_ref[...]
    up_f = jnp.dot(fo_ref[...].astype(BF16), wuf_ref[...], preferred_element_type=F32)
    up_n = jnp.dot(nsa_o.astype(BF16), wun_ref[...], preferred_element_type=F32)
    mixed = jax.nn.sigmoid(gf_ref[...]) * up_f + jax.nn.sigmoid(gn_ref[...]) * up_n
    mix = jnp.dot(mixed.astype(BF16), wout_ref[...], preferred_element_type=F32)
    o_ref[...] = _layer_norm(alpha * x_ref[...] + mix, g_ref[...], b_ref[...])


def _gate_expand():
    perm = _nsa_perm()
    e = np.zeros((LANES, N_NSA_BRANCHES * NSA_WIDTH), np.float32)
    for br in range(N_NSA_BRANCHES):
        for pos in range(NSA_WIDTH):
            head = perm[pos] // HEAD_DIM
            e[br * NSA_HEADS + head, br * NSA_WIDTH + pos] = 1.0
    return jnp.asarray(e)


def _merge_ln(x, fox_o, o_c, o_s, o_w, ng, gf, gn, w_up_fox, w_up_nsa, w_out, g, b, alpha, name):
    n, d = x.shape
    tm = _pick(n, (256, 128, 8))
    row = lambda w: pl.BlockSpec((tm, w), lambda i: (i, 0))
    wuf = w_up_fox.astype(BF16)
    wun = w_up_nsa[_nsa_perm(), :].astype(BF16)
    wout = w_out.astype(BF16)
    eg = _gate_expand()
    return pl.pallas_call(
        functools.partial(_merge_kernel, alpha=alpha),
        grid=(n // tm,),
        in_specs=[row(d), row(FOX_WIDTH), row(NSA_WIDTH), row(NSA_WIDTH), row(NSA_WIDTH), row(LANES), row(d), row(d),
                  _const_spec(wuf.shape), _const_spec(wun.shape), _const_spec(wout.shape), _const_spec(eg.shape),
                  _const_spec((1, d)), _const_spec((1, d))],
        out_specs=row(d),
        out_shape=jax.ShapeDtypeStruct((n, d), F32),
        compiler_params=_params(("parallel",), 48),
        name=name,
    )(x, fox_o, o_c, o_s, o_w, ng, gf, gn, wuf, wun, wout, eg, g.reshape(1, d), b.reshape(1, d))


def _page_specs(block, group, n_pages, second=0):
    nd = len(block)

    def spec(slot):
        return pl.BlockSpec(block, lambda si, j, pt: (pt[si * n_pages + jnp.minimum(j * group + slot, n_pages - 1)],
                                                      second) + (0,) * (nd - 2))

    return [spec(slot) for slot in range(group)]


def _lf_pages_kernel(pt_ref, *refs, group):
    o_ref = refs[group]
    for i in range(group):
        o_ref[0, :, i * PAGE_SIZE:(i + 1) * PAGE_SIZE] = refs[i][0]


def _lf_pages(page_table, cache_lf_t):
    s, n_pages = page_table.shape
    group = _pick(n_pages, (64, 32, 16, 8, 4, 2, 1))
    grid_spec = pltpu.PrefetchScalarGridSpec(
        num_scalar_prefetch=1, grid=(s, n_pages // group),
        in_specs=_page_specs((1, FOX_HEADS, PAGE_SIZE), group, n_pages),
        out_specs=pl.BlockSpec((1, FOX_HEADS, group * PAGE_SIZE), lambda si, j, pt: (si, 0, j)))
    return pl.pallas_call(
        functools.partial(_lf_pages_kernel, group=group),
        grid_spec=grid_spec,
        out_shape=jax.ShapeDtypeStruct((s, FOX_HEADS, n_pages * PAGE_SIZE), F32),
        compiler_params=_params(("parallel", "arbitrary"), 32),
        name="lf_pages",
    )(page_table.reshape(-1), *([cache_lf_t] * group))


def _fox_sample_kernel(pt_ref, q_ref, cc_ref, cr_ref, crn_ref, kn_ref, vn_ref, *refs, group, n_steps, n_new):
    pages = refs[:group]
    o_ref, m_ref, l_ref, acc_ref = refs[group:]
    j = pl.program_id(1)

    @pl.when(j == 0)
    def _():
        m_ref[...] = jnp.full_like(m_ref, M_INIT)
        l_ref[...] = jnp.zeros_like(l_ref)
        acc_ref[...] = jnp.zeros_like(acc_ref)

    def update(s, value_dots):
        m_prev = m_ref[...]
        m_new = jnp.maximum(m_prev, jnp.max(s, axis=1, keepdims=True))
        p = jnp.exp2(s - m_new)
        alpha = jnp.exp2(m_prev - m_new)
        l_ref[...] = alpha * l_ref[...] + jnp.sum(p, axis=1, keepdims=True)
        acc_ref[...] = alpha * acc_ref[...] + jnp.concatenate(
            [value_dots(h, p[h * n_new:(h + 1) * n_new].astype(BF16)) for h in range(FOX_HEADS)], axis=0)
        m_ref[...] = m_new

    def tiles(kv, h):
        return jnp.concatenate([pages[i][0, kv, h].astype(BF16) for i in range(group)], axis=1)

    scores = [jnp.dot(q_ref[0, h], tiles(0, h), preferred_element_type=F32) + (cc_ref[0, h] - cr_ref[0, h:h + 1, :])
              for h in range(FOX_HEADS)]
    update(jnp.concatenate(scores, axis=0),
           lambda h, pb: lax.dot_general(pb, tiles(1, h), NT_DIMS, preferred_element_type=F32))

    @pl.when(j == n_steps - 1)
    def _():
        t_q = lax.broadcasted_iota(jnp.int32, (n_new, n_new), 0)
        t_k = lax.broadcasted_iota(jnp.int32, (n_new, n_new), 1)
        new_scores = []
        for h in range(FOX_HEADS):
            s = lax.dot_general(q_ref[0, h], kn_ref[0, h], NT_DIMS, preferred_element_type=F32)
            new_scores.append(jnp.where(t_k <= t_q, s + (cc_ref[0, h] - crn_ref[0, h:h + 1, :]), NEG_INF))
        update(jnp.concatenate(new_scores, axis=0),
               lambda h, pb: jnp.dot(pb, vn_ref[0, h], preferred_element_type=F32))
        o_ref[0] = acc_ref[...] / jnp.maximum(l_ref[...], 1e-30)


def _fox_sample(page_table, cache_kv_t, q_h, k_new_h, v_new_h, c_q, c_row, c_new):
    s, n_pages = page_table.shape
    n_new = q_h.shape[2]
    group = _pick(n_pages, (16, 8, 4, 2, 1))
    n_steps = n_pages // group
    rows = FOX_HEADS * n_new
    seq4 = lambda shape: pl.BlockSpec(shape, lambda si, j, pt: (si, 0, 0, 0))
    grid_spec = pltpu.PrefetchScalarGridSpec(
        num_scalar_prefetch=1, grid=(s, n_steps),
        in_specs=[seq4((1, FOX_HEADS, n_new, HEAD_DIM)), seq4((1, FOX_HEADS, n_new, 1)),
                  pl.BlockSpec((1, FOX_HEADS, group * PAGE_SIZE), lambda si, j, pt: (si, 0, j)),
                  pl.BlockSpec((1, FOX_HEADS, n_new), lambda si, j, pt: (si, 0, 0)),
                  seq4((1, FOX_HEADS, n_new, HEAD_DIM)), seq4((1, FOX_HEADS, n_new, HEAD_DIM))]
        + _page_specs((1, 2, FOX_HEADS, HEAD_DIM, PAGE_SIZE), group, n_pages),
        out_specs=pl.BlockSpec((1, rows, HEAD_DIM), lambda si, j, pt: (si, 0, 0)),
        scratch_shapes=[pltpu.VMEM((rows, 1), F32), pltpu.VMEM((rows, 1), F32), pltpu.VMEM((rows, HEAD_DIM), F32)])
    out = pl.pallas_call(
        functools.partial(_fox_sample_kernel, group=group, n_steps=n_steps, n_new=n_new),
        grid_spec=grid_spec,
        out_shape=jax.ShapeDtypeStruct((s, rows, HEAD_DIM), F32),
        compiler_params=_params(("parallel", "arbitrary"), 48),
        name="fox_sample",
    )(page_table.reshape(-1), q_h, c_q, c_row, c_new, k_new_h, v_new_h, *([cache_kv_t] * group))
    return out.reshape(s, FOX_HEADS, n_new, HEAD_DIM)


def _to_rows(tile):
    g, d, n = tile.shape
    return tile.reshape(g * d, n).T


def _compress_pages_kernel(pt_ref, pak, pbk, wak, wbk, w2k, pav, pbv, wav, wbv, w2v, *refs, group, n_steps, ns):
    pages = refs[:group]
    kc_o, vc_o, rk_ref, rv_ref = refs[group:]
    j = pl.program_id(1)
    for i in range(group):
        first = pl.multiple_of((j * group + i) * PAGE_SIZE, PAGE_SIZE)
        rk_ref[pl.ds(first, PAGE_SIZE), :] = _to_rows(pages[i][0, 0])
        rv_ref[pl.ds(first, PAGE_SIZE), :] = _to_rows(pages[i][0, 1])

    @pl.when(j == n_steps - 1)
    def _():
        half_rows = lambda ref: (lambda r: ref[pl.ds(r, ns, stride=CMP_STRIDE), :])
        kc_o[0] = _compress_rows(half_rows(rk_ref), pak, pbk, wak, wbk, w2k, ns)
        vc_o[0] = _compress_rows(half_rows(rv_ref), pav, pbv, wav, wbv, w2v, ns)


def _compress_pages(page_table, cache_kv_t, cmp_k, cmp_v):
    s, n_pages = page_table.shape
    group = _pick(n_pages, (64, 32, 16, 8, 4, 2, 1))
    n_steps = n_pages // group
    past = n_pages * PAGE_SIZE
    ns = past // CMP_STRIDE
    consts = list(cmp_k) + list(cmp_v)
    out = pl.BlockSpec((1, ns, LANES), lambda si, j, pt: (si, 0, 0))
    grid_spec = pltpu.PrefetchScalarGridSpec(
        num_scalar_prefetch=1, grid=(s, n_steps),
        in_specs=[_const_spec(c.shape) for c in consts]
        + _page_specs((1, 2, NSA_GROUPS, HEAD_DIM, PAGE_SIZE), group, n_pages, second=0),
        out_specs=[out, out],
        scratch_shapes=[pltpu.VMEM((past, LANES), F32), pltpu.VMEM((past, LANES), F32)])
    return pl.pallas_call(
        functools.partial(_compress_pages_kernel, group=group, n_steps=n_steps, ns=ns),
        grid_spec=grid_spec,
        out_shape=[jax.ShapeDtypeStruct((s, ns, LANES), BF16), jax.ShapeDtypeStruct((s, ns, LANES), BF16)],
        compiler_params=_params(("parallel", "arbitrary"), 48),
        name="compress_pages",
    )(page_table.reshape(-1), *consts, *([cache_kv_t] * group))


def _slc_sample_kernel(pt_ref, q_ref, sel_ref, kn_ref, vn_ref, *refs, group, n_steps, n_new, nbp, n_slc):
    pages = refs[:group]
    o_ref, m_ref, l_ref, acc_ref = refs[group:]
    j = pl.program_id(1)
    n_keys = group * PAGE_SIZE
    rows = NSA_GROUPS * NSA_HPG * n_new

    @pl.when(j == 0)
    def _():
        m_ref[...] = jnp.full_like(m_ref, M_INIT)
        l_ref[...] = jnp.zeros_like(l_ref)
        acc_ref[...] = jnp.zeros_like(acc_ref)

    def group_rows(per_query):
        return jnp.concatenate([per_query[g] for g in range(NSA_GROUPS) for _ in range(NSA_HPG)], axis=0)

    def update(s, values):
        m_prev = m_ref[...]
        m_new = jnp.maximum(m_prev, jnp.max(s, axis=1, keepdims=True))
        p = jnp.exp2(s - m_new)
        alpha = jnp.exp2(m_prev - m_new)
        l_ref[...] = alpha * l_ref[...] + jnp.sum(p, axis=1, keepdims=True)
        acc_ref[...] = alpha * acc_ref[...] + values(p.astype(BF16))
        m_ref[...] = m_new

    def tiles(c):
        return jnp.concatenate([pages[i][0, c].reshape(NSA_KV_WIDTH, PAGE_SIZE).astype(BF16) for i in range(group)],
                               axis=1)

    q = q_ref[0]
    blk_row = lax.broadcasted_iota(jnp.int32, (nbp, n_keys), 0)
    blk_key = (j * n_keys + lax.broadcasted_iota(jnp.int32, (nbp, n_keys), 1)) // SLC_LEN
    expand = (blk_row == blk_key).astype(BF16)
    bias = [jnp.dot(sel_ref[0, :, g * nbp:(g + 1) * nbp], expand, preferred_element_type=F32)
            for g in range(NSA_GROUPS)]
    update(jnp.dot(q, tiles(0), preferred_element_type=F32) + group_rows(bias),
           lambda pb: lax.dot_general(pb, tiles(1), NT_DIMS, preferred_element_type=F32))

    @pl.when(j == n_steps - 1)
    def _():
        t_q = lax.broadcasted_iota(jnp.int32, (rows, n_new), 0) % n_new
        t_k = lax.broadcasted_iota(jnp.int32, (rows, n_new), 1)
        last_blk = [sel_ref[0, :, g * nbp + n_slc - 1:g * nbp + n_slc].astype(F32) for g in range(NSA_GROUPS)]
        s = lax.dot_general(q, kn_ref[0], NT_DIMS, preferred_element_type=F32) + group_rows(last_blk)
        update(jnp.where(t_k <= t_q, s, NEG_INF), lambda pb: jnp.dot(pb, vn_ref[0], preferred_element_type=F32))
        o_ref[0] = acc_ref[...] / jnp.maximum(l_ref[...], 1e-30)


def _slc_sample(page_table, cache_kv_t, q_rows, sel, ks_new, vs_new, *, nbp, n_slc):
    s, n_pages = page_table.shape
    n_new = ks_new.shape[1]
    rows = q_rows.shape[1]
    group = _pick(n_pages, (32, 16, 8, 4, 2, 1))
    n_steps = n_pages // group
    seq = lambda shape: pl.BlockSpec(shape, lambda si, j, pt: (si, 0, 0))
    grid_spec = pltpu.PrefetchScalarGridSpec(
        num_scalar_prefetch=1, grid=(s, n_steps),
        in_specs=[seq((1, rows, LANES)), seq((1, n_new, NSA_GROUPS * nbp)), seq((1, n_new, LANES)),
                  seq((1, n_new, LANES))]
        + _page_specs((1, 2, NSA_GROUPS, HEAD_DIM, PAGE_SIZE), group, n_pages, second=1),
        out_specs=seq((1, rows, LANES)),
        scratch_shapes=[pltpu.VMEM((rows, 1), F32), pltpu.VMEM((rows, 1), F32), pltpu.VMEM((rows, LANES), F32)])
    return pl.pallas_call(
        functools.partial(_slc_sample_kernel, group=group, n_steps=n_steps, n_new=n_new, nbp=nbp, n_slc=n_slc),
        grid_spec=grid_spec,
        out_shape=jax.ShapeDtypeStruct((s, rows, LANES), F32),
        compiler_params=_params(("parallel", "arbitrary"), 40),
        name="slc_sample",
    )(page_table.reshape(-1), q_rows, sel, ks_new, vs_new, *([cache_kv_t] * group))


def _win_rows_kernel(st_ref, kwnew_ref, vwnew_ref, kw_o, vw_o, *, keep, n_new):
    for c, new_ref, o_ref in ((0, kwnew_ref, kw_o), (1, vwnew_ref, vw_o)):
        o_ref[0] = jnp.zeros(o_ref.shape[1:], BF16)
        o_ref[0, 0:keep, :] = _to_rows(st_ref[0, c]).astype(BF16)
        o_ref[0, keep:keep + n_new, :] = new_ref[0]


def _win_rows(state_t, kw_new, vw_new, t_all):
    s, keep = state_t.shape[0], state_t.shape[-1]
    n_new = kw_new.shape[1]
    seq = lambda si: (si, 0, 0)
    return pl.pallas_call(
        functools.partial(_win_rows_kernel, keep=keep, n_new=n_new),
        grid=(s,),
        in_specs=[pl.BlockSpec((1, 2, NSA_GROUPS, HEAD_DIM, keep), lambda si: (si, 0, 0, 0, 0)),
                  pl.BlockSpec((1, n_new, LANES), seq), pl.BlockSpec((1, n_new, LANES), seq)],
        out_specs=[pl.BlockSpec((1, t_all, LANES), seq), pl.BlockSpec((1, t_all, LANES), seq)],
        out_shape=[jax.ShapeDtypeStruct((s, t_all, LANES), BF16), jax.ShapeDtypeStruct((s, t_all, LANES), BF16)],
        compiler_params=_params(("parallel",), 32),
        name="win_rows",
    )(state_t, kw_new, vw_new)


def _rows_minor(a):
    return jnp.moveaxis(a, 1, -1)


def kernel(x_prompt, x_sample, cache_fox_kv, cache_fox_logf, cache_nsa_kv, state_win_kv, page_table, ln1_g, ln1_b, ffn1_w_up, ffn1_w_down, w_in, b_fgate, cmp_pos_k, cmp_wk1, cmp_wk2, cmp_pos_v, cmp_wv1, cmp_wv2, w_up_fox, w_up_nsa, w_out, ln2_g, ln2_b, ffn2_w_up, ffn2_w_down, ln3_g, ln3_b):
    depth = ln1_g.shape[0]
    assert depth == 1, "single-layer step"
    alpha = (2.0 * depth) ** 0.25
    bsz, seq, d = x_prompt.shape
    sb, n_new, _ = x_sample.shape
    n_pages = page_table.shape[1]
    past = n_pages * PAGE_SIZE
    keep = state_win_kv.shape[2]
    assert seq % 256 == 0 and past % SLC_LEN == 0 and n_new < CMP_STRIDE and keep == WINDOW
    layer = 0

    ws, bf = _split_w_in(w_in[layer], b_fgate[layer])
    cmp_k = _compress_weights(cmp_pos_k[layer], cmp_wk1[layer], cmp_wk2[layer])
    cmp_v = _compress_weights(cmp_pos_v[layer], cmp_wv1[layer], cmp_wv2[layer])

    n = bsz * seq
    xp = _ffn_ln(x_prompt.reshape(n, d), ffn1_w_up[layer], ffn1_w_down[layer], ln1_g[layer], ln1_b[layer], alpha,
                 "ffn1_prompt")
    cos_p, sin_p = _rope_tables(jnp.arange(seq, dtype=jnp.int32))
    mp = _in_proj(xp, ws, bf, cos_p, sin_p, "in_proj_prompt")
    r3 = lambda a: a.reshape(bsz, seq, a.shape[-1])

    c_p = _cumsum_time(r3(mp["lf"]).transpose(0, 2, 1), "cumsum_prompt")
    tq = _pick(seq, (512, 256))
    tk = _pick(seq, (512, 256))
    q_fox, k_fox, v_fox = _augment_fox(r3(mp["fq"]), r3(mp["fk"]), r3(mp["fv"]), c_p.transpose(0, 2, 1))
    fox_o = _attention("fox", q_fox, k_fox, v_fox, tq=tq, tk=tk, q_off=0, k_off=0, name="fox_prompt")

    nkv3 = r3(mp["nkv"])
    kc_p = _compress(nkv3, 0, cmp_k, "compress_k_prompt")
    vc_p = _compress(nkv3, 1, cmp_v, "compress_v_prompt")
    n_cmp_p = seq // CMP_STRIDE - 1
    n_slc_p = -(-seq // SLC_LEN)
    nbp_p = _round_up(n_slc_p, LANES)
    nq3 = r3(mp["nq"])
    oc_p, sel_p = _cmp_select(nq3, kc_p, vc_p, tq=256, sps=1, n_cmp=n_cmp_p, n_slc=n_slc_p, nbp=nbp_p, q_off=0,
                              name="cmp_select_prompt")
    assert n_slc_p <= LANES - HEAD_DIM
    q_nsa, k_slc, v_slc, k_win, v_win = _augment_nsa(nq3, sel_p, r3(mp["ks"]), r3(mp["vs"]), r3(mp["kw"]),
                                                      r3(mp["vw"]), nbp_p)
    os_p = _attention("slc", q_nsa, k_slc, v_slc, tq=tq, tk=tk, q_off=0, k_off=0, name="slc_prompt")
    ow_p = _attention("win", q_nsa, k_win, v_win, tq=256, tk=256, q_off=0, k_off=0, name="win_prompt")
    flat = lambda a: a.reshape(n, a.shape[-1])
    xp2 = _merge_ln(xp, flat(fox_o), flat(oc_p), flat(os_p), flat(ow_p), mp["ng"], mp["gf"], mp["gn"],
                    w_up_fox[layer], w_up_nsa[layer], w_out[layer], ln2_g[layer], ln2_b[layer], alpha, "merge_prompt")
    yp = _ffn_ln(xp2, ffn2_w_up[layer], ffn2_w_down[layer], ln3_g[layer], ln3_b[layer], alpha, "ffn2_prompt")

    ns_rows = sb * n_new
    xs = _ffn_ln(x_sample.reshape(ns_rows, d), ffn1_w_up[layer], ffn1_w_down[layer], ln1_g[layer], ln1_b[layer],
                 alpha, "ffn1_sample")
    cos_s, sin_s = _rope_tables(past + jnp.arange(n_new, dtype=jnp.int32))
    tile_rows = _pick(ns_rows, (256, 128, 8))
    reps = tile_rows // n_new
    ms = _in_proj(xs, ws, bf, jnp.tile(cos_s, (reps, 1)), jnp.tile(sin_s, (reps, 1)), "in_proj_sample")
    s3 = lambda a: a.reshape(sb, n_new, a.shape[-1])
    by_head = lambda a: s3(a).reshape(sb, n_new, FOX_HEADS, HEAD_DIM).transpose(0, 2, 1, 3)

    lf_past = _lf_pages(page_table, _rows_minor(cache_fox_logf[layer]))
    lf_new = s3(ms["lf"]).transpose(0, 2, 1)
    t_cs = _round_up(past + n_new, 8 * LANES)
    c_s = _cumsum_time(jnp.pad(jnp.concatenate([lf_past, lf_new], axis=2), ((0, 0), (0, 0), (0, t_cs - past - n_new))),
                       "cumsum_sample") * LOG2E
    c_new = c_s[:, :, past:past + n_new]
    fox_o_s = _fox_sample(page_table, _rows_minor(cache_fox_kv[layer]), by_head(ms["fq"]), by_head(ms["fk"]),
                          by_head(ms["fv"]), c_new[..., None], c_s, c_new)
    fox_o_s = fox_o_s.transpose(0, 2, 1, 3).reshape(ns_rows, FOX_WIDTH)

    nsa_t = _rows_minor(cache_nsa_kv[layer])
    kc_s, vc_s = _compress_pages(page_table, nsa_t, cmp_k, cmp_v)
    n_cmp_s = (past + n_new) // CMP_STRIDE - 1
    n_slc_s = -(-(past + n_new) // SLC_LEN)
    nbp_s = _round_up(n_slc_s, LANES)
    nq_s = s3(ms["nq"])
    sps = _pick(sb, tuple(c for c in (16, 8, 4, 2, 1) if c * n_new <= LANES))
    oc_s, sel_s = _cmp_select(nq_s, kc_s, vc_s, tq=n_new, sps=sps, n_cmp=n_cmp_s, n_slc=n_slc_s, nbp=nbp_s,
                              q_off=past, name="cmp_select_sample")
    q_ghtd = nq_s.reshape(sb, n_new, NSA_HPG, NSA_GROUPS, HEAD_DIM).transpose(0, 3, 2, 1, 4)
    zero = jnp.zeros_like(q_ghtd[:, 0])
    q_rows = jnp.concatenate([jnp.concatenate([q_ghtd[:, 0], zero], axis=-1),
                              jnp.concatenate([zero, q_ghtd[:, 1]], axis=-1)], axis=1).reshape(sb, -1, LANES)
    o_rows = _slc_sample(page_table, nsa_t, q_rows, sel_s, s3(ms["ks"]), s3(ms["vs"]), nbp=nbp_s, n_slc=n_slc_s)
    o_rows = o_rows.reshape(sb, NSA_GROUPS, NSA_HPG, n_new, LANES)
    os_s = jnp.stack([o_rows[:, 0, :, :, :HEAD_DIM], o_rows[:, 1, :, :, HEAD_DIM:]], axis=3)
    os_s = os_s.transpose(0, 2, 1, 3, 4).reshape(sb, n_new, NSA_WIDTH)
    t_win = _round_up(keep + n_new, LANES)
    kw_all, vw_all = _win_rows(_rows_minor(state_win_kv[layer]), s3(ms["kw"]), s3(ms["vw"]), t_win)
    ow_s = _attention("win", _augment(_head_major(nq_s, NSA_HEADS), []),
                      _augment(_head_major(kw_all, NSA_GROUPS), []), _augment(_head_major(vw_all, NSA_GROUPS), [1.0]),
                      tq=n_new, tk=t_win, q_off=past, k_off=past - keep, name="win_sample")
    flat_s = lambda a: a.reshape(ns_rows, a.shape[-1])
    xs2 = _merge_ln(xs, fox_o_s, flat_s(oc_s), flat_s(os_s), flat_s(ow_s), ms["ng"], ms["gf"], ms["gn"],
                    w_up_fox[layer], w_up_nsa[layer], w_out[layer], ln2_g[layer], ln2_b[layer], alpha, "merge_sample")
    ys = _ffn_ln(xs2, ffn2_w_up[layer], ffn2_w_down[layer], ln3_g[layer], ln3_b[layer], alpha, "ffn2_sample")

    fox_kv_p = mp["fkv"].reshape(1, bsz, seq, 2, FOX_HEADS, HEAD_DIM)
    fox_kv_s = ms["fkv"].reshape(1, sb, n_new, 2, FOX_HEADS, HEAD_DIM)
    logf_p = mp["lf"].reshape(1, bsz, seq, FOX_HEADS)
    logf_s = ms["lf"].reshape(1, sb, n_new, FOX_HEADS)
    nsa_kv_p = mp["nkv"].reshape(1, bsz, seq, 4, NSA_GROUPS, HEAD_DIM)
    nsa_kv_s = ms["nkv"].reshape(1, sb, n_new, 4, NSA_GROUPS, HEAD_DIM)
    win_rows_p = r3(mp["win"]).reshape(bsz, seq, 2, NSA_GROUPS, HEAD_DIM)
    if seq >= keep:
        win_p = win_rows_p[:, seq - keep:]
    else:
        win_p = jnp.pad(win_rows_p, ((0, 0), (keep - seq, 0), (0, 0), (0, 0), (0, 0)))
    new_win = ms["win"].reshape(sb, n_new, 2, NSA_GROUPS, HEAD_DIM).astype(state_win_kv.dtype)
    win_s = jnp.concatenate([state_win_kv[layer], new_win], axis=1)[:, -keep:]
    return (yp.reshape(bsz, seq, d), ys.reshape(sb, n_new, d), fox_kv_p, fox_kv_s, logf_p, logf_s,
            nsa_kv_p, nsa_kv_s, win_p[None], win_s[None])
```
